```python
import math
import jax, jax.numpy as jnp
from jax import lax
import numpy as np

D_MODEL = 1024
BATCH = 4
SEQ = 4096
DEPTH = 4

MEM_LEN = 256
EPS = 1e-6

GLA_HEADS = 4
GLA_DV = 96
GLA_DK = 48
GLA_QK = GLA_HEADS * GLA_DK
GLA_WIDTH = GLA_HEADS * GLA_DV
GLA_GATE_RANK = 16
GLA_TAU = 16.0
GLA_CHUNK = 64

CONV_WIDTH = 256
CONV_K = 3

MLA_HEADS = 6
MLA_NOPE = 64
MLA_ROPE = 32
MLA_V = 64
MLA_Q_RANK = 256
MLA_KV_RANK = 256
MLA_WIDTH = MLA_HEADS * MLA_V
ROPE_BASE = 10000.0
Q_BLOCK = 128

D_MIX = GLA_WIDTH + CONV_WIDTH + MLA_WIDTH

MEM_HEADS = 4
MEM_HEAD_DIM = 128
MEM_INNER = MEM_HEADS * MEM_HEAD_DIM

IN_SPLITS = (GLA_QK, GLA_QK, GLA_WIDTH, GLA_GATE_RANK, GLA_WIDTH,
             CONV_WIDTH, CONV_WIDTH, CONV_WIDTH, CONV_WIDTH,
             MLA_Q_RANK, MLA_KV_RANK, MLA_ROPE, MLA_WIDTH)
IN_WIDTH = sum(IN_SPLITS)

kernel_name = 'hybrid_gla_shortconv_mla_memory_trunk'


def _split_points():
    pts, acc = [], 0
    for s in IN_SPLITS[:-1]:
        acc += s
        pts.append(acc)
    return pts


def rms_norm(x, g):
    xf = x.astype(jnp.float32)
    y = xf * lax.rsqrt(jnp.mean(xf * xf, axis=-1, keepdims=True) + EPS)
    return (y * g.astype(jnp.float32)).astype(x.dtype)


def rope_tables(positions):
    inv_freq = 1.0 / (ROPE_BASE ** (jnp.arange(0, MLA_ROPE, 2, dtype=jnp.float32) / MLA_ROPE))
    ang = positions.astype(jnp.float32)[..., None] * inv_freq
    return jnp.cos(ang), jnp.sin(ang)


def apply_rope(x, cos, sin):
    half = MLA_ROPE // 2
    xf = x.astype(jnp.float32)
    x1, x2 = xf[..., :half], xf[..., half:]
    c, s = cos[:, :, None, :], sin[:, :, None, :]
    return jnp.concatenate([x1 * c - x2 * s, x2 * c + x1 * s], axis=-1).astype(x.dtype)


def gla_chunked(q, k, v, log_a):
    B, S, H, dk = q.shape
    dv = v.shape[-1]
    C = GLA_CHUNK
    N = S // C

    def chunks(t):
        return t.astype(jnp.float32).reshape(B, N, C, H, t.shape[-1]).transpose(0, 3, 1, 2, 4)

    qf = chunks(q) * (dk ** -0.5)
    kf, vf, la = chunks(k), chunks(v), chunks(log_a)
    b = jnp.cumsum(la, axis=3)
    b_last = b[:, :, :, -1:, :]
    q_dec = qf * jnp.exp(b)
    k_inv = kf * jnp.exp(-b)
    k_end = kf * jnp.exp(b_last - b)
    mask = jnp.tril(jnp.ones((C, C), dtype=bool))
    A = jnp.einsum('bhncd,bhnjd->bhncj', q_dec, k_inv)
    A = jnp.where(mask, A, 0.0)
    o_intra = jnp.einsum('bhncj,bhnjv->bhncv', A, vf)
    U = jnp.einsum('bhncd,bhncv->bhndv', k_end, vf)
    decay = jnp.exp(b_last[:, :, :, 0, :])

    def step(state, inp):
        u, d = inp
        return d[..., None] * state + u, state

    init = jnp.zeros((B, H, dk, dv), jnp.float32)
    _, s_prev = lax.scan(step, init, (U.transpose(2, 0, 1, 3, 4), decay.transpose(2, 0, 1, 3)))
    s_prev = s_prev.transpose(1, 2, 0, 3, 4)
    o_inter = jnp.einsum('bhncd,bhndv->bhncv', q_dec, s_prev)
    o = o_intra + o_inter
    return o.transpose(0, 2, 3, 1, 4).reshape(B, S, H, dv)


def causal_attention_blocks(q, k, v, scale):
    B, S, H, D = q.shape
    Dv = v.shape[-1]
    nb = S // Q_BLOCK
    qb = q.reshape(B, nb, Q_BLOCK, H, D).transpose(1, 0, 2, 3, 4)
    key_pos = jnp.arange(S)

    def one_block(args):
        qblk, start = args
        s = jnp.einsum('bqhd,bkhd->bhqk', qblk, k).astype(jnp.float32) * scale
        qpos = start + jnp.arange(Q_BLOCK)
        allowed = key_pos[None, :] <= qpos[:, None]
        s = jnp.where(allowed, s, -1e30)
        p = jax.nn.softmax(s, axis=-1).astype(v.dtype)
        return jnp.einsum('bhqk,bkhd->bqhd', p, v)

    out = lax.map(one_block, (qb, jnp.arange(nb) * Q_BLOCK))
    return out.transpose(1, 0, 2, 3, 4).reshape(B, S, H, Dv)


def hybrid_mixer(xn, cos, sin, w_in, gla_w_gate, gla_b_gate, gla_norm, conv_w,
                 mla_q_norm, mla_w_uq, mla_kv_norm, mla_w_ukv, w_out):
    B, S, _ = xn.shape
    proj = xn @ w_in
    (gq, gk, gv, glr, ggate, cc, cb, ch, cgate,
     cq, ckv, kr, mgate) = jnp.split(proj, _split_points(), axis=-1)

    z = glr @ gla_w_gate + gla_b_gate
    log_a = (jax.nn.log_sigmoid(z.astype(jnp.float32)) / GLA_TAU).reshape(B, S, GLA_HEADS, GLA_DK)
    o_gla = gla_chunked(gq.reshape(B, S, GLA_HEADS, GLA_DK),
                        gk.reshape(B, S, GLA_HEADS, GLA_DK),
                        gv.reshape(B, S, GLA_HEADS, GLA_DV), log_a)
    o_gla = rms_norm(o_gla, gla_norm).astype(xn.dtype).reshape(B, S, GLA_WIDTH)
    o_gla = o_gla * jax.nn.silu(ggate)

    u = cc * ch
    conv = lax.conv_general_dilated(u, conv_w[:, None, :].astype(u.dtype), window_strides=(1,),
                                    padding=[(CONV_K - 1, 0)],
                                    dimension_numbers=('NWC', 'WIO', 'NWC'),
                                    feature_group_count=CONV_WIDTH)
    o_conv = cb * conv * jax.nn.silu(cgate)

    qh = (rms_norm(cq, mla_q_norm) @ mla_w_uq).reshape(B, S, MLA_HEADS, MLA_NOPE + MLA_ROPE)
    q_full = jnp.concatenate([qh[..., :MLA_NOPE], apply_rope(qh[..., MLA_NOPE:], cos, sin)], axis=-1)
    kvh = (rms_norm(ckv, mla_kv_norm) @ mla_w_ukv).reshape(B, S, MLA_HEADS, MLA_NOPE + MLA_V)
    k_rope = apply_rope(kr.reshape(B, S, 1, MLA_ROPE), cos, sin)
    k_full = jnp.concatenate([kvh[..., :MLA_NOPE],
                              jnp.broadcast_to(k_rope, (B, S, MLA_HEADS, MLA_ROPE))], axis=-1)
    v_mla = kvh[..., MLA_NOPE:]
    o_mla = causal_attention_blocks(q_full, k_full, v_mla, 1.0 / math.sqrt(MLA_NOPE + MLA_ROPE))
    o_mla = o_mla.reshape(B, S, MLA_WIDTH) * jax.nn.silu(mgate)

    return jnp.concatenate([o_gla, o_conv, o_mla], axis=-1) @ w_out


def memory_cross_attention(hn, mem, norm_mem, wq, wk, wv, wo):
    B, S, _ = hn.shape
    M = mem.shape[1]
    memn = rms_norm(mem, norm_mem)
    q = (hn @ wq).reshape(B, S, MEM_HEADS, MEM_HEAD_DIM)
    k = (memn @ wk).reshape(B, M, MEM_HEADS, MEM_HEAD_DIM)
    v = (memn @ wv).reshape(B, M, MEM_HEADS, MEM_HEAD_DIM)
    s = jnp.einsum('bshd,bmhd->bhsm', q, k).astype(jnp.float32) / math.sqrt(MEM_HEAD_DIM)
    p = jax.nn.softmax(s, axis=-1).astype(v.dtype)
    o = jnp.einsum('bhsm,bmhd->bshd', p, v).reshape(B, S, MEM_INNER)
    return o @ wo


def setup_inputs(seed: int = 0) -> dict:
    key = jax.random.key(seed)
    ks = jax.random.split(key, 24)
    f32 = jnp.float32

    def dense(k, fan_in, fan_out):
        return jax.random.normal(k, (DEPTH, fan_in, fan_out), f32) * (fan_in ** -0.5)

    def gain(k, n):
        return 1.0 + 0.02 * jax.random.normal(k, (DEPTH, n), f32)

    x = jax.random.normal(ks[0], (BATCH, SEQ, D_MODEL), f32)
    mem = jax.random.normal(ks[1], (BATCH, MEM_LEN, D_MODEL), f32)
    offset = jax.random.randint(ks[2], (BATCH, 1), 0, 1024, dtype=jnp.int32)
    positions = jnp.arange(SEQ, dtype=jnp.int32)[None, :] + offset
    return {
        'x': x,
        'mem': mem,
        'positions': positions,
        'norm_mix': gain(ks[3], D_MODEL),
        'w_in': dense(ks[4], D_MODEL, IN_WIDTH),
        'gla_w_gate': dense(ks[5], GLA_GATE_RANK, GLA_QK),
        'gla_b_gate': 0.02 * jax.random.normal(ks[6], (DEPTH, GLA_QK), f32),
        'gla_norm': gain(ks[7], GLA_DV),
        'conv_w': jax.random.normal(ks[8], (DEPTH, CONV_K, CONV_WIDTH), f32) * (CONV_K ** -0.5),
        'mla_q_norm': gain(ks[9], MLA_Q_RANK),
        'mla_w_uq': dense(ks[10], MLA_Q_RANK, MLA_HEADS * (MLA_NOPE + MLA_ROPE)),
        'mla_kv_norm': gain(ks[11], MLA_KV_RANK),
        'mla_w_ukv': dense(ks[12], MLA_KV_RANK, MLA_HEADS * (MLA_NOPE + MLA_V)),
        'w_out': dense(ks[13], D_MIX, D_MODEL),
        'norm_xattn': gain(ks[14], D_MODEL),
        'norm_mem': gain(ks[15], D_MODEL),
        'mem_wq': dense(ks[16], D_MODEL, MEM_INNER),
        'mem_wk': dense(ks[17], D_MODEL, MEM_INNER),
        'mem_wv': dense(ks[18], D_MODEL, MEM_INNER),
        'mem_wo': dense(ks[19], MEM_INNER, D_MODEL),
        'norm_final': 1.0 + 0.02 * jax.random.normal(ks[20], (D_MODEL,), f32),
    }


def reference(x, mem, positions, norm_mix, w_in, gla_w_gate, gla_b_gate, gla_norm, conv_w,
              mla_q_norm, mla_w_uq, mla_kv_norm, mla_w_ukv, w_out, norm_xattn, norm_mem,
              mem_wq, mem_wk, mem_wv, mem_wo, norm_final):
    cos, sin = rope_tables(positions)
    h = x
    for l in range(DEPTH):
        h = h + hybrid_mixer(rms_norm(h, norm_mix[l]), cos, sin, w_in[l], gla_w_gate[l],
                             gla_b_gate[l], gla_norm[l], conv_w[l], mla_q_norm[l], mla_w_uq[l],
                             mla_kv_norm[l], mla_w_ukv[l], w_out[l])
        h = h + memory_cross_attention(rms_norm(h, norm_xattn[l]), mem, norm_mem[l],
                                       mem_wq[l], mem_wk[l], mem_wv[l], mem_wo[l])
    return rms_norm(h, norm_final)
```

```python
import functools
import math

import jax
import jax.numpy as jnp
from jax import lax
from jax.experimental import pallas as pl
from jax.experimental.pallas import tpu as pltpu

F32 = jnp.float32
BF16 = jnp.bfloat16

D_MODEL = 1024
EPS = 1e-6

GLA_HEADS = 4
GLA_DV = 96
GLA_DK = 48
GLA_QK = GLA_HEADS * GLA_DK
GLA_WIDTH = GLA_HEADS * GLA_DV
GLA_GATE_RANK = 16
GLA_TAU = 16.0
GLA_CHUNK = 64
GLA_QK_PAD = 256

CONV_WIDTH = 256
CONV_K = 3

MLA_HEADS = 6
MLA_NOPE = 64
MLA_ROPE = 32
MLA_V = 64
MLA_Q_RANK = 256
MLA_KV_RANK = 256
MLA_WIDTH = MLA_HEADS * MLA_V
MLA_SLOT = 128
ROPE_BASE = 10000.0
ROPE_HALF = MLA_ROPE // 2

D_MIX = GLA_WIDTH + CONV_WIDTH + MLA_WIDTH

MEM_HEADS = 4
MEM_HEAD_DIM = 128
MEM_INNER = MEM_HEADS * MEM_HEAD_DIM

LOG2E = math.log2(math.e)
MASK_VALUE = -1e30

SEG_Q = 0
SEG_K = SEG_Q + GLA_QK_PAD
SEG_V = SEG_K + GLA_QK_PAD
SEG_GG = SEG_V + GLA_WIDTH
SEG_CC = SEG_GG + GLA_WIDTH
SEG_CB = SEG_CC + CONV_WIDTH
SEG_CH = SEG_CB + CONV_WIDTH
SEG_CG = SEG_CH + CONV_WIDTH
SEG_CQ = SEG_CG + CONV_WIDTH
SEG_CKV = SEG_CQ + MLA_Q_RANK
SEG_MG = SEG_CKV + MLA_KV_RANK
SEG_KR = SEG_MG + MLA_WIDTH
IN_PAD = SEG_KR + MLA_SLOT
KR_ROPE_OFF = MLA_NOPE
KR_GATE_OFF = MLA_NOPE + MLA_ROPE

VMEM_LIMIT_BYTES = 56 * 1024 * 1024


def _silu(x):
    return x * (1.0 / (1.0 + jnp.exp(-x)))


def _rms(x, g):
    return x * lax.rsqrt(jnp.mean(x * x, axis=-1, keepdims=True) + EPS) * g


def _dot(a, b):
    return jnp.dot(a, b, preferred_element_type=F32)


def _dot_nt(a, b):
    return lax.dot_general(a, b, (((1,), (1,)), ((), ())), preferred_element_type=F32)


def _dot_tn(a, b):
    return lax.dot_general(a, b, (((0,), (0,)), ((), ())), preferred_element_type=F32)


def _rope_slot(x, c, s1, s2):
    return x * c + pltpu.roll(x, ROPE_HALF, 1) * s1 + pltpu.roll(x, MLA_SLOT - ROPE_HALF, 1) * s2


def _proj_kernel(h_ref, g_ref, win_ref, wg_ref, bg_ref, convw_ref, qn_ref, wuq_ref, kvn_ref,
                 wukv_ref, rc_ref, rs1_ref, rs2_ref,
                 gq_ref, gk_ref, gla_ref, gv_ref, gg_ref, oconv_ref, mq_ref, mk_ref, mv_ref,
                 mg_ref, ubuf_ref, *, tile, q_scale):
    t = pl.program_id(1)
    x = h_ref[0]
    xn = _rms(x, g_ref[...]).astype(BF16)

    def seg(a, b):
        return _dot(xn, win_ref[:, a:b])

    gq_ref[0] = seg(SEG_Q, SEG_K)
    gk_ref[0] = seg(SEG_K, SEG_V)
    gv_ref[0] = seg(SEG_V, SEG_GG)
    gg_ref[0] = _silu(seg(SEG_GG, SEG_CC))
    kr = seg(SEG_KR, IN_PAD)
    z = _dot(kr.astype(BF16), wg_ref[...]) + bg_ref[...]
    gla_ref[0] = (jnp.minimum(z, 0.0) - jnp.log1p(jnp.exp(-jnp.abs(z)))) * (1.0 / GLA_TAU)

    u = seg(SEG_CC, SEG_CB) * seg(SEG_CH, SEG_CG)

    @pl.when(t == 0)
    def _():
        ubuf_ref[0:8, :] = jnp.zeros((8, CONV_WIDTH), F32)

    ubuf_ref[8:8 + tile, :] = u
    cw = convw_ref[...]
    conv = (cw[0:1, :] * ubuf_ref[6:6 + tile, :] + cw[1:2, :] * ubuf_ref[7:7 + tile, :]
            + cw[2:3, :] * u)
    oconv_ref[0] = (seg(SEG_CB, SEG_CH) * conv * _silu(seg(SEG_CG, SEG_CQ))).astype(BF16)
    ubuf_ref[0:8, :] = u[tile - 8:tile, :]

    rc = rc_ref[0]
    rs1 = rs1_ref[0]
    rs2 = rs2_ref[0]
    cqn = _rms(seg(SEG_CQ, SEG_CKV), qn_ref[...]).astype(BF16)
    qh = _dot(cqn, wuq_ref[...])
    for h in range(MLA_HEADS):
        sl = slice(h * MLA_SLOT, (h + 1) * MLA_SLOT)
        mq_ref[0, :, sl] = (_rope_slot(qh[:, sl], rc, rs1, rs2) * q_scale).astype(BF16)
    ckvn = _rms(seg(SEG_CKV, SEG_MG), kvn_ref[...]).astype(BF16)
    kvh = _dot(ckvn, wukv_ref[...])
    lane = lax.broadcasted_iota(jnp.int32, (1, MLA_SLOT), 1)
    in_rope = (lane >= KR_ROPE_OFF) & (lane < KR_ROPE_OFF + MLA_ROPE)
    krope = jnp.where(in_rope, _rope_slot(kr, rc, rs1, rs2), 0.0)
    for h in range(MLA_HEADS):
        sl = slice(h * MLA_SLOT, (h + 1) * MLA_SLOT)
        mk_ref[0, :, sl] = (kvh[:, sl] + krope).astype(BF16)
    mv_ref[0] = kvh[:, MLA_HEADS * MLA_SLOT:].astype(BF16)
    mg_ref[0] = _silu(seg(SEG_MG, SEG_KR))


def _proj_call(h, l, p, tables, tile):
    B, S, D = h.shape
    nt = S // tile
    q_scale = LOG2E / math.sqrt(MLA_NOPE + MLA_ROPE)

    def tok(w):
        return pl.BlockSpec((1, tile, w), lambda b, t: (b, t, 0))

    def lay(shape):
        return pl.BlockSpec((None,) + shape, lambda b, t: (l,) + (0,) * len(shape))

    in_specs = [
        tok(D),
        lay((1, D)),
        lay((D, IN_PAD)),
        lay((MLA_SLOT, GLA_QK_PAD)),
        lay((1, GLA_QK_PAD)),
        lay((CONV_K, CONV_WIDTH)),
        lay((1, MLA_Q_RANK)),
        lay((MLA_Q_RANK, MLA_HEADS * MLA_SLOT)),
        lay((1, MLA_KV_RANK)),
        lay((MLA_KV_RANK, MLA_HEADS * MLA_SLOT + MLA_WIDTH)),
        tok(MLA_SLOT), tok(MLA_SLOT), tok(MLA_SLOT),
    ]
    outs = [
        (GLA_QK_PAD, F32), (GLA_QK_PAD, F32), (GLA_QK_PAD, F32), (GLA_WIDTH, F32),
        (GLA_WIDTH, F32), (CONV_WIDTH, BF16), (MLA_HEADS * MLA_SLOT, BF16),
        (MLA_HEADS * MLA_SLOT, BF16), (MLA_WIDTH, BF16), (MLA_WIDTH, F32),
    ]
    return pl.pallas_call(
        functools.partial(_proj_kernel, tile=tile, q_scale=q_scale),
        grid=(B, nt),
        in_specs=in_specs,
        out_specs=[tok(w) for w, _ in outs],
        out_shape=[jax.ShapeDtypeStruct((B, S, w), dt) for w, dt in outs],
        scratch_shapes=[pltpu.VMEM((tile + 8, CONV_WIDTH), F32)],
        compiler_params=pltpu.CompilerParams(
            dimension_semantics=("arbitrary", "arbitrary"),
            vmem_limit_bytes=VMEM_LIMIT_BYTES),
        name="proj",
    )(h, p["norm_mix"], p["w_in"], p["wg"], p["bg"], p["conv_w"], p["q_norm"], p["w_uq"],
      p["kv_norm"], p["w_ukv"], *tables)


def _gla_kernel(q_ref, k_ref, la_ref, v_ref, gg_ref, gn_ref, o_ref, st_ref, *, tile):
    t = pl.program_id(1)
    C = GLA_CHUNK

    @pl.when(t == 0)
    def _():
        st_ref[...] = jnp.zeros(st_ref.shape, F32)

    lane_k = lax.broadcasted_iota(jnp.int32, (1, GLA_QK_PAD), 1)
    lane_v = lax.broadcasted_iota(jnp.int32, (1, GLA_WIDTH), 1)
    kmask = [(lane_k >= h * GLA_DK) & (lane_k < (h + 1) * GLA_DK) for h in range(GLA_HEADS)]
    vmask = [(lane_v >= h * GLA_DV) & (lane_v < (h + 1) * GLA_DV) for h in range(GLA_HEADS)]
    row = lax.broadcasted_iota(jnp.int32, (C, C), 0)
    col = lax.broadcasted_iota(jnp.int32, (C, C), 1)
    tril = row >= col
    cum = tril.astype(BF16)
    hrow = lax.broadcasted_iota(jnp.int32, (C, GLA_HEADS * C), 0)
    hcol = lax.broadcasted_iota(jnp.int32, (C, GLA_HEADS * C), 1)
    tril_heads = hrow >= (hcol & (C - 1))
    srow = lax.broadcasted_iota(jnp.int32, (GLA_WIDTH, GLA_QK_PAD), 0)
    slane = lax.broadcasted_iota(jnp.int32, (GLA_WIDTH, GLA_QK_PAD), 1)
    same_head = None
    for h in range(GLA_HEADS):
        blk = ((srow >= h * GLA_DV) & (srow < (h + 1) * GLA_DV)
               & (slane >= h * GLA_DK) & (slane < (h + 1) * GLA_DK))
        same_head = blk if same_head is None else (same_head | blk)
    gn = gn_ref[...]

    def chunk(c, carry):
        sl = pl.ds(pl.multiple_of(c * C, C), C)
        q = q_ref[0, sl, :]
        k = k_ref[0, sl, :]
        la = la_ref[0, sl, :]
        v = v_ref[0, sl, :]
        la_hi = la.astype(BF16)
        la_lo = (la - la_hi.astype(F32)).astype(BF16)
        b = _dot(cum, la_hi) + _dot(cum, la_lo)
        b_last = b[C - 1:C, :]
        q_dec = (q * (GLA_DK ** -0.5) * jnp.exp(b)).astype(BF16)
        k_inv = k * jnp.exp(-b)
        k_end = (k * jnp.exp(b_last - b)).astype(BF16)
        decay = jnp.exp(b_last)
        k_heads = jnp.concatenate(
            [jnp.where(kmask[h], k_inv, 0.0) for h in range(GLA_HEADS)], axis=0).astype(BF16)
        a = _dot_nt(q_dec, k_heads)
        a = jnp.where(tril_heads, a, 0.0).astype(BF16)
        v_heads = jnp.concatenate(
            [jnp.where(vmask[h], v, 0.0) for h in range(GLA_HEADS)], axis=0).astype(BF16)
        st = st_ref[...]
        o = _dot(a, v_heads) + _dot_nt(q_dec, st.astype(BF16))
        upd = _dot_tn(v.astype(BF16), k_end)
        st_ref[...] = st * decay + jnp.where(same_head, upd, 0.0)
        o2 = o * o
        ms = jnp.zeros_like(o)
        for h in range(GLA_HEADS):
            mh = jnp.sum(jnp.where(vmask[h], o2, 0.0), axis=1, keepdims=True) * (1.0 / GLA_DV)
            ms = jnp.where(vmask[h], mh, ms)
        y = o * lax.rsqrt(ms + EPS) * gn
        o_ref[0, sl, :] = (y * gg_ref[0, sl, :]).astype(BF16)
        return carry

    lax.fori_loop(0, tile // C, chunk, 0)


def _gla_call(gq, gk, gla, gv, gg, l, p, tile):
    B, S, _ = gq.shape
    nt = S // tile

    def tok(w):
        return pl.BlockSpec((1, tile, w), lambda b, t: (b, t, 0))

    return pl.pallas_call(
        functools.partial(_gla_kernel, tile=tile),
        grid=(B, nt),
        in_specs=[tok(GLA_QK_PAD), tok(GLA_QK_PAD), tok(GLA_QK_PAD), tok(GLA_WIDTH),
                  tok(GLA_WIDTH),
                  pl.BlockSpec((None, 1, GLA_WIDTH), lambda b, t: (l, 0, 0))],
        out_specs=tok(GLA_WIDTH),
        out_shape=jax.ShapeDtypeStruct((B, S, GLA_WIDTH), BF16),
        scratch_shapes=[pltpu.VMEM((GLA_WIDTH, GLA_QK_PAD), F32)],
        compiler_params=pltpu.CompilerParams(
            dimension_semantics=("arbitrary", "arbitrary"),
            vmem_limit_bytes=VMEM_LIMIT_BYTES),
        name="gla",
    )(gq, gk, gla, gv, gg, p["gla_norm"])


def _mla_kernel(q_ref, k_ref, v_ref, mg_ref, o_ref, m_ref, l_ref, acc_ref, *, tq, tk):
    qi = pl.program_id(2)
    nrep = tk // MLA_SLOT
    row = lax.broadcasted_iota(jnp.int32, (tq, tk), 0)
    col = lax.broadcasted_iota(jnp.int32, (tq, tk), 1)
    causal = row >= col

    for e in range(2):
        q = q_ref[0, :, e * MLA_SLOT:(e + 1) * MLA_SLOT]
        m_ref[...] = jnp.full(m_ref.shape, MASK_VALUE, F32)
        l_ref[...] = jnp.zeros(l_ref.shape, F32)
        acc_ref[e] = jnp.zeros(acc_ref.shape[1:], F32)

        def step(j, masked, q=q, e=e):
            rows = pl.ds(pl.multiple_of(j * tk, tk), tk)
            kj = k_ref[0, rows, e * MLA_SLOT:(e + 1) * MLA_SLOT]
            vj = v_ref[0, rows, :]
            s = _dot_nt(q, kj)
            if masked:
                s = jnp.where(causal, s, MASK_VALUE)
            m_prev = m_ref[...]
            m_new = jnp.maximum(m_prev, jnp.max(s, axis=1, keepdims=True))
            alpha = jnp.exp2(m_prev - m_new)
            p = jnp.exp2(s - pltpu.repeat(m_new, nrep, 1))
            l_ref[...] = alpha * l_ref[...] + jnp.sum(p, axis=1, keepdims=True)
            acc_ref[e] = alpha * acc_ref[e] + _dot(p.astype(BF16), vj)
            m_ref[...] = m_new

        def body(j, carry):
            step(j, False)
            return carry

        lax.fori_loop(0, qi * (tq // tk), body, 0)
        for d in range(tq // tk):
            step(qi * (tq // tk) + d, True)
        acc_ref[e] = acc_ref[e] / l_ref[...]

    lane = lax.broadcasted_iota(jnp.int32, (1, MLA_SLOT), 1)
    o = jnp.where(lane < MLA_V, acc_ref[0], acc_ref[1])
    o_ref[0] = (o * mg_ref[0]).astype(BF16)


def _mla_call(mq, mk, mv, mg, tq, tk):
    B, S, _ = mq.shape
    assert tq == tk
    nq = S // tq
    pairs = MLA_HEADS // 2
    return pl.pallas_call(
        functools.partial(_mla_kernel, tq=tq, tk=tk),
        grid=(B, pairs, nq),
        in_specs=[
            pl.BlockSpec((1, tq, 2 * MLA_SLOT), lambda b, hp, i: (b, i, hp)),
            pl.BlockSpec((1, S, 2 * MLA_SLOT), lambda b, hp, i: (b, 0, hp)),
            pl.BlockSpec((1, S, 2 * MLA_V), lambda b, hp, i: (b, 0, hp)),
            pl.BlockSpec((1, tq, 2 * MLA_V), lambda b, hp, i: (b, i, hp)),
        ],
        out_specs=pl.BlockSpec((1, tq, 2 * MLA_V), lambda b, hp, i: (b, i, hp)),
        out_shape=jax.ShapeDtypeStruct((B, S, MLA_WIDTH), BF16),
        scratch_shapes=[pltpu.VMEM((tq, MLA_SLOT), F32), pltpu.VMEM((tq, MLA_SLOT), F32),
                        pltpu.VMEM((2, tq, MLA_SLOT), F32)],
        compiler_params=pltpu.CompilerParams(
            dimension_semantics=("arbitrary", "arbitrary", "arbitrary"),
            vmem_limit_bytes=VMEM_LIMIT_BYTES),
        name="mla",
    )(mq, mk, mv, mg)


def _memkv_kernel(mem_ref, g_ref, wk_ref, wv_ref, k_ref, v_ref):
    memn = _rms(mem_ref[0], g_ref[...]).astype(BF16)
    k_ref[...] = _dot(memn, wk_ref[...]).astype(BF16)
    v_ref[...] = _dot(memn, wv_ref[...]).astype(BF16)


def _memkv_call(mem, norm_mem, wk, wv):
    B, M, D = mem.shape
    L = wk.shape[0]
    kv_shape = jax.ShapeDtypeStruct((L, B, M, MEM_INNER), BF16)
    kv_spec = pl.BlockSpec((None, None, M, MEM_INNER), lambda l, b: (l, b, 0, 0))
    return pl.pallas_call(
        _memkv_kernel,
        grid=(L, B),
        in_specs=[pl.BlockSpec((1, M, D), lambda l, b: (b, 0, 0)),
                  pl.BlockSpec((None, 1, D), lambda l, b: (l, 0, 0)),
                  pl.BlockSpec((None, D, MEM_INNER), lambda l, b: (l, 0, 0)),
                  pl.BlockSpec((None, D, MEM_INNER), lambda l, b: (l, 0, 0))],
        out_specs=[kv_spec, kv_spec],
        out_shape=[kv_shape, kv_shape],
        compiler_params=pltpu.CompilerParams(
            dimension_semantics=("arbitrary", "arbitrary"),
            vmem_limit_bytes=VMEM_LIMIT_BYTES),
        name="memkv",
    )(mem, norm_mem, wk, wv)


def _post_kernel(h_ref, og_ref, oc_ref, om_ref, wout_ref, gx_ref, wq_ref, mk_ref, mv_ref,
                 wo_ref, gf_ref, o_ref, *, final_norm):
    w = wout_ref
    mix = (_dot(og_ref[0], w[0:GLA_WIDTH, :])
           + _dot(oc_ref[0], w[GLA_WIDTH:GLA_WIDTH + CONV_WIDTH, :])
           + _dot(om_ref[0], w[GLA_WIDTH + CONV_WIDTH:D_MIX, :]))
    h1 = h_ref[0] + mix
    hn = _rms(h1, gx_ref[...]).astype(BF16)
    q = (_dot(hn, wq_ref[...]) * (LOG2E / math.sqrt(MEM_HEAD_DIM))).astype(BF16)
    heads = []
    for hd in range(MEM_HEADS):
        sl = slice(hd * MEM_HEAD_DIM, (hd + 1) * MEM_HEAD_DIM)
        s = _dot_nt(q[:, sl], mk_ref[:, sl])
        p = jnp.exp2(s - jnp.max(s, axis=1, keepdims=True))
        p = p / jnp.sum(p, axis=1, keepdims=True)
        heads.append(_dot(p.astype(BF16), mv_ref[:, sl]).astype(BF16))
    o = jnp.concatenate(heads, axis=1)
    h2 = h1 + _dot(o, wo_ref[...])
    if final_norm:
        h2 = _rms(h2, gf_ref[...])
    o_ref[0] = h2


def _post_call(h, og, oc, om, memk, memv, l, p, norm_final, tile, final_norm):
    B, S, D = h.shape
    nt = S // tile
    M = memk.shape[2]

    def tok(w):
        return pl.BlockSpec((1, tile, w), lambda b, t: (b, t, 0))

    def lay(shape):
        return pl.BlockSpec((None,) + shape, lambda b, t: (l,) + (0,) * len(shape))

    mem_spec = pl.BlockSpec((None, None, M, MEM_INNER), lambda b, t: (l, b, 0, 0))
    return pl.pallas_call(
        functools.partial(_post_kernel, final_norm=final_norm),
        grid=(B, nt),
        in_specs=[tok(D), tok(GLA_WIDTH), tok(CONV_WIDTH), tok(MLA_WIDTH),
                  lay((D_MIX, D)), lay((1, D)), lay((D, MEM_INNER)), mem_spec, mem_spec,
                  lay((MEM_INNER, D)), pl.BlockSpec((1, D), lambda b, t: (0, 0))],
        out_specs=tok(D),
        out_shape=jax.ShapeDtypeStruct((B, S, D), F32),
        compiler_params=pltpu.CompilerParams(
            dimension_semantics=("arbitrary", "arbitrary"),
            vmem_limit_bytes=VMEM_LIMIT_BYTES),
        name="post",
    )(h, og, oc, om, p["w_out"], p["norm_xattn"], p["mem_wq"], memk, memv, p["mem_wo"],
      norm_final)


def _pad_cols(w, width):
    return jnp.pad(w, [(0, 0)] * (w.ndim - 1) + [(0, width - w.shape[-1])])


def _pack_w_in(w_in):
    L, D, _ = w_in.shape
    pts, acc = [], 0
    for s in (GLA_QK, GLA_QK, GLA_WIDTH, GLA_GATE_RANK, GLA_WIDTH, CONV_WIDTH, CONV_WIDTH,
              CONV_WIDTH, CONV_WIDTH, MLA_Q_RANK, MLA_KV_RANK, MLA_ROPE):
        acc += s
        pts.append(acc)
    gq, gk, gv, glr, gg, cc, cb, ch, cg, cq, ckv, kr, mg = jnp.split(w_in, pts, axis=-1)
    zeros = lambda n: jnp.zeros((L, D, n), w_in.dtype)
    kr_seg = jnp.concatenate(
        [zeros(KR_ROPE_OFF), kr, glr, zeros(MLA_SLOT - KR_GATE_OFF - GLA_GATE_RANK)], axis=-1)
    packed = jnp.concatenate(
        [_pad_cols(gq, GLA_QK_PAD), _pad_cols(gk, GLA_QK_PAD), gv, gg, cc, cb, ch, cg, cq, ckv,
         mg, kr_seg], axis=-1)
    return packed.astype(BF16)


def _prep_params(norm_mix, w_in, gla_w_gate, gla_b_gate, gla_norm, conv_w, mla_q_norm, mla_w_uq,
                 mla_kv_norm, mla_w_ukv, w_out, norm_xattn, mem_wq, mem_wo):
    L = w_in.shape[0]
    wg = jnp.zeros((L, MLA_SLOT, GLA_QK_PAD), F32)
    wg = wg.at[:, KR_GATE_OFF:KR_GATE_OFF + GLA_GATE_RANK, :GLA_QK].set(gla_w_gate)
    w_uq = mla_w_uq.reshape(L, MLA_Q_RANK, MLA_HEADS, MLA_NOPE + MLA_ROPE)
    w_uq = _pad_cols(w_uq, MLA_SLOT).reshape(L, MLA_Q_RANK, MLA_HEADS * MLA_SLOT)
    w_ukv = mla_w_ukv.reshape(L, MLA_KV_RANK, MLA_HEADS, MLA_NOPE + MLA_V)
    w_uk = _pad_cols(w_ukv[..., :MLA_NOPE], MLA_SLOT).reshape(L, MLA_KV_RANK, MLA_HEADS * MLA_SLOT)
    w_uv = w_ukv[..., MLA_NOPE:].reshape(L, MLA_KV_RANK, MLA_WIDTH)
    return {
        "norm_mix": norm_mix[:, None, :],
        "w_in": _pack_w_in(w_in),
        "wg": wg.astype(BF16),
        "bg": _pad_cols(gla_b_gate, GLA_QK_PAD)[:, None, :],
        "gla_norm": jnp.tile(gla_norm, (1, GLA_HEADS))[:, None, :],
        "conv_w": conv_w,
        "q_norm": mla_q_norm[:, None, :],
        "w_uq": w_uq.astype(BF16),
        "kv_norm": mla_kv_norm[:, None, :],
        "w_ukv": jnp.concatenate([w_uk, w_uv], axis=-1).astype(BF16),
        "w_out": w_out.astype(BF16),
        "norm_xattn": norm_xattn[:, None, :],
        "mem_wq": mem_wq.astype(BF16),
        "mem_wo": mem_wo.astype(BF16),
    }


def _rope_tables(positions):
    inv_freq = 1.0 / (ROPE_BASE ** (jnp.arange(0, MLA_ROPE, 2, dtype=F32) / MLA_ROPE))
    ang = positions.astype(F32)[..., None] * inv_freq
    cos, sin = jnp.cos(ang), jnp.sin(ang)
    B, S, _ = cos.shape
    one = jnp.ones((B, S, MLA_NOPE), F32)
    z = lambda n: jnp.zeros((B, S, n), F32)
    tail = MLA_SLOT - MLA_NOPE - MLA_ROPE
    rc = jnp.concatenate([one, cos, cos, z(tail)], axis=-1)
    rs1 = jnp.concatenate([z(MLA_NOPE + ROPE_HALF), sin, z(tail)], axis=-1)
    rs2 = jnp.concatenate([z(MLA_NOPE), -sin, z(ROPE_HALF + tail)], axis=-1)
    return rc, rs1, rs2


def _pick_tile(S, want):
    t = min(S, want)
    assert S % t == 0 and t % GLA_CHUNK == 0
    return t


def kernel(x, mem, positions, norm_mix, w_in, gla_w_gate, gla_b_gate, gla_norm, conv_w, mla_q_norm,
           mla_w_uq, mla_kv_norm, mla_w_ukv, w_out, norm_xattn, norm_mem, mem_wq, mem_wk, mem_wv,
           mem_wo, norm_final):
    depth = w_in.shape[0]
    S = x.shape[1]
    tile = _pick_tile(S, 512)
    p = _prep_params(norm_mix, w_in, gla_w_gate, gla_b_gate, gla_norm, conv_w, mla_q_norm,
                     mla_w_uq, mla_kv_norm, mla_w_ukv, w_out, norm_xattn, mem_wq, mem_wo)
    tables = _rope_tables(positions)
    memk, memv = _memkv_call(mem, norm_mem[:, None, :], mem_wk.astype(BF16), mem_wv.astype(BF16))
    nf = norm_final[None, :]
    h = x
    for l in range(depth):
        gq, gk, gla, gv, gg, oc, mq, mk, mv, mg = _proj_call(h, l, p, tables, tile)
        og = _gla_call(gq, gk, gla, gv, gg, l, p, tile)
        om = _mla_call(mq, mk, mv, mg, tile, tile)
        h = _post_call(h, og, oc, om, memk, memv, l, p, nf, tile, l == depth - 1)
    return h
```

```python
import functools
import math

import jax
import jax.numpy as jnp
from jax import lax
from jax.experimental import pallas as pl
from jax.experimental.pallas import tpu as pltpu

F32 = jnp.float32
BF16 = jnp.bfloat16

D_MODEL = 1024
EPS = 1e-6

GLA_HEADS = 4
GLA_DV = 96
GLA_DK = 48
GLA_QK = GLA_HEADS * GLA_DK
GLA_WIDTH = GLA_HEADS * GLA_DV
GLA_GATE_RANK = 16
GLA_TAU = 16.0
GLA_CHUNK = 64
GLA_QK_PAD = 256

CONV_WIDTH = 256
CONV_K = 3

MLA_HEADS = 6
MLA_NOPE = 64
MLA_ROPE = 32
MLA_V = 64
MLA_Q_RANK = 256
MLA_KV_RANK = 256
MLA_WIDTH = MLA_HEADS * MLA_V
MLA_SLOT = 128
MLA_ACC_ROWS = MLA_V + 16
ROPE_BASE = 10000.0
ROPE_HALF = MLA_ROPE // 2

D_MIX = GLA_WIDTH + CONV_WIDTH + MLA_WIDTH

MEM_HEADS = 4
MEM_HEAD_DIM = 128
MEM_INNER = MEM_HEADS * MEM_HEAD_DIM

LOG2E = math.log2(math.e)
MASK_VALUE = -1e30

SEG_Q = 0
SEG_K = SEG_Q + GLA_QK_PAD
SEG_V = SEG_K + GLA_QK_PAD
SEG_GG = SEG_V + GLA_WIDTH
SEG_CC = SEG_GG + GLA_WIDTH
SEG_CB = SEG_CC + CONV_WIDTH
SEG_CH = SEG_CB + CONV_WIDTH
SEG_CG = SEG_CH + CONV_WIDTH
SEG_CQ = SEG_CG + CONV_WIDTH
SEG_CKV = SEG_CQ + MLA_Q_RANK
SEG_MG = SEG_CKV + MLA_KV_RANK
SEG_KR = SEG_MG + MLA_WIDTH
IN_PAD = SEG_KR + MLA_SLOT
KR_ROPE_OFF = MLA_NOPE
KR_GATE_OFF = MLA_NOPE + MLA_ROPE

VMEM_LIMIT_BYTES = 56 * 1024 * 1024


def _silu(x):
    return x * (1.0 / (1.0 + jnp.exp(-x)))


def _rms(x, g):
    return x * lax.rsqrt(jnp.mean(x * x, axis=-1, keepdims=True) + EPS) * g


def _dot(a, b):
    return jnp.dot(a, b, preferred_element_type=F32)


def _dot_nt(a, b):
    return lax.dot_general(a, b, (((1,), (1,)), ((), ())), preferred_element_type=F32)


def _dot_tn(a, b):
    return lax.dot_general(a, b, (((0,), (0,)), ((), ())), preferred_element_type=F32)


def _rope_slot(x, c, s1, s2):
    return x * c + pltpu.roll(x, ROPE_HALF, 1) * s1 + pltpu.roll(x, MLA_SLOT - ROPE_HALF, 1) * s2


def _proj_kernel(h_ref, g_ref, win_ref, wg_ref, bg_ref, convw_ref, qn_ref, wuq_ref, kvn_ref,
                 wukv_ref, rc_ref, rs1_ref, rs2_ref,
                 gq_ref, gk_ref, gla_ref, gv_ref, gg_ref, oconv_ref, mq_ref, mk_ref, mv_ref,
                 mg_ref, ubuf_ref, *, tile, q_scale):
    t = pl.program_id(1)
    x = h_ref[0]
    xn = _rms(x, g_ref[...]).astype(BF16)

    def seg(a, b):
        return _dot(xn, win_ref[:, a:b])

    gq_ref[0] = seg(SEG_Q, SEG_K)
    gk_ref[0] = seg(SEG_K, SEG_V)
    gv_ref[0] = seg(SEG_V, SEG_GG)
    gg_ref[0] = _silu(seg(SEG_GG, SEG_CC))
    kr = seg(SEG_KR, IN_PAD)
    z = _dot(kr.astype(BF16), wg_ref[...]) + bg_ref[...]
    gla_ref[0] = (jnp.minimum(z, 0.0) - jnp.log1p(jnp.exp(-jnp.abs(z)))) * (1.0 / GLA_TAU)

    u = seg(SEG_CC, SEG_CB) * seg(SEG_CH, SEG_CG)

    @pl.when(t == 0)
    def _():
        ubuf_ref[0:8, :] = jnp.zeros((8, CONV_WIDTH), F32)

    ubuf_ref[8:8 + tile, :] = u
    cw = convw_ref[...]
    conv = (cw[0:1, :] * ubuf_ref[6:6 + tile, :] + cw[1:2, :] * ubuf_ref[7:7 + tile, :]
            + cw[2:3, :] * u)
    oconv_ref[0] = (seg(SEG_CB, SEG_CH) * conv * _silu(seg(SEG_CG, SEG_CQ))).astype(BF16)
    ubuf_ref[0:8, :] = u[tile - 8:tile, :]

    rc = rc_ref[0]
    rs1 = rs1_ref[0]
    rs2 = rs2_ref[0]
    cqn = _rms(seg(SEG_CQ, SEG_CKV), qn_ref[...]).astype(BF16)
    qh = _dot(cqn, wuq_ref[...])
    for h in range(MLA_HEADS):
        sl = slice(h * MLA_SLOT, (h + 1) * MLA_SLOT)
        qf = _rope_slot(qh[:, sl], rc, rs1, rs2) * q_scale
        mq_ref[0, 0, sl, :] = qf.T.astype(BF16)
    ckvn = _rms(seg(SEG_CKV, SEG_MG), kvn_ref[...]).astype(BF16)
    kvh = _dot(ckvn, wukv_ref[...])
    lane = lax.broadcasted_iota(jnp.int32, (1, MLA_SLOT), 1)
    in_rope = (lane >= KR_ROPE_OFF) & (lane < KR_ROPE_OFF + MLA_ROPE)
    krope = jnp.where(in_rope, _rope_slot(kr, rc, rs1, rs2), 0.0)
    for h in range(MLA_HEADS):
        sl = slice(h * MLA_SLOT, (h + 1) * MLA_SLOT)
        mk_ref[0, :, sl] = (kvh[:, sl] + krope).astype(BF16)
    mv_ref[0, 0] = kvh[:, MLA_HEADS * MLA_SLOT:].T.astype(BF16)
    mg_ref[0] = _silu(seg(SEG_MG, SEG_KR))


def _proj_call(h, l, p, tables, tile):
    B, S, D = h.shape
    nt = S // tile
    q_scale = LOG2E / math.sqrt(MLA_NOPE + MLA_ROPE)

    def tok(w):
        return pl.BlockSpec((1, tile, w), lambda b, t: (b, t, 0))

    def lay(shape):
        return pl.BlockSpec((None,) + shape, lambda b, t: (l,) + (0,) * len(shape))

    in_specs = [
        tok(D),
        lay((1, D)),
        lay((D, IN_PAD)),
        lay((MLA_SLOT, GLA_QK_PAD)),
        lay((1, GLA_QK_PAD)),
        lay((CONV_K, CONV_WIDTH)),
        lay((1, MLA_Q_RANK)),
        lay((MLA_Q_RANK, MLA_HEADS * MLA_SLOT)),
        lay((1, MLA_KV_RANK)),
        lay((MLA_KV_RANK, MLA_HEADS * MLA_SLOT + MLA_WIDTH)),
        tok(MLA_SLOT), tok(MLA_SLOT), tok(MLA_SLOT),
    ]
    def tok_out(w, dt):
        return tok(w), jax.ShapeDtypeStruct((B, S, w), dt)

    def tile_t_out(w, dt):
        return (pl.BlockSpec((1, 1, w, tile), lambda b, t: (b, t, 0, 0)),
                jax.ShapeDtypeStruct((B, nt, w, tile), dt))

    outs = [
        tok_out(GLA_QK_PAD, F32), tok_out(GLA_QK_PAD, F32), tok_out(GLA_QK_PAD, F32),
        tok_out(GLA_WIDTH, F32), tok_out(GLA_WIDTH, F32), tok_out(CONV_WIDTH, BF16),
        tile_t_out(MLA_HEADS * MLA_SLOT, BF16), tok_out(MLA_HEADS * MLA_SLOT, BF16),
        tile_t_out(MLA_WIDTH, BF16), tok_out(MLA_WIDTH, F32),
    ]
    return pl.pallas_call(
        functools.partial(_proj_kernel, tile=tile, q_scale=q_scale),
        grid=(B, nt),
        in_specs=in_specs,
        out_specs=[spec for spec, _ in outs],
        out_shape=[shape for _, shape in outs],
        scratch_shapes=[pltpu.VMEM((tile + 8, CONV_WIDTH), F32)],
        compiler_params=pltpu.CompilerParams(
            dimension_semantics=("arbitrary", "arbitrary"),
            vmem_limit_bytes=VMEM_LIMIT_BYTES),
        name="proj",
    )(h, p["norm_mix"], p["w_in"], p["wg"], p["bg"], p["conv_w"], p["q_norm"], p["w_uq"],
      p["kv_norm"], p["w_ukv"], *tables)


def _gla_kernel(q_ref, k_ref, la_ref, v_ref, gg_ref, gn_ref, o_ref, st_ref, *, tile):
    t = pl.program_id(1)
    C = GLA_CHUNK

    @pl.when(t == 0)
    def _():
        st_ref[...] = jnp.zeros(st_ref.shape, F32)

    lane_k = lax.broadcasted_iota(jnp.int32, (1, GLA_QK_PAD), 1)
    lane_v = lax.broadcasted_iota(jnp.int32, (1, GLA_WIDTH), 1)
    kmask = [(lane_k >= h * GLA_DK) & (lane_k < (h + 1) * GLA_DK) for h in range(GLA_HEADS)]
    vmask = [(lane_v >= h * GLA_DV) & (lane_v < (h + 1) * GLA_DV) for h in range(GLA_HEADS)]
    row = lax.broadcasted_iota(jnp.int32, (C, 2 * C), 0)
    col = lax.broadcasted_iota(jnp.int32, (C, 2 * C), 1)
    cum2 = (row >= (col & (C - 1))).astype(BF16)
    hrow = lax.broadcasted_iota(jnp.int32, (C, GLA_HEADS * C), 0)
    hcol = lax.broadcasted_iota(jnp.int32, (C, GLA_HEADS * C), 1)
    tril_heads = hrow >= (hcol & (C - 1))
    srow = lax.broadcasted_iota(jnp.int32, (GLA_WIDTH, GLA_QK_PAD), 0)
    slane = lax.broadcasted_iota(jnp.int32, (GLA_WIDTH, GLA_QK_PAD), 1)
    same_head = None
    for h in range(GLA_HEADS):
        blk = ((srow >= h * GLA_DV) & (srow < (h + 1) * GLA_DV)
               & (slane >= h * GLA_DK) & (slane < (h + 1) * GLA_DK))
        same_head = blk if same_head is None else (same_head | blk)
    gn = gn_ref[...]

    st = st_ref[...]
    for c in range(tile // C):
        sl = slice(c * C, (c + 1) * C)
        q = q_ref[0, sl, :]
        k = k_ref[0, sl, :]
        la = la_ref[0, sl, :]
        v = v_ref[0, sl, :]
        la_hi = la.astype(BF16)
        la_lo = (la - la_hi.astype(F32)).astype(BF16)
        b = _dot(cum2, jnp.concatenate([la_hi, la_lo], axis=0))
        b_last = b[C - 1:C, :]
        q_dec = (q * (GLA_DK ** -0.5) * jnp.exp(b)).astype(BF16)
        k_inv = k * jnp.exp(-b)
        k_end = (k * jnp.exp(b_last - b)).astype(BF16)
        decay = jnp.exp(b_last)
        k_heads = jnp.concatenate(
            [jnp.where(kmask[h], k_inv, 0.0) for h in range(GLA_HEADS)], axis=0).astype(BF16)
        a = _dot_nt(q_dec, k_heads)
        a = jnp.where(tril_heads, a, 0.0).astype(BF16)
        v_heads = jnp.concatenate(
            [jnp.where(vmask[h], v, 0.0) for h in range(GLA_HEADS)], axis=0).astype(BF16)
        o = _dot(a, v_heads) + _dot_nt(q_dec, st.astype(BF16))
        upd = _dot_tn(v.astype(BF16), k_end)
        st = st * decay + jnp.where(same_head, upd, 0.0)
        o2 = o * o
        ms = jnp.zeros_like(o)
        for h in range(GLA_HEADS):
            mh = jnp.sum(jnp.where(vmask[h], o2, 0.0), axis=1, keepdims=True) * (1.0 / GLA_DV)
            ms = jnp.where(vmask[h], mh, ms)
        y = o * lax.rsqrt(ms + EPS) * gn
        o_ref[0, sl, :] = (y * gg_ref[0, sl, :]).astype(BF16)
    st_ref[...] = st


def _gla_call(gq, gk, gla, gv, gg, l, p, tile):
    B, S, _ = gq.shape
    nt = S // tile

    def tok(w):
        return pl.BlockSpec((1, tile, w), lambda b, t: (b, t, 0))

    return pl.pallas_call(
        functools.partial(_gla_kernel, tile=tile),
        grid=(B, nt),
        in_specs=[tok(GLA_QK_PAD), tok(GLA_QK_PAD), tok(GLA_QK_PAD), tok(GLA_WIDTH),
                  tok(GLA_WIDTH),
                  pl.BlockSpec((None, 1, GLA_WIDTH), lambda b, t: (l, 0, 0))],
        out_specs=tok(GLA_WIDTH),
        out_shape=jax.ShapeDtypeStruct((B, S, GLA_WIDTH), BF16),
        scratch_shapes=[pltpu.VMEM((GLA_WIDTH, GLA_QK_PAD), F32)],
        compiler_params=pltpu.CompilerParams(
            dimension_semantics=("arbitrary", "arbitrary"),
            vmem_limit_bytes=VMEM_LIMIT_BYTES),
        name="gla",
    )(gq, gk, gla, gv, gg, p["gla_norm"])


def _mla_kernel(qt_ref, k_ref, vt_ref, mg_ref, o_ref, m_ref, acc_ref, sa_ref, sb_ref, *, tq, tk):
    qi = pl.program_id(2)
    krow = lax.broadcasted_iota(jnp.int32, (tk, tq), 0)
    qcol = lax.broadcasted_iota(jnp.int32, (tk, tq), 1)
    causal = krow <= qcol
    ones_rows = jnp.ones((MLA_ACC_ROWS - MLA_V, tk), BF16)

    m_ref[...] = jnp.full(m_ref.shape, MASK_VALUE, F32)
    acc_ref[...] = jnp.zeros(acc_ref.shape, F32)

    def scores(j, s_ref):
        rows = pl.ds(pl.multiple_of(j * tk, tk), tk)
        for e in range(2):
            kj = k_ref[0, rows, e * MLA_SLOT:(e + 1) * MLA_SLOT]
            qt = qt_ref[0, 0, e * MLA_SLOT:(e + 1) * MLA_SLOT, :]
            s_ref[e] = _dot(kj, qt)

    def consume(j, s_ref, masked):
        for e in range(2):
            s = s_ref[e]
            if masked:
                s = jnp.where(causal, s, MASK_VALUE)
            m_prev = m_ref[e]
            m_new = jnp.maximum(m_prev, jnp.max(s, axis=0, keepdims=True))
            alpha = jnp.exp2(m_prev - m_new)
            p = jnp.exp2(s - m_new).astype(BF16)
            vt = jnp.concatenate([vt_ref[0, j, e * MLA_V:(e + 1) * MLA_V, :], ones_rows], axis=0)
            acc_ref[e] = alpha * acc_ref[e] + _dot(vt, p)
            m_ref[e] = m_new

    scores(0, sa_ref)

    def body(i, carry):
        j = 2 * i
        scores(j + 1, sb_ref)
        consume(j, sa_ref, False)
        scores(j + 2, sa_ref)
        consume(j + 1, sb_ref, False)
        return carry

    lax.fori_loop(0, qi // 2, body, 0)

    @pl.when(qi % 2 == 0)
    def _():
        consume(qi, sa_ref, True)

    @pl.when(qi % 2 == 1)
    def _():
        scores(qi, sb_ref)
        consume(qi - 1, sa_ref, False)
        consume(qi, sb_ref, True)

    ot = jnp.concatenate(
        [acc_ref[e, 0:MLA_V, :] / acc_ref[e, MLA_V:MLA_V + 1, :] for e in range(2)], axis=0)
    o_ref[0] = (ot.T * mg_ref[0]).astype(BF16)


def _mla_call(mqt, mk, mvt, mg, tile):
    B, nt, _, _ = mqt.shape
    S = nt * tile
    pairs = MLA_HEADS // 2
    return pl.pallas_call(
        functools.partial(_mla_kernel, tq=tile, tk=tile),
        grid=(B, pairs, nt),
        in_specs=[
            pl.BlockSpec((1, 1, 2 * MLA_SLOT, tile), lambda b, hp, i: (b, i, hp, 0)),
            pl.BlockSpec((1, S, 2 * MLA_SLOT), lambda b, hp, i: (b, 0, hp)),
            pl.BlockSpec((1, nt, 2 * MLA_V, tile), lambda b, hp, i: (b, 0, hp, 0)),
            pl.BlockSpec((1, tile, 2 * MLA_V), lambda b, hp, i: (b, i, hp)),
        ],
        out_specs=pl.BlockSpec((1, tile, 2 * MLA_V), lambda b, hp, i: (b, i, hp)),
        out_shape=jax.ShapeDtypeStruct((B, S, MLA_WIDTH), BF16),
        scratch_shapes=[pltpu.VMEM((2, 1, tile), F32),
                        pltpu.VMEM((2, MLA_ACC_ROWS, tile), F32),
                        pltpu.VMEM((2, tile, tile), F32), pltpu.VMEM((2, tile, tile), F32)],
        compiler_params=pltpu.CompilerParams(
            dimension_semantics=("arbitrary", "arbitrary", "arbitrary"),
            vmem_limit_bytes=VMEM_LIMIT_BYTES),
        name="mla",
    )(mqt, mk, mvt, mg)


def _memkv_kernel(mem_ref, g_ref, wk_ref, wv_ref, k_ref, v_ref):
    memn = _rms(mem_ref[0], g_ref[...]).astype(BF16)
    k_ref[...] = _dot(memn, wk_ref[...].astype(BF16)).astype(BF16)
    v_ref[...] = _dot(memn, wv_ref[...].astype(BF16)).astype(BF16)


def _memkv_call(mem, norm_mem, wk, wv):
    B, M, D = mem.shape
    L = wk.shape[0]
    kv_shape = jax.ShapeDtypeStruct((L, B, M, MEM_INNER), BF16)
    kv_spec = pl.BlockSpec((None, None, M, MEM_INNER), lambda l, b: (l, b, 0, 0))
    return pl.pallas_call(
        _memkv_kernel,
        grid=(L, B),
        in_specs=[pl.BlockSpec((1, M, D), lambda l, b: (b, 0, 0)),
                  pl.BlockSpec((None, 1, D), lambda l, b: (l, 0, 0)),
                  pl.BlockSpec((None, D, MEM_INNER), lambda l, b: (l, 0, 0)),
                  pl.BlockSpec((None, D, MEM_INNER), lambda l, b: (l, 0, 0))],
        out_specs=[kv_spec, kv_spec],
        out_shape=[kv_shape, kv_shape],
        compiler_params=pltpu.CompilerParams(
            dimension_semantics=("arbitrary", "arbitrary"),
            vmem_limit_bytes=VMEM_LIMIT_BYTES),
        name="memkv",
    )(mem, norm_mem, wk, wv)


def _post_kernel(h_ref, og_ref, oc_ref, om_ref, wout_ref, gx_ref, wq_ref, mk_ref, mv_ref,
                 wo_ref, gf_ref, o_ref, wout_bf, wq_bf, wo_bf, *, final_norm):
    @pl.when((pl.program_id(0) == 0) & (pl.program_id(1) == 0))
    def _():
        wout_bf[...] = wout_ref[...].astype(BF16)
        wq_bf[...] = wq_ref[...].astype(BF16)
        wo_bf[...] = wo_ref[...].astype(BF16)

    x = jnp.concatenate([og_ref[0], oc_ref[0], om_ref[0]], axis=1)
    h1 = h_ref[0] + _dot(x, wout_bf[...])
    hn = _rms(h1, gx_ref[...]).astype(BF16)
    q = (_dot(hn, wq_bf[...]) * (LOG2E / math.sqrt(MEM_HEAD_DIM))).astype(BF16)
    ones = jnp.ones((mv_ref.shape[0], MEM_HEAD_DIM), BF16)
    heads = []
    for hd in range(MEM_HEADS):
        sl = slice(hd * MEM_HEAD_DIM, (hd + 1) * MEM_HEAD_DIM)
        s = _dot_nt(q[:, sl], mk_ref[:, sl])
        p = jnp.exp2(s - jnp.max(s, axis=1, keepdims=True)).astype(BF16)
        nd = _dot(p, jnp.concatenate([mv_ref[:, sl], ones], axis=1))
        heads.append((nd[:, :MEM_HEAD_DIM] / nd[:, MEM_HEAD_DIM:]).astype(BF16))
    o = jnp.concatenate(heads, axis=1)
    h2 = h1 + _dot(o, wo_bf[...])
    if final_norm:
        h2 = _rms(h2, gf_ref[...])
    o_ref[0] = h2


def _post_call(h, og, oc, om, memk, memv, l, p, norm_final, tile, final_norm):
    B, S, D = h.shape
    nt = S // tile
    M = memk.shape[2]

    def tok(w):
        return pl.BlockSpec((1, tile, w), lambda b, t: (b, t, 0))

    def lay(shape):
        return pl.BlockSpec((None,) + shape, lambda b, t: (l,) + (0,) * len(shape))

    mem_spec = pl.BlockSpec((None, None, M, MEM_INNER), lambda b, t: (l, b, 0, 0))
    return pl.pallas_call(
        functools.partial(_post_kernel, final_norm=final_norm),
        grid=(B, nt),
        in_specs=[tok(D), tok(GLA_WIDTH), tok(CONV_WIDTH), tok(MLA_WIDTH),
                  lay((D_MIX, D)), lay((1, D)), lay((D, MEM_INNER)), mem_spec, mem_spec,
                  lay((MEM_INNER, D)), pl.BlockSpec((1, D), lambda b, t: (0, 0))],
        out_specs=tok(D),
        out_shape=jax.ShapeDtypeStruct((B, S, D), F32),
        scratch_shapes=[pltpu.VMEM((D_MIX, D), BF16), pltpu.VMEM((D, MEM_INNER), BF16),
                        pltpu.VMEM((MEM_INNER, D), BF16)],
        compiler_params=pltpu.CompilerParams(
            dimension_semantics=("arbitrary", "arbitrary"),
            vmem_limit_bytes=VMEM_LIMIT_BYTES),
        name="post",
    )(h, og, oc, om, p["w_out"], p["norm_xattn"], p["mem_wq"], memk, memv, p["mem_wo"],
      norm_final)


def _pad_cols(w, width):
    return jnp.pad(w, [(0, 0)] * (w.ndim - 1) + [(0, width - w.shape[-1])])


def _pack_w_in_kernel(w_ref, o_ref):
    w = w_ref[0]
    rows = w.shape[0]
    o_gv = 2 * GLA_QK
    o_glr = o_gv + GLA_WIDTH
    o_gg = o_glr + GLA_GATE_RANK
    o_cc = o_gg + GLA_WIDTH
    o_kr = o_cc + 4 * CONV_WIDTH + MLA_Q_RANK + MLA_KV_RANK
    o_mg = o_kr + MLA_ROPE
    zeros = lambda n: jnp.zeros((rows, n), w.dtype)
    parts = [
        w[:, 0:GLA_QK], zeros(GLA_QK_PAD - GLA_QK),
        w[:, GLA_QK:o_gv], zeros(GLA_QK_PAD - GLA_QK),
        w[:, o_gv:o_glr], w[:, o_gg:o_cc], w[:, o_cc:o_kr], w[:, o_mg:o_mg + MLA_WIDTH],
        zeros(KR_ROPE_OFF), w[:, o_kr:o_mg], w[:, o_glr:o_gg],
        zeros(MLA_SLOT - KR_GATE_OFF - GLA_GATE_RANK),
    ]
    o_ref[0] = jnp.concatenate(parts, axis=1).astype(BF16)


def _pack_w_in(w_in):
    L, D, W = w_in.shape
    rows = 256
    return pl.pallas_call(
        _pack_w_in_kernel,
        grid=(L, D // rows),
        in_specs=[pl.BlockSpec((1, rows, W), lambda l, r: (l, r, 0))],
        out_specs=pl.BlockSpec((1, rows, IN_PAD), lambda l, r: (l, r, 0)),
        out_shape=jax.ShapeDtypeStruct((L, D, IN_PAD), BF16),
        compiler_params=pltpu.CompilerParams(
            dimension_semantics=("arbitrary", "arbitrary"),
            vmem_limit_bytes=VMEM_LIMIT_BYTES),
        name="pack_w_in",
    )(w_in)


def _prep_params(norm_mix, w_in, gla_w_gate, gla_b_gate, gla_norm, conv_w, mla_q_norm, mla_w_uq,
                 mla_kv_norm, mla_w_ukv, w_out, norm_xattn, mem_wq, mem_wo):
    L = w_in.shape[0]
    wg = jnp.zeros((L, MLA_SLOT, GLA_QK_PAD), F32)
    wg = wg.at[:, KR_GATE_OFF:KR_GATE_OFF + GLA_GATE_RANK, :GLA_QK].set(gla_w_gate)
    w_uq = mla_w_uq.reshape(L, MLA_Q_RANK, MLA_HEADS, MLA_NOPE + MLA_ROPE)
    w_uq = _pad_cols(w_uq, MLA_SLOT).reshape(L, MLA_Q_RANK, MLA_HEADS * MLA_SLOT)
    w_ukv = mla_w_ukv.reshape(L, MLA_KV_RANK, MLA_HEADS, MLA_NOPE + MLA_V)
    w_uk = _pad_cols(w_ukv[..., :MLA_NOPE], MLA_SLOT).reshape(L, MLA_KV_RANK, MLA_HEADS * MLA_SLOT)
    w_uv = w_ukv[..., MLA_NOPE:].reshape(L, MLA_KV_RANK, MLA_WIDTH)
    return {
        "norm_mix": norm_mix[:, None, :],
        "w_in": _pack_w_in(w_in),
        "wg": wg.astype(BF16),
        "bg": _pad_cols(gla_b_gate, GLA_QK_PAD)[:, None, :],
        "gla_norm": jnp.tile(gla_norm, (1, GLA_HEADS))[:, None, :],
        "conv_w": conv_w,
        "q_norm": mla_q_norm[:, None, :],
        "w_uq": w_uq.astype(BF16),
        "kv_norm": mla_kv_norm[:, None, :],
        "w_ukv": jnp.concatenate([w_uk, w_uv], axis=-1).astype(BF16),
        "w_out": w_out,
        "norm_xattn": norm_xattn[:, None, :],
        "mem_wq": mem_wq,
        "mem_wo": mem_wo,
    }


def _rope_tables(positions):
    inv_freq = 1.0 / (ROPE_BASE ** (jnp.arange(0, MLA_ROPE, 2, dtype=F32) / MLA_ROPE))
    lane = jnp.arange(MLA_SLOT)
    lane_freq = inv_freq[(lane - MLA_NOPE) % ROPE_HALF]
    ang = positions.astype(F32)[..., None] * lane_freq
    cos, sin = jnp.cos(ang), jnp.sin(ang)
    first = (lane >= MLA_NOPE) & (lane < MLA_NOPE + ROPE_HALF)
    second = (lane >= MLA_NOPE + ROPE_HALF) & (lane < MLA_NOPE + MLA_ROPE)
    rc = jnp.where(lane < MLA_NOPE, 1.0, jnp.where(first | second, cos, 0.0))
    rs1 = jnp.where(second, sin, 0.0)
    rs2 = jnp.where(first, -sin, 0.0)
    return rc, rs1, rs2


def _pick_tile(S, want):
    t = min(S, want)
    assert S % t == 0 and t % GLA_CHUNK == 0
    return t


def kernel(x, mem, positions, norm_mix, w_in, gla_w_gate, gla_b_gate, gla_norm, conv_w, mla_q_norm,
           mla_w_uq, mla_kv_norm, mla_w_ukv, w_out, norm_xattn, norm_mem, mem_wq, mem_wk, mem_wv,
           mem_wo, norm_final):
    depth = w_in.shape[0]
    S = x.shape[1]
    tile = _pick_tile(S, 512)
    p = _prep_params(norm_mix, w_in, gla_w_gate, gla_b_gate, gla_norm, conv_w, mla_q_norm,
                     mla_w_uq, mla_kv_norm, mla_w_ukv, w_out, norm_xattn, mem_wq, mem_wo)
    tables = _rope_tables(positions)
    memk, memv = _memkv_call(mem, norm_mem[:, None, :], mem_wk, mem_wv)
    nf = norm_final[None, :]
    h = x
    for l in range(depth):
        gq, gk, gla, gv, gg, oc, mq, mk, mv, mg = _proj_call(h, l, p, tables, tile)
        og = _gla_call(gq, gk, gla, gv, gg, l, p, tile)
        om = _mla_call(mq, mk, mv, mg, tile)
        h = _post_call(h, og, oc, om, memk, memv, l, p, nf, tile, l == depth - 1)
    return h
```

```python
import functools
import math

import jax
import jax.numpy as jnp
from jax import lax
from jax.experimental import pallas as pl
from jax.experimental.pallas import tpu as pltpu

F32 = jnp.float32
BF16 = jnp.bfloat16

D_MODEL = 1024
EPS = 1e-6

GLA_HEADS = 4
GLA_DV = 96
GLA_DK = 48
GLA_QK = GLA_HEADS * GLA_DK
GLA_WIDTH = GLA_HEADS * GLA_DV
GLA_GATE_RANK = 16
GLA_TAU = 16.0
GLA_CHUNK = 64
GLA_QK_PAD = 256

CONV_WIDTH = 256
CONV_K = 3

MLA_HEADS = 6
MLA_NOPE = 64
MLA_ROPE = 32
MLA_V = 64
MLA_Q_RANK = 256
MLA_KV_RANK = 256
MLA_WIDTH = MLA_HEADS * MLA_V
MLA_SLOT = 128
MLA_ACC_ROWS = MLA_V + 16
ROPE_BASE = 10000.0
ROPE_HALF = MLA_ROPE // 2

D_MIX = GLA_WIDTH + CONV_WIDTH + MLA_WIDTH

MEM_HEADS = 4
MEM_HEAD_DIM = 128
MEM_INNER = MEM_HEADS * MEM_HEAD_DIM

LOG2E = math.log2(math.e)
MASK_VALUE = -1e30

SEG_Q = 0
SEG_K = SEG_Q + GLA_QK_PAD
SEG_V = SEG_K + GLA_QK_PAD
SEG_GG = SEG_V + GLA_WIDTH
SEG_CC = SEG_GG + GLA_WIDTH
SEG_CB = SEG_CC + CONV_WIDTH
SEG_CH = SEG_CB + CONV_WIDTH
SEG_CG = SEG_CH + CONV_WIDTH
SEG_CQ = SEG_CG + CONV_WIDTH
SEG_CKV = SEG_CQ + MLA_Q_RANK
SEG_MG = SEG_CKV + MLA_KV_RANK
SEG_KR = SEG_MG + MLA_WIDTH
IN_PAD = SEG_KR + MLA_SLOT
KR_ROPE_OFF = MLA_NOPE
KR_GATE_OFF = MLA_NOPE + MLA_ROPE

VMEM_LIMIT_BYTES = 56 * 1024 * 1024


def _silu(x):
    return x * (1.0 / (1.0 + jnp.exp(-x)))


def _rms(x, g):
    return x * lax.rsqrt(jnp.mean(x * x, axis=-1, keepdims=True) + EPS) * g


def _dot(a, b):
    return jnp.dot(a, b, preferred_element_type=F32)


def _dot_nt(a, b):
    return lax.dot_general(a, b, (((1,), (1,)), ((), ())), preferred_element_type=F32)


def _dot_tn(a, b):
    return lax.dot_general(a, b, (((0,), (0,)), ((), ())), preferred_element_type=F32)


def _rope_slot(x, c, s1, s2):
    return x * c + pltpu.roll(x, ROPE_HALF, 1) * s1 + pltpu.roll(x, MLA_SLOT - ROPE_HALF, 1) * s2


def _proj_kernel(h_ref, g_ref, win_ref, wg_ref, bg_ref, convw_ref, qn_ref, wuq_ref, kvn_ref,
                 wukv_ref, rc_ref, rs1_ref, rs2_ref,
                 gq_ref, gk_ref, gla_ref, gv_ref, gg_ref, oconv_ref, mq_ref, mk_ref, mv_ref,
                 mg_ref, ubuf_ref, *, tile, q_scale):
    t = pl.program_id(1)
    x = h_ref[0]
    xn = _rms(x, g_ref[...]).astype(BF16)

    def seg(a, b):
        return _dot(xn, win_ref[:, a:b])

    gq_ref[0] = seg(SEG_Q, SEG_K)
    gk_ref[0] = seg(SEG_K, SEG_V)
    gv_ref[0] = seg(SEG_V, SEG_GG)
    gg_ref[0] = _silu(seg(SEG_GG, SEG_CC))
    kr = seg(SEG_KR, IN_PAD)
    z = _dot(kr.astype(BF16), wg_ref[...]) + bg_ref[...]
    gla_ref[0] = (jnp.minimum(z, 0.0) - jnp.log1p(jnp.exp(-jnp.abs(z)))) * (1.0 / GLA_TAU)

    u = seg(SEG_CC, SEG_CB) * seg(SEG_CH, SEG_CG)

    @pl.when(t == 0)
    def _():
        ubuf_ref[0:8, :] = jnp.zeros((8, CONV_WIDTH), F32)

    ubuf_ref[8:8 + tile, :] = u
    cw = convw_ref[...]
    conv = (cw[0:1, :] * ubuf_ref[6:6 + tile, :] + cw[1:2, :] * ubuf_ref[7:7 + tile, :]
            + cw[2:3, :] * u)
    oconv_ref[0] = (seg(SEG_CB, SEG_CH) * conv * _silu(seg(SEG_CG, SEG_CQ))).astype(BF16)
    ubuf_ref[0:8, :] = u[tile - 8:tile, :]

    rc = rc_ref[0]
    rs1 = rs1_ref[0]
    rs2 = rs2_ref[0]
    cqn = _rms(seg(SEG_CQ, SEG_CKV), qn_ref[...]).astype(BF16)
    qh = _dot(cqn, wuq_ref[...])
    for h in range(MLA_HEADS):
        sl = slice(h * MLA_SLOT, (h + 1) * MLA_SLOT)
        qf = _rope_slot(qh[:, sl], rc, rs1, rs2) * q_scale
        mq_ref[0, 0, sl, :] = qf.T.astype(BF16)
    ckvn = _rms(seg(SEG_CKV, SEG_MG), kvn_ref[...]).astype(BF16)
    kvh = _dot(ckvn, wukv_ref[...])
    lane = lax.broadcasted_iota(jnp.int32, (1, MLA_SLOT), 1)
    in_rope = (lane >= KR_ROPE_OFF) & (lane < KR_ROPE_OFF + MLA_ROPE)
    krope = jnp.where(in_rope, _rope_slot(kr, rc, rs1, rs2), 0.0)
    for h in range(MLA_HEADS):
        sl = slice(h * MLA_SLOT, (h + 1) * MLA_SLOT)
        mk_ref[0, :, sl] = (kvh[:, sl] + krope).astype(BF16)
    mv_ref[0, 0] = kvh[:, MLA_HEADS * MLA_SLOT:].T.astype(BF16)
    mg_ref[0] = _silu(seg(SEG_MG, SEG_KR))


def _proj_call(h, l, p, tables, tile):
    B, S, D = h.shape
    nt = S // tile
    q_scale = LOG2E / math.sqrt(MLA_NOPE + MLA_ROPE)

    def tok(w):
        return pl.BlockSpec((1, tile, w), lambda b, t: (b, t, 0))

    def lay(shape):
        return pl.BlockSpec((None,) + shape, lambda b, t: (l,) + (0,) * len(shape))

    in_specs = [
        tok(D),
        lay((1, D)),
        lay((D, IN_PAD)),
        lay((MLA_SLOT, GLA_QK_PAD)),
        lay((1, GLA_QK_PAD)),
        lay((CONV_K, CONV_WIDTH)),
        lay((1, MLA_Q_RANK)),
        lay((MLA_Q_RANK, MLA_HEADS * MLA_SLOT)),
        lay((1, MLA_KV_RANK)),
        lay((MLA_KV_RANK, MLA_HEADS * MLA_SLOT + MLA_WIDTH)),
        tok(MLA_SLOT), tok(MLA_SLOT), tok(MLA_SLOT),
    ]
    def tok_out(w, dt):
        return tok(w), jax.ShapeDtypeStruct((B, S, w), dt)

    def tile_t_out(w, dt):
        return (pl.BlockSpec((1, 1, w, tile), lambda b, t: (b, t, 0, 0)),
                jax.ShapeDtypeStruct((B, nt, w, tile), dt))

    outs = [
        tok_out(GLA_QK_PAD, F32), tok_out(GLA_QK_PAD, F32), tok_out(GLA_QK_PAD, F32),
        tok_out(GLA_WIDTH, F32), tok_out(GLA_WIDTH, F32), tok_out(CONV_WIDTH, BF16),
        tile_t_out(MLA_HEADS * MLA_SLOT, BF16), tok_out(MLA_HEADS * MLA_SLOT, BF16),
        tile_t_out(MLA_WIDTH, BF16), tok_out(MLA_WIDTH, F32),
    ]
    return pl.pallas_call(
        functools.partial(_proj_kernel, tile=tile, q_scale=q_scale),
        grid=(B, nt),
        in_specs=in_specs,
        out_specs=[spec for spec, _ in outs],
        out_shape=[shape for _, shape in outs],
        scratch_shapes=[pltpu.VMEM((tile + 8, CONV_WIDTH), F32)],
        compiler_params=pltpu.CompilerParams(
            dimension_semantics=("arbitrary", "arbitrary"),
            vmem_limit_bytes=VMEM_LIMIT_BYTES),
        name="proj",
    )(h, p["norm_mix"], p["w_in"], p["wg"], p["bg"], p["conv_w"], p["q_norm"], p["w_uq"],
      p["kv_norm"], p["w_ukv"], *tables)


def _gla_kernel(q_ref, k_ref, la_ref, v_ref, gg_ref, gn_ref, o_ref, st_ref, *, tile):
    t = pl.program_id(1)
    C = GLA_CHUNK

    @pl.when(t == 0)
    def _():
        st_ref[...] = jnp.zeros(st_ref.shape, F32)

    lane_k = lax.broadcasted_iota(jnp.int32, (1, GLA_QK_PAD), 1)
    lane_v = lax.broadcasted_iota(jnp.int32, (1, GLA_WIDTH), 1)
    kmask = [(lane_k >= h * GLA_DK) & (lane_k < (h + 1) * GLA_DK) for h in range(GLA_HEADS)]
    vmask = [(lane_v >= h * GLA_DV) & (lane_v < (h + 1) * GLA_DV) for h in range(GLA_HEADS)]
    row = lax.broadcasted_iota(jnp.int32, (C, 2 * C), 0)
    col = lax.broadcasted_iota(jnp.int32, (C, 2 * C), 1)
    cum2 = (row >= (col & (C - 1))).astype(BF16)
    hrow = lax.broadcasted_iota(jnp.int32, (C, GLA_HEADS * C), 0)
    hcol = lax.broadcasted_iota(jnp.int32, (C, GLA_HEADS * C), 1)
    tril_heads = hrow >= (hcol & (C - 1))
    srow = lax.broadcasted_iota(jnp.int32, (GLA_WIDTH, GLA_QK_PAD), 0)
    slane = lax.broadcasted_iota(jnp.int32, (GLA_WIDTH, GLA_QK_PAD), 1)
    same_head = None
    for h in range(GLA_HEADS):
        blk = ((srow >= h * GLA_DV) & (srow < (h + 1) * GLA_DV)
               & (slane >= h * GLA_DK) & (slane < (h + 1) * GLA_DK))
        same_head = blk if same_head is None else (same_head | blk)
    gn = gn_ref[...]

    st = st_ref[...]
    for c in range(tile // C):
        sl = slice(c * C, (c + 1) * C)
        q = q_ref[0, sl, :]
        k = k_ref[0, sl, :]
        la = la_ref[0, sl, :]
        v = v_ref[0, sl, :]
        la_hi = la.astype(BF16)
        la_lo = (la - la_hi.astype(F32)).astype(BF16)
        b = _dot(cum2, jnp.concatenate([la_hi, la_lo], axis=0))
        b_last = b[C - 1:C, :]
        q_dec = (q * (GLA_DK ** -0.5) * jnp.exp(b)).astype(BF16)
        k_inv = k * jnp.exp(-b)
        k_end = (k * jnp.exp(b_last - b)).astype(BF16)
        decay = jnp.exp(b_last)
        k_heads = jnp.concatenate(
            [jnp.where(kmask[h], k_inv, 0.0) for h in range(GLA_HEADS)], axis=0).astype(BF16)
        a = _dot_nt(q_dec, k_heads)
        a = jnp.where(tril_heads, a, 0.0).astype(BF16)
        v_heads = jnp.concatenate(
            [jnp.where(vmask[h], v, 0.0) for h in range(GLA_HEADS)], axis=0).astype(BF16)
        o = _dot(a, v_heads) + _dot_nt(q_dec, st.astype(BF16))
        upd = _dot_tn(v.astype(BF16), k_end)
        st = st * decay + jnp.where(same_head, upd, 0.0)
        o2 = o * o
        ms = jnp.zeros_like(o)
        for h in range(GLA_HEADS):
            mh = jnp.sum(jnp.where(vmask[h], o2, 0.0), axis=1, keepdims=True) * (1.0 / GLA_DV)
            ms = jnp.where(vmask[h], mh, ms)
        y = o * lax.rsqrt(ms + EPS) * gn
        o_ref[0, sl, :] = (y * gg_ref[0, sl, :]).astype(BF16)
    st_ref[...] = st


def _gla_call(gq, gk, gla, gv, gg, l, p, tile):
    B, S, _ = gq.shape
    nt = S // tile

    def tok(w):
        return pl.BlockSpec((1, tile, w), lambda b, t: (b, t, 0))

    return pl.pallas_call(
        functools.partial(_gla_kernel, tile=tile),
        grid=(B, nt),
        in_specs=[tok(GLA_QK_PAD), tok(GLA_QK_PAD), tok(GLA_QK_PAD), tok(GLA_WIDTH),
                  tok(GLA_WIDTH),
                  pl.BlockSpec((None, 1, GLA_WIDTH), lambda b, t: (l, 0, 0))],
        out_specs=tok(GLA_WIDTH),
        out_shape=jax.ShapeDtypeStruct((B, S, GLA_WIDTH), BF16),
        scratch_shapes=[pltpu.VMEM((GLA_WIDTH, GLA_QK_PAD), F32)],
        compiler_params=pltpu.CompilerParams(
            dimension_semantics=("arbitrary", "arbitrary"),
            vmem_limit_bytes=VMEM_LIMIT_BYTES),
        name="gla",
    )(gq, gk, gla, gv, gg, p["gla_norm"])


def _mla_kernel(qt_ref, k_ref, vt_ref, mg_ref, o_ref, m_ref, acc_ref, s_ref, *, tile):
    qi = pl.program_id(1)
    krow = lax.broadcasted_iota(jnp.int32, (tile, tile), 0)
    qcol = lax.broadcasted_iota(jnp.int32, (tile, tile), 1)
    causal = krow <= qcol
    ones_rows = jnp.ones((MLA_ACC_ROWS - MLA_V, tile), BF16)

    m_ref[...] = jnp.full(m_ref.shape, MASK_VALUE, F32)
    acc_ref[...] = jnp.zeros(acc_ref.shape, F32)

    def scores(j, e):
        rows = pl.ds(pl.multiple_of(j * tile, tile), tile)
        kj = k_ref[0, rows, e * MLA_SLOT:(e + 1) * MLA_SLOT]
        qt = qt_ref[0, 0, e * MLA_SLOT:(e + 1) * MLA_SLOT, :]
        s_ref[e] = _dot(kj, qt)

    def consume(j, e, masked):
        s = s_ref[e]
        if masked:
            s = jnp.where(causal, s, MASK_VALUE)
        m_prev = m_ref[e]
        m_new = jnp.maximum(m_prev, jnp.max(s, axis=0, keepdims=True))
        alpha = jnp.exp2(m_prev - m_new)
        p = jnp.exp2(s - m_new).astype(BF16)
        vt = jnp.concatenate([vt_ref[0, j, e * MLA_V:(e + 1) * MLA_V, :], ones_rows], axis=0)
        acc_ref[e] = alpha * acc_ref[e] + _dot(vt, p)
        m_ref[e] = m_new

    scores(0, 0)

    def body(j, carry):
        for e in range(MLA_HEADS):
            if e + 1 < MLA_HEADS:
                scores(j, e + 1)
            else:
                scores(j + 1, 0)
            consume(j, e, False)
        return carry

    lax.fori_loop(0, qi, body, 0)
    for e in range(MLA_HEADS):
        if e + 1 < MLA_HEADS:
            scores(qi, e + 1)
        consume(qi, e, True)

    ot = jnp.concatenate(
        [acc_ref[e, 0:MLA_V, :] / acc_ref[e, MLA_V:MLA_V + 1, :] for e in range(MLA_HEADS)],
        axis=0)
    o_ref[0] = (ot.T * mg_ref[0]).astype(BF16)


def _mla_call(mqt, mk, mvt, mg, tile):
    B, nt, _, _ = mqt.shape
    S = nt * tile
    return pl.pallas_call(
        functools.partial(_mla_kernel, tile=tile),
        grid=(B, nt),
        in_specs=[
            pl.BlockSpec((1, 1, MLA_HEADS * MLA_SLOT, tile), lambda b, i: (b, i, 0, 0)),
            pl.BlockSpec((1, S, MLA_HEADS * MLA_SLOT), lambda b, i: (b, 0, 0)),
            pl.BlockSpec((1, nt, MLA_WIDTH, tile), lambda b, i: (b, 0, 0, 0)),
            pl.BlockSpec((1, tile, MLA_WIDTH), lambda b, i: (b, i, 0)),
        ],
        out_specs=pl.BlockSpec((1, tile, MLA_WIDTH), lambda b, i: (b, i, 0)),
        out_shape=jax.ShapeDtypeStruct((B, S, MLA_WIDTH), BF16),
        scratch_shapes=[pltpu.VMEM((MLA_HEADS, 1, tile), F32),
                        pltpu.VMEM((MLA_HEADS, MLA_ACC_ROWS, tile), F32),
                        pltpu.VMEM((MLA_HEADS, tile, tile), F32)],
        compiler_params=pltpu.CompilerParams(
            dimension_semantics=("arbitrary", "arbitrary"),
            vmem_limit_bytes=VMEM_LIMIT_BYTES),
        name="mla",
    )(mqt, mk, mvt, mg)


def _memkv_kernel(mem_ref, g_ref, wk_ref, wv_ref, k_ref, v_ref):
    memn = _rms(mem_ref[0], g_ref[...]).astype(BF16)
    k_ref[...] = _dot(memn, wk_ref[...].astype(BF16)).astype(BF16)
    v_ref[...] = _dot(memn, wv_ref[...].astype(BF16)).astype(BF16)


def _memkv_call(mem, norm_mem, wk, wv):
    B, M, D = mem.shape
    L = wk.shape[0]
    kv_shape = jax.ShapeDtypeStruct((L, B, M, MEM_INNER), BF16)
    kv_spec = pl.BlockSpec((None, None, M, MEM_INNER), lambda l, b: (l, b, 0, 0))
    return pl.pallas_call(
        _memkv_kernel,
        grid=(L, B),
        in_specs=[pl.BlockSpec((1, M, D), lambda l, b: (b, 0, 0)),
                  pl.BlockSpec((None, 1, D), lambda l, b: (l, 0, 0)),
                  pl.BlockSpec((None, D, MEM_INNER), lambda l, b: (l, 0, 0)),
                  pl.BlockSpec((None, D, MEM_INNER), lambda l, b: (l, 0, 0))],
        out_specs=[kv_spec, kv_spec],
        out_shape=[kv_shape, kv_shape],
        compiler_params=pltpu.CompilerParams(
            dimension_semantics=("arbitrary", "arbitrary"),
            vmem_limit_bytes=VMEM_LIMIT_BYTES),
        name="memkv",
    )(mem, norm_mem, wk, wv)


def _post_kernel(h_ref, og_ref, oc_ref, om_ref, wout_ref, gx_ref, wq_ref, mk_ref, mv_ref,
                 wo_ref, gf_ref, o_ref, wout_bf, wq_bf, wo_bf, *, final_norm):
    @pl.when((pl.program_id(0) == 0) & (pl.program_id(1) == 0))
    def _():
        wout_bf[...] = wout_ref[...].astype(BF16)
        wq_bf[...] = wq_ref[...].astype(BF16)
        wo_bf[...] = wo_ref[...].astype(BF16)

    x = jnp.concatenate([og_ref[0], oc_ref[0], om_ref[0]], axis=1)
    h1 = h_ref[0] + _dot(x, wout_bf[...])
    hn = _rms(h1, gx_ref[...]).astype(BF16)
    q = (_dot(hn, wq_bf[...]) * (LOG2E / math.sqrt(MEM_HEAD_DIM))).astype(BF16)
    ones = jnp.ones((mv_ref.shape[0], MEM_HEAD_DIM), BF16)
    heads = []
    for hd in range(MEM_HEADS):
        sl = slice(hd * MEM_HEAD_DIM, (hd + 1) * MEM_HEAD_DIM)
        s = _dot_nt(q[:, sl], mk_ref[:, sl])
        p = jnp.exp2(s - jnp.max(s, axis=1, keepdims=True)).astype(BF16)
        nd = _dot(p, jnp.concatenate([mv_ref[:, sl], ones], axis=1))
        heads.append((nd[:, :MEM_HEAD_DIM] / nd[:, MEM_HEAD_DIM:]).astype(BF16))
    o = jnp.concatenate(heads, axis=1)
    h2 = h1 + _dot(o, wo_bf[...])
    if final_norm:
        h2 = _rms(h2, gf_ref[...])
    o_ref[0] = h2


def _post_call(h, og, oc, om, memk, memv, l, p, norm_final, tile, final_norm):
    B, S, D = h.shape
    nt = S // tile
    M = memk.shape[2]

    def tok(w):
        return pl.BlockSpec((1, tile, w), lambda b, t: (b, t, 0))

    def lay(shape):
        return pl.BlockSpec((None,) + shape, lambda b, t: (l,) + (0,) * len(shape))

    mem_spec = pl.BlockSpec((None, None, M, MEM_INNER), lambda b, t: (l, b, 0, 0))
    return pl.pallas_call(
        functools.partial(_post_kernel, final_norm=final_norm),
        grid=(B, nt),
        in_specs=[tok(D), tok(GLA_WIDTH), tok(CONV_WIDTH), tok(MLA_WIDTH),
                  lay((D_MIX, D)), lay((1, D)), lay((D, MEM_INNER)), mem_spec, mem_spec,
                  lay((MEM_INNER, D)), pl.BlockSpec((1, D), lambda b, t: (0, 0))],
        out_specs=tok(D),
        out_shape=jax.ShapeDtypeStruct((B, S, D), F32),
        scratch_shapes=[pltpu.VMEM((D_MIX, D), BF16), pltpu.VMEM((D, MEM_INNER), BF16),
                        pltpu.VMEM((MEM_INNER, D), BF16)],
        compiler_params=pltpu.CompilerParams(
            dimension_semantics=("arbitrary", "arbitrary"),
            vmem_limit_bytes=VMEM_LIMIT_BYTES),
        name="post",
    )(h, og, oc, om, p["w_out"], p["norm_xattn"], p["mem_wq"], memk, memv, p["mem_wo"],
      norm_final)


def _pad_cols(w, width):
    return jnp.pad(w, [(0, 0)] * (w.ndim - 1) + [(0, width - w.shape[-1])])


def _pack_w_in_kernel(wt_ref, o_ref):
    cols = wt_ref.shape[2]
    o_gv = 2 * GLA_QK
    o_glr = o_gv + GLA_WIDTH
    o_gg = o_glr + GLA_GATE_RANK
    o_cc = o_gg + GLA_WIDTH
    o_kr = o_cc + 4 * CONV_WIDTH + MLA_Q_RANK + MLA_KV_RANK
    o_mg = o_kr + MLA_ROPE
    zeros = lambda n: jnp.zeros((n, cols), F32)

    def rows(a, b):
        return wt_ref[0, a:b, :]

    pieces = [
        (SEG_Q, [rows(0, GLA_QK), zeros(GLA_QK_PAD - GLA_QK)]),
        (SEG_K, [rows(GLA_QK, o_gv), zeros(GLA_QK_PAD - GLA_QK)]),
        (SEG_V, [rows(o_gv, o_glr)]),
        (SEG_GG, [rows(o_gg, o_cc)]),
        (SEG_CC, [rows(o_cc, o_kr)]),
        (SEG_MG, [rows(o_mg, o_mg + MLA_WIDTH)]),
        (SEG_KR, [zeros(KR_ROPE_OFF), rows(o_kr, o_mg), rows(o_glr, o_gg),
                  zeros(MLA_SLOT - KR_GATE_OFF - GLA_GATE_RANK)]),
    ]
    for off, parts in pieces:
        blk = parts[0] if len(parts) == 1 else jnp.concatenate(parts, axis=0)
        o_ref[0, :, off:off + blk.shape[0]] = blk.T.astype(BF16)


def _pack_w_in(w_in):
    wt = jnp.swapaxes(w_in, 1, 2)
    L, W, D = wt.shape
    cols = 256
    return pl.pallas_call(
        _pack_w_in_kernel,
        grid=(L, D // cols),
        in_specs=[pl.BlockSpec((1, W, cols), lambda l, c: (l, 0, c))],
        out_specs=pl.BlockSpec((1, cols, IN_PAD), lambda l, c: (l, c, 0)),
        out_shape=jax.ShapeDtypeStruct((L, D, IN_PAD), BF16),
        compiler_params=pltpu.CompilerParams(
            dimension_semantics=("arbitrary", "arbitrary"),
            vmem_limit_bytes=VMEM_LIMIT_BYTES),
        name="pack_w_in",
    )(wt)


def _prep_params(norm_mix, w_in, gla_w_gate, gla_b_gate, gla_norm, conv_w, mla_q_norm, mla_w_uq,
                 mla_kv_norm, mla_w_ukv, w_out, norm_xattn, mem_wq, mem_wo):
    L = w_in.shape[0]
    wg = jnp.zeros((L, MLA_SLOT, GLA_QK_PAD), F32)
    wg = wg.at[:, KR_GATE_OFF:KR_GATE_OFF + GLA_GATE_RANK, :GLA_QK].set(gla_w_gate)
    w_uq = mla_w_uq.reshape(L, MLA_Q_RANK, MLA_HEADS, MLA_NOPE + MLA_ROPE)
    w_uq = _pad_cols(w_uq, MLA_SLOT).reshape(L, MLA_Q_RANK, MLA_HEADS * MLA_SLOT)
    w_ukv = mla_w_ukv.reshape(L, MLA_KV_RANK, MLA_HEADS, MLA_NOPE + MLA_V)
    w_uk = _pad_cols(w_ukv[..., :MLA_NOPE], MLA_SLOT).reshape(L, MLA_KV_RANK, MLA_HEADS * MLA_SLOT)
    w_uv = w_ukv[..., MLA_NOPE:].reshape(L, MLA_KV_RANK, MLA_WIDTH)
    return {
        "norm_mix": norm_mix[:, None, :],
        "w_in": _pack_w_in(w_in),
        "wg": wg.astype(BF16),
        "bg": _pad_cols(gla_b_gate, GLA_QK_PAD)[:, None, :],
        "gla_norm": jnp.tile(gla_norm, (1, GLA_HEADS))[:, None, :],
        "conv_w": conv_w,
        "q_norm": mla_q_norm[:, None, :],
        "w_uq": w_uq.astype(BF16),
        "kv_norm": mla_kv_norm[:, None, :],
        "w_ukv": jnp.concatenate([w_uk, w_uv], axis=-1).astype(BF16),
        "w_out": w_out,
        "norm_xattn": norm_xattn[:, None, :],
        "mem_wq": mem_wq,
        "mem_wo": mem_wo,
    }


def _rope_tables(positions):
    inv_freq = 1.0 / (ROPE_BASE ** (jnp.arange(0, MLA_ROPE, 2, dtype=F32) / MLA_ROPE))
    lane = jnp.arange(MLA_SLOT)
    lane_freq = inv_freq[(lane - MLA_NOPE) % ROPE_HALF]
    ang = positions.astype(F32)[..., None] * lane_freq
    cos, sin = jnp.cos(ang), jnp.sin(ang)
    first = (lane >= MLA_NOPE) & (lane < MLA_NOPE + ROPE_HALF)
    second = (lane >= MLA_NOPE + ROPE_HALF) & (lane < MLA_NOPE + MLA_ROPE)
    rc = jnp.where(lane < MLA_NOPE, 1.0, jnp.where(first | second, cos, 0.0))
    rs1 = jnp.where(second, sin, 0.0)
    rs2 = jnp.where(first, -sin, 0.0)
    return rc, rs1, rs2


def _pick_tile(S, want):
    t = min(S, want)
    assert S % t == 0 and t % GLA_CHUNK == 0
    return t


def kernel(x, mem, positions, norm_mix, w_in, gla_w_gate, gla_b_gate, gla_norm, conv_w, mla_q_norm,
           mla_w_uq, mla_kv_norm, mla_w_ukv, w_out, norm_xattn, norm_mem, mem_wq, mem_wk, mem_wv,
           mem_wo, norm_final):
    depth = w_in.shape[0]
    S = x.shape[1]
    tile = _pick_tile(S, 512)
    p = _prep_params(norm_mix, w_in, gla_w_gate, gla_b_gate, gla_norm, conv_w, mla_q_norm,
                     mla_w_uq, mla_kv_norm, mla_w_ukv, w_out, norm_xattn, mem_wq, mem_wo)
    tables = _rope_tables(positions)
    memk, memv = _memkv_call(mem, norm_mem[:, None, :], mem_wk, mem_wv)
    nf = norm_final[None, :]
    h = x
    for l in range(depth):
        gq, gk, gla, gv, gg, oc, mq, mk, mv, mg = _proj_call(h, l, p, tables, tile)
        og = _gla_call(gq, gk, gla, gv, gg, l, p, tile)
        om = _mla_call(mq, mk, mv, mg, tile)
        h = _post_call(h, og, oc, om, memk, memv, l, p, nf, tile, l == depth - 1)
    return h
```

```python
import functools
import math

import jax
import jax.numpy as jnp
from jax import lax
from jax.experimental import pallas as pl
from jax.experimental.pallas import tpu as pltpu

F32 = jnp.float32
BF16 = jnp.bfloat16

D_MODEL = 1024
EPS = 1e-6

GLA_HEADS = 4
GLA_DV = 96
GLA_DK = 48
GLA_QK = GLA_HEADS * GLA_DK
GLA_WIDTH = GLA_HEADS * GLA_DV
GLA_GATE_RANK = 16
GLA_TAU = 16.0
GLA_CHUNK = 64
GLA_QK_PAD = 256

CONV_WIDTH = 256
CONV_K = 3

MLA_HEADS = 6
MLA_NOPE = 64
MLA_ROPE = 32
MLA_V = 64
MLA_Q_RANK = 256
MLA_KV_RANK = 256
MLA_WIDTH = MLA_HEADS * MLA_V
MLA_SLOT = 128
MLA_Q_NOPE_W = MLA_HEADS * MLA_NOPE
MLA_ACC_ROWS = MLA_V + 16
ROPE_BASE = 10000.0
ROPE_HALF = MLA_ROPE // 2

D_MIX = GLA_WIDTH + CONV_WIDTH + MLA_WIDTH

MEM_HEADS = 4
MEM_HEAD_DIM = 128
MEM_INNER = MEM_HEADS * MEM_HEAD_DIM

LOG2E = math.log2(math.e)
MASK_VALUE = -1e30

SEG_Q = 0
SEG_K = SEG_Q + GLA_QK_PAD
SEG_V = SEG_K + GLA_QK_PAD
SEG_GG = SEG_V + GLA_WIDTH
SEG_CC = SEG_GG + GLA_WIDTH
SEG_CB = SEG_CC + CONV_WIDTH
SEG_CH = SEG_CB + CONV_WIDTH
SEG_CG = SEG_CH + CONV_WIDTH
SEG_CQ = SEG_CG + CONV_WIDTH
SEG_CKV = SEG_CQ + MLA_Q_RANK
SEG_MG = SEG_CKV + MLA_KV_RANK
SEG_KR = SEG_MG + MLA_WIDTH
IN_PAD = SEG_KR + MLA_SLOT
KR_ROPE_OFF = MLA_NOPE
KR_GATE_OFF = MLA_NOPE + MLA_ROPE

VMEM_LIMIT_BYTES = 56 * 1024 * 1024
PROJ_TILE = 1024


def _silu(x):
    return x * (1.0 / (1.0 + jnp.exp(-x)))


def _rms(x, g):
    return x * lax.rsqrt(jnp.mean(x * x, axis=-1, keepdims=True) + EPS) * g


def _dot(a, b):
    return jnp.dot(a, b, preferred_element_type=F32)


def _dot_nt(a, b):
    return lax.dot_general(a, b, (((1,), (1,)), ((), ())), preferred_element_type=F32)


def _dot_tn(a, b):
    return lax.dot_general(a, b, (((0,), (0,)), ((), ())), preferred_element_type=F32)


def _gla_tile(q, k, la, v, gate, gn, st_ref, og_ref):
    C = GLA_CHUNK
    tile = q.shape[0]
    lane_k = lax.broadcasted_iota(jnp.int32, (1, GLA_QK_PAD), 1)
    lane_v = lax.broadcasted_iota(jnp.int32, (1, GLA_WIDTH), 1)
    kmask = [(lane_k >= h * GLA_DK) & (lane_k < (h + 1) * GLA_DK) for h in range(GLA_HEADS)]
    vmask = [(lane_v >= h * GLA_DV) & (lane_v < (h + 1) * GLA_DV) for h in range(GLA_HEADS)]
    row = lax.broadcasted_iota(jnp.int32, (C, 2 * C), 0)
    col = lax.broadcasted_iota(jnp.int32, (C, 2 * C), 1)
    cum2 = (row >= (col & (C - 1))).astype(BF16)
    hrow = lax.broadcasted_iota(jnp.int32, (C, GLA_HEADS * C), 0)
    hcol = lax.broadcasted_iota(jnp.int32, (C, GLA_HEADS * C), 1)
    tril_heads = hrow >= (hcol & (C - 1))
    srow = lax.broadcasted_iota(jnp.int32, (GLA_WIDTH, GLA_QK_PAD), 0)
    slane = lax.broadcasted_iota(jnp.int32, (GLA_WIDTH, GLA_QK_PAD), 1)
    same_head = None
    for h in range(GLA_HEADS):
        blk = ((srow >= h * GLA_DV) & (srow < (h + 1) * GLA_DV)
               & (slane >= h * GLA_DK) & (slane < (h + 1) * GLA_DK))
        same_head = blk if same_head is None else (same_head | blk)

    st = st_ref[...]
    for c in range(tile // C):
        sl = slice(c * C, (c + 1) * C)
        qc, kc, lac, vc = q[sl], k[sl], la[sl], v[sl]
        la_hi = lac.astype(BF16)
        la_lo = (lac - la_hi.astype(F32)).astype(BF16)
        b = _dot(cum2, jnp.concatenate([la_hi, la_lo], axis=0))
        b_last = b[C - 1:C, :]
        q_dec = (qc * (GLA_DK ** -0.5) * jnp.exp(b)).astype(BF16)
        k_inv = kc * jnp.exp(-b)
        k_end = (kc * jnp.exp(b_last - b)).astype(BF16)
        decay = jnp.exp(b_last)
        k_heads = jnp.concatenate(
            [jnp.where(kmask[h], k_inv, 0.0) for h in range(GLA_HEADS)], axis=0).astype(BF16)
        a = _dot_nt(q_dec, k_heads)
        a = jnp.where(tril_heads, a, 0.0).astype(BF16)
        v_heads = jnp.concatenate(
            [jnp.where(vmask[h], vc, 0.0) for h in range(GLA_HEADS)], axis=0).astype(BF16)
        o = _dot(a, v_heads) + _dot_nt(q_dec, st.astype(BF16))
        upd = _dot_tn(vc.astype(BF16), k_end)
        st = st * decay + jnp.where(same_head, upd, 0.0)
        o2 = o * o
        ms = jnp.zeros_like(o)
        for h in range(GLA_HEADS):
            mh = jnp.sum(jnp.where(vmask[h], o2, 0.0), axis=1, keepdims=True) * (1.0 / GLA_DV)
            ms = jnp.where(vmask[h], mh, ms)
        y = o * lax.rsqrt(ms + EPS) * gn
        og_ref[0, sl, :] = (y * gate[sl]).astype(BF16)
    st_ref[...] = st


def _proj_kernel(h_ref, g_ref, win_ref, wg_ref, bg_ref, gn_ref, convw_ref, qn_ref, wuq_ref,
                 kvn_ref, wukv_ref, cos_ref, sin_ref,
                 og_ref, oconv_ref, mq_ref, mk_ref, mv_ref, mg_ref,
                 ubuf_ref, st_ref, *, tile, attn_tile, q_scale):
    t = pl.program_id(1)

    @pl.when(t == 0)
    def _():
        ubuf_ref[0:8, :] = jnp.zeros((8, CONV_WIDTH), F32)
        st_ref[...] = jnp.zeros(st_ref.shape, F32)

    x = h_ref[0]
    xn = _rms(x, g_ref[...]).astype(BF16)

    def seg(a, b):
        return _dot(xn, win_ref[:, a:b])

    kr = seg(SEG_KR, IN_PAD)
    z = _dot(kr.astype(BF16), wg_ref[...]) + bg_ref[...]
    la = (jnp.minimum(z, 0.0) - jnp.log1p(jnp.exp(-jnp.abs(z)))) * (1.0 / GLA_TAU)
    _gla_tile(seg(SEG_Q, SEG_K), seg(SEG_K, SEG_V), la, seg(SEG_V, SEG_GG),
              _silu(seg(SEG_GG, SEG_CC)), gn_ref[...], st_ref, og_ref)

    u = seg(SEG_CC, SEG_CB) * seg(SEG_CH, SEG_CG)
    ubuf_ref[8:8 + tile, :] = u
    cw = convw_ref[...]
    conv = (cw[0:1, :] * ubuf_ref[6:6 + tile, :] + cw[1:2, :] * ubuf_ref[7:7 + tile, :]
            + cw[2:3, :] * u)
    oconv_ref[0] = (seg(SEG_CB, SEG_CH) * conv * _silu(seg(SEG_CG, SEG_CQ))).astype(BF16)
    ubuf_ref[0:8, :] = u[tile - 8:tile, :]

    cos = cos_ref[0]
    sin = sin_ref[0]
    cqn = _rms(seg(SEG_CQ, SEG_CKV), qn_ref[...]).astype(BF16)
    qn = _dot(cqn, wuq_ref[:, 0:MLA_Q_NOPE_W]) * q_scale
    qr = _dot(cqn, wuq_ref[:, MLA_Q_NOPE_W:])
    x1, x2 = qr[:, 0:MLA_SLOT], qr[:, MLA_SLOT:]
    cq, sq = cos * q_scale, sin * q_scale
    r1 = x1 * cq - x2 * sq
    r2 = x2 * cq + x1 * sq
    pad_rows = jnp.zeros((MLA_SLOT - MLA_NOPE - MLA_ROPE, attn_tile), BF16)
    for i in range(tile // attn_tile):
        rows = slice(i * attn_tile, (i + 1) * attn_tile)
        qnt = qn[rows].T.astype(BF16)
        r1t = r1[rows].T.astype(BF16)
        r2t = r2[rows].T.astype(BF16)
        for h in range(MLA_HEADS):
            base = h * MLA_SLOT
            mq_ref[0, i, base:base + MLA_NOPE, :] = qnt[h * MLA_NOPE:(h + 1) * MLA_NOPE]
            mq_ref[0, i, base + MLA_NOPE:base + MLA_NOPE + ROPE_HALF, :] = (
                r1t[h * ROPE_HALF:(h + 1) * ROPE_HALF])
            mq_ref[0, i, base + MLA_NOPE + ROPE_HALF:base + MLA_NOPE + MLA_ROPE, :] = (
                r2t[h * ROPE_HALF:(h + 1) * ROPE_HALF])
            mq_ref[0, i, base + MLA_NOPE + MLA_ROPE:base + MLA_SLOT, :] = pad_rows

    ckvn = _rms(seg(SEG_CKV, SEG_MG), kvn_ref[...]).astype(BF16)
    kvh = _dot(ckvn, wukv_ref[...])
    lane = lax.broadcasted_iota(jnp.int32, (1, MLA_SLOT), 1)
    first = (lane >= KR_ROPE_OFF) & (lane < KR_ROPE_OFF + ROPE_HALF)
    second = (lane >= KR_ROPE_OFF + ROPE_HALF) & (lane < KR_ROPE_OFF + MLA_ROPE)
    krope = (kr * jnp.where(first | second, cos, 0.0)
             + pltpu.roll(kr, ROPE_HALF, 1) * jnp.where(second, sin, 0.0)
             - pltpu.roll(kr, MLA_SLOT - ROPE_HALF, 1) * jnp.where(first, sin, 0.0))
    for h in range(MLA_HEADS):
        sl = slice(h * MLA_SLOT, (h + 1) * MLA_SLOT)
        mk_ref[0, :, sl] = (kvh[:, sl] + krope).astype(BF16)
    vals = kvh[:, MLA_HEADS * MLA_SLOT:]
    for i in range(tile // attn_tile):
        mv_ref[0, i] = vals[i * attn_tile:(i + 1) * attn_tile].T.astype(BF16)
    mg_ref[0] = _silu(seg(SEG_MG, SEG_KR))


def _proj_call(h, l, p, tables, tile, attn_tile):
    B, S, D = h.shape
    nt = S // tile
    sub = tile // attn_tile
    q_scale = LOG2E / math.sqrt(MLA_NOPE + MLA_ROPE)

    def tok(w):
        return pl.BlockSpec((1, tile, w), lambda b, t: (b, t, 0))

    def lay(shape):
        return pl.BlockSpec((None,) + shape, lambda b, t: (l,) + (0,) * len(shape))

    in_specs = [
        tok(D),
        lay((1, D)),
        lay((D, IN_PAD)),
        lay((MLA_SLOT, GLA_QK_PAD)),
        lay((1, GLA_QK_PAD)),
        lay((1, GLA_WIDTH)),
        lay((CONV_K, CONV_WIDTH)),
        lay((1, MLA_Q_RANK)),
        lay((MLA_Q_RANK, MLA_Q_NOPE_W + 2 * MLA_SLOT)),
        lay((1, MLA_KV_RANK)),
        lay((MLA_KV_RANK, MLA_HEADS * MLA_SLOT + MLA_WIDTH)),
        tok(MLA_SLOT), tok(MLA_SLOT),
    ]

    def tok_out(w, dt):
        return tok(w), jax.ShapeDtypeStruct((B, S, w), dt)

    def tile_t_out(w, dt):
        return (pl.BlockSpec((1, sub, w, attn_tile), lambda b, t: (b, t, 0, 0)),
                jax.ShapeDtypeStruct((B, S // attn_tile, w, attn_tile), dt))

    outs = [
        tok_out(GLA_WIDTH, BF16), tok_out(CONV_WIDTH, BF16),
        tile_t_out(MLA_HEADS * MLA_SLOT, BF16), tok_out(MLA_HEADS * MLA_SLOT, BF16),
        tile_t_out(MLA_WIDTH, BF16), tok_out(MLA_WIDTH, F32),
    ]
    return pl.pallas_call(
        functools.partial(_proj_kernel, tile=tile, attn_tile=attn_tile, q_scale=q_scale),
        grid=(B, nt),
        in_specs=in_specs,
        out_specs=[spec for spec, _ in outs],
        out_shape=[shape for _, shape in outs],
        scratch_shapes=[pltpu.VMEM((tile + 8, CONV_WIDTH), F32),
                        pltpu.VMEM((GLA_WIDTH, GLA_QK_PAD), F32)],
        compiler_params=pltpu.CompilerParams(
            dimension_semantics=("arbitrary", "arbitrary"),
            vmem_limit_bytes=VMEM_LIMIT_BYTES),
        name="proj",
    )(h, p["norm_mix"], p["w_in"], p["wg"], p["bg"], p["gla_norm"], p["conv_w"], p["q_norm"],
      p["w_uq"], p["kv_norm"], p["w_ukv"], *tables)


def _mla_kernel(qt_ref, k_ref, vt_ref, mg_ref, o_ref, m_ref, acc_ref, s_ref, *, tile):
    qi = pl.program_id(1)
    nt = pl.num_programs(1)
    half = tile // 2
    krow = lax.broadcasted_iota(jnp.int32, (half, tile), 0)
    qcol = lax.broadcasted_iota(jnp.int32, (half, tile), 1)
    causal_top = krow <= qcol
    causal_sq = causal_top[:, 0:half]
    ones_rows = jnp.ones((MLA_ACC_ROWS - MLA_V, tile), BF16)

    m_ref[...] = jnp.full(m_ref.shape, MASK_VALUE, F32)
    acc_ref[...] = jnp.zeros(acc_ref.shape, F32)

    def head(e):
        return slice(e * MLA_SLOT, (e + 1) * MLA_SLOT)

    def scores(j, e, q_tile):
        rows = pl.ds(pl.multiple_of(j * tile, tile), tile)
        s_ref[e] = _dot(k_ref[0, rows, head(e)], qt_ref[0, q_tile, head(e), :])

    def scores_diag(e):
        top = pl.ds(pl.multiple_of(qi * tile, tile), half)
        bot = pl.ds(pl.multiple_of(qi * tile + half, half), half)
        s_ref[e, 0:half, :] = _dot(k_ref[0, top, head(e)], qt_ref[0, qi, head(e), :])
        s_ref[e, half:tile, half:tile] = _dot(k_ref[0, bot, head(e)],
                                              qt_ref[0, qi, head(e), half:tile])

    def values(j, e, lo, hi):
        vt = vt_ref[0, j, e * MLA_V:(e + 1) * MLA_V, lo:hi]
        return jnp.concatenate([vt, ones_rows[:, lo:hi]], axis=0)

    def consume(j, e):
        s = s_ref[e]
        m_prev = m_ref[e]
        m_new = jnp.maximum(m_prev, jnp.max(s, axis=0, keepdims=True))
        alpha = jnp.exp2(m_prev - m_new)
        p = jnp.exp2(s - m_new).astype(BF16)
        acc_ref[e] = alpha * acc_ref[e] + _dot(values(j, e, 0, tile), p)
        m_ref[e] = m_new

    def consume_diag(e):
        s = jnp.where(causal_top, s_ref[e, 0:half, :], MASK_VALUE)
        m_prev = m_ref[e]
        m_new = jnp.maximum(m_prev, jnp.max(s, axis=0, keepdims=True))
        p = jnp.exp2(s - m_new).astype(BF16)
        acc = jnp.exp2(m_prev - m_new) * acc_ref[e] + _dot(values(qi, e, 0, half), p)
        s2 = jnp.where(causal_sq, s_ref[e, half:tile, half:tile], MASK_VALUE)
        m2_prev = m_new[:, half:tile]
        m2_new = jnp.maximum(m2_prev, jnp.max(s2, axis=0, keepdims=True))
        p2 = jnp.exp2(s2 - m2_new).astype(BF16)
        acc2 = (jnp.exp2(m2_prev - m2_new) * acc[:, half:tile]
                + _dot(values(qi, e, half, tile), p2))
        acc = jnp.concatenate([acc[:, 0:half], acc2], axis=1)
        return acc[0:MLA_V, :] / acc[MLA_V:MLA_V + 1, :]

    @pl.when(qi == 0)
    def _():
        scores(0, 0, 0)

    def body(j, carry):
        for e in range(MLA_HEADS):
            if e + 1 < MLA_HEADS:
                scores(j, e + 1, qi)
            else:
                scores(j + 1, 0, qi)
            consume(j, e)
        return carry

    lax.fori_loop(0, qi, body, 0)
    outs = []
    for e in range(MLA_HEADS):
        if e + 1 < MLA_HEADS:
            scores_diag(e + 1)
        else:
            scores(0, 0, jnp.minimum(qi + 1, nt - 1))
        outs.append(consume_diag(e))
    ot = jnp.concatenate(outs, axis=0)
    o_ref[0] = (ot.T * mg_ref[0]).astype(BF16)


def _mla_call(mqt, mk, mvt, mg, tile):
    B, nt, _, _ = mqt.shape
    S = nt * tile
    return pl.pallas_call(
        functools.partial(_mla_kernel, tile=tile),
        grid=(B, nt),
        in_specs=[
            pl.BlockSpec((1, nt, MLA_HEADS * MLA_SLOT, tile), lambda b, i: (b, 0, 0, 0)),
            pl.BlockSpec((1, S, MLA_HEADS * MLA_SLOT), lambda b, i: (b, 0, 0)),
            pl.BlockSpec((1, nt, MLA_WIDTH, tile), lambda b, i: (b, 0, 0, 0)),
            pl.BlockSpec((1, tile, MLA_WIDTH), lambda b, i: (b, i, 0)),
        ],
        out_specs=pl.BlockSpec((1, tile, MLA_WIDTH), lambda b, i: (b, i, 0)),
        out_shape=jax.ShapeDtypeStruct((B, S, MLA_WIDTH), BF16),
        scratch_shapes=[pltpu.VMEM((MLA_HEADS, 1, tile), F32),
                        pltpu.VMEM((MLA_HEADS, MLA_ACC_ROWS, tile), F32),
                        pltpu.VMEM((MLA_HEADS, tile, tile), F32)],
        compiler_params=pltpu.CompilerParams(
            dimension_semantics=("arbitrary", "arbitrary"),
            vmem_limit_bytes=VMEM_LIMIT_BYTES),
        name="mla",
    )(mqt, mk, mvt, mg)


def _memkv_kernel(mem_ref, g_ref, wk_ref, wv_ref, k_ref, v_ref):
    memn = _rms(mem_ref[0], g_ref[...]).astype(BF16)
    k_ref[...] = _dot(memn, wk_ref[...].astype(BF16)).astype(BF16)
    v_ref[...] = _dot(memn, wv_ref[...].astype(BF16)).astype(BF16)


def _memkv_call(mem, norm_mem, wk, wv):
    B, M, D = mem.shape
    L = wk.shape[0]
    kv_shape = jax.ShapeDtypeStruct((L, B, M, MEM_INNER), BF16)
    kv_spec = pl.BlockSpec((None, None, M, MEM_INNER), lambda l, b: (l, b, 0, 0))
    return pl.pallas_call(
        _memkv_kernel,
        grid=(L, B),
        in_specs=[pl.BlockSpec((1, M, D), lambda l, b: (b, 0, 0)),
                  pl.BlockSpec((None, 1, D), lambda l, b: (l, 0, 0)),
                  pl.BlockSpec((None, D, MEM_INNER), lambda l, b: (l, 0, 0)),
                  pl.BlockSpec((None, D, MEM_INNER), lambda l, b: (l, 0, 0))],
        out_specs=[kv_spec, kv_spec],
        out_shape=[kv_shape, kv_shape],
        compiler_params=pltpu.CompilerParams(
            dimension_semantics=("arbitrary", "arbitrary"),
            vmem_limit_bytes=VMEM_LIMIT_BYTES),
        name="memkv",
    )(mem, norm_mem, wk, wv)


def _post_kernel(h_ref, og_ref, oc_ref, om_ref, wout_ref, gx_ref, wq_ref, mk_ref, mv_ref,
                 wo_ref, gf_ref, o_ref, wout_bf, wq_bf, wo_bf, *, final_norm):
    @pl.when((pl.program_id(0) == 0) & (pl.program_id(1) == 0))
    def _():
        wout_bf[...] = wout_ref[...].astype(BF16)
        wq_bf[...] = wq_ref[...].astype(BF16)
        wo_bf[...] = wo_ref[...].astype(BF16)

    x = jnp.concatenate([og_ref[0], oc_ref[0], om_ref[0]], axis=1)
    h1 = h_ref[0] + _dot(x, wout_bf[...])
    hn = _rms(h1, gx_ref[...]).astype(BF16)
    q = (_dot(hn, wq_bf[...]) * (LOG2E / math.sqrt(MEM_HEAD_DIM))).astype(BF16)
    ones = jnp.ones((mv_ref.shape[0], MEM_HEAD_DIM), BF16)
    heads = []
    for hd in range(MEM_HEADS):
        sl = slice(hd * MEM_HEAD_DIM, (hd + 1) * MEM_HEAD_DIM)
        s = _dot_nt(q[:, sl], mk_ref[:, sl])
        p = jnp.exp2(s - jnp.max(s, axis=1, keepdims=True)).astype(BF16)
        nd = _dot(p, jnp.concatenate([mv_ref[:, sl], ones], axis=1))
        heads.append((nd[:, :MEM_HEAD_DIM] / nd[:, MEM_HEAD_DIM:]).astype(BF16))
    o = jnp.concatenate(heads, axis=1)
    h2 = h1 + _dot(o, wo_bf[...])
    if final_norm:
        h2 = _rms(h2, gf_ref[...])
    o_ref[0] = h2


def _post_call(h, og, oc, om, memk, memv, l, p, norm_final, tile, final_norm):
    B, S, D = h.shape
    nt = S // tile
    M = memk.shape[2]

    def tok(w):
        return pl.BlockSpec((1, tile, w), lambda b, t: (b, t, 0))

    def lay(shape):
        return pl.BlockSpec((None,) + shape, lambda b, t: (l,) + (0,) * len(shape))

    mem_spec = pl.BlockSpec((None, None, M, MEM_INNER), lambda b, t: (l, b, 0, 0))
    return pl.pallas_call(
        functools.partial(_post_kernel, final_norm=final_norm),
        grid=(B, nt),
        in_specs=[tok(D), tok(GLA_WIDTH), tok(CONV_WIDTH), tok(MLA_WIDTH),
                  lay((D_MIX, D)), lay((1, D)), lay((D, MEM_INNER)), mem_spec, mem_spec,
                  lay((MEM_INNER, D)), pl.BlockSpec((1, D), lambda b, t: (0, 0))],
        out_specs=tok(D),
        out_shape=jax.ShapeDtypeStruct((B, S, D), F32),
        scratch_shapes=[pltpu.VMEM((D_MIX, D), BF16), pltpu.VMEM((D, MEM_INNER), BF16),
                        pltpu.VMEM((MEM_INNER, D), BF16)],
        compiler_params=pltpu.CompilerParams(
            dimension_semantics=("arbitrary", "arbitrary"),
            vmem_limit_bytes=VMEM_LIMIT_BYTES),
        name="post",
    )(h, og, oc, om, p["w_out"], p["norm_xattn"], p["mem_wq"], memk, memv, p["mem_wo"],
      norm_final)


def _pad_cols(w, width):
    return jnp.pad(w, [(0, 0)] * (w.ndim - 1) + [(0, width - w.shape[-1])])


def _pack_w_in_kernel(wt_ref, o_ref):
    cols = wt_ref.shape[2]
    o_gv = 2 * GLA_QK
    o_glr = o_gv + GLA_WIDTH
    o_gg = o_glr + GLA_GATE_RANK
    o_cc = o_gg + GLA_WIDTH
    o_kr = o_cc + 4 * CONV_WIDTH + MLA_Q_RANK + MLA_KV_RANK
    o_mg = o_kr + MLA_ROPE
    zeros = lambda n: jnp.zeros((n, cols), F32)

    def rows(a, b):
        return wt_ref[0, a:b, :]

    pieces = [
        (SEG_Q, [rows(0, GLA_QK), zeros(GLA_QK_PAD - GLA_QK)]),
        (SEG_K, [rows(GLA_QK, o_gv), zeros(GLA_QK_PAD - GLA_QK)]),
        (SEG_V, [rows(o_gv, o_glr)]),
        (SEG_GG, [rows(o_gg, o_cc)]),
        (SEG_CC, [rows(o_cc, o_kr)]),
        (SEG_MG, [rows(o_mg, o_mg + MLA_WIDTH)]),
        (SEG_KR, [zeros(KR_ROPE_OFF), rows(o_kr, o_mg), rows(o_glr, o_gg),
                  zeros(MLA_SLOT - KR_GATE_OFF - GLA_GATE_RANK)]),
    ]
    for off, parts in pieces:
        blk = parts[0] if len(parts) == 1 else jnp.concatenate(parts, axis=0)
        o_ref[0, :, off:off + blk.shape[0]] = blk.T.astype(BF16)


def _pack_w_in(w_in):
    wt = jnp.swapaxes(w_in, 1, 2)
    L, W, D = wt.shape
    cols = 256
    return pl.pallas_call(
        _pack_w_in_kernel,
        grid=(L, D // cols),
        in_specs=[pl.BlockSpec((1, W, cols), lambda l, c: (l, 0, c))],
        out_specs=pl.BlockSpec((1, cols, IN_PAD), lambda l, c: (l, c, 0)),
        out_shape=jax.ShapeDtypeStruct((L, D, IN_PAD), BF16),
        compiler_params=pltpu.CompilerParams(
            dimension_semantics=("arbitrary", "arbitrary"),
            vmem_limit_bytes=VMEM_LIMIT_BYTES),
        name="pack_w_in",
    )(wt)


def _prep_params(norm_mix, w_in, gla_w_gate, gla_b_gate, gla_norm, conv_w, mla_q_norm, mla_w_uq,
                 mla_kv_norm, mla_w_ukv, w_out, norm_xattn, mem_wq, mem_wo):
    L = w_in.shape[0]
    wg = jnp.zeros((L, MLA_SLOT, GLA_QK_PAD), F32)
    wg = wg.at[:, KR_GATE_OFF:KR_GATE_OFF + GLA_GATE_RANK, :GLA_QK].set(gla_w_gate)
    w_uq = mla_w_uq.reshape(L, MLA_Q_RANK, MLA_HEADS, MLA_NOPE + MLA_ROPE)
    half = lambda lo: _pad_cols(
        w_uq[..., lo:lo + ROPE_HALF].reshape(L, MLA_Q_RANK, MLA_HEADS * ROPE_HALF), MLA_SLOT)
    w_uq = jnp.concatenate(
        [w_uq[..., :MLA_NOPE].reshape(L, MLA_Q_RANK, MLA_Q_NOPE_W), half(MLA_NOPE),
         half(MLA_NOPE + ROPE_HALF)], axis=-1)
    w_ukv = mla_w_ukv.reshape(L, MLA_KV_RANK, MLA_HEADS, MLA_NOPE + MLA_V)
    w_uk = _pad_cols(w_ukv[..., :MLA_NOPE], MLA_SLOT).reshape(L, MLA_KV_RANK, MLA_HEADS * MLA_SLOT)
    w_uv = w_ukv[..., MLA_NOPE:].reshape(L, MLA_KV_RANK, MLA_WIDTH)
    return {
        "norm_mix": norm_mix[:, None, :],
        "w_in": _pack_w_in(w_in),
        "wg": wg.astype(BF16),
        "bg": _pad_cols(gla_b_gate, GLA_QK_PAD)[:, None, :],
        "gla_norm": jnp.tile(gla_norm, (1, GLA_HEADS))[:, None, :],
        "conv_w": conv_w,
        "q_norm": mla_q_norm[:, None, :],
        "w_uq": w_uq.astype(BF16),
        "kv_norm": mla_kv_norm[:, None, :],
        "w_ukv": jnp.concatenate([w_uk, w_uv], axis=-1).astype(BF16),
        "w_out": w_out,
        "norm_xattn": norm_xattn[:, None, :],
        "mem_wq": mem_wq,
        "mem_wo": mem_wo,
    }


def _rope_tables(positions):
    inv_freq = 1.0 / (ROPE_BASE ** (jnp.arange(0, MLA_ROPE, 2, dtype=F32) / MLA_ROPE))
    lane_freq = jnp.tile(inv_freq, MLA_SLOT // ROPE_HALF)
    ang = positions.astype(F32)[..., None] * lane_freq
    return jnp.cos(ang), jnp.sin(ang)


def _pick_tile(S, want):
    t = min(S, want)
    assert S % t == 0 and t % GLA_CHUNK == 0
    return t


def kernel(x, mem, positions, norm_mix, w_in, gla_w_gate, gla_b_gate, gla_norm, conv_w, mla_q_norm,
           mla_w_uq, mla_kv_norm, mla_w_ukv, w_out, norm_xattn, norm_mem, mem_wq, mem_wk, mem_wv,
           mem_wo, norm_final):
    depth = w_in.shape[0]
    S = x.shape[1]
    attn_tile = _pick_tile(S, 512)
    tile = _pick_tile(S, PROJ_TILE)
    p = _prep_params(norm_mix, w_in, gla_w_gate, gla_b_gate, gla_norm, conv_w, mla_q_norm,
                     mla_w_uq, mla_kv_norm, mla_w_ukv, w_out, norm_xattn, mem_wq, mem_wo)
    tables = _rope_tables(positions)
    memk, memv = _memkv_call(mem, norm_mem[:, None, :], mem_wk, mem_wv)
    nf = norm_final[None, :]
    h = x
    for l in range(depth):
        og, oc, mq, mk, mv, mg = _proj_call(h, l, p, tables, tile, attn_tile)
        om = _mla_call(mq, mk, mv, mg, attn_tile)
        h = _post_call(h, og, oc, om, memk, memv, l, p, nf, tile, l == depth - 1)
    return h
```

```python
import functools
import math

import jax
import jax.numpy as jnp
from jax import lax
from jax.experimental import pallas as pl
from jax.experimental.pallas import tpu as pltpu

F32 = jnp.float32
BF16 = jnp.bfloat16

D_MODEL = 1024
EPS = 1e-6

GLA_HEADS = 4
GLA_DV = 96
GLA_DK = 48
GLA_QK = GLA_HEADS * GLA_DK
GLA_WIDTH = GLA_HEADS * GLA_DV
GLA_GATE_RANK = 16
GLA_TAU = 16.0
GLA_CHUNK = 64
GLA_QK_PAD = 256

CONV_WIDTH = 256
CONV_K = 3

MLA_HEADS = 6
MLA_NOPE = 64
MLA_ROPE = 32
MLA_V = 64
MLA_Q_RANK = 256
MLA_KV_RANK = 256
MLA_WIDTH = MLA_HEADS * MLA_V
MLA_SLOT = 128
MLA_Q_NOPE_W = MLA_HEADS * MLA_NOPE
MLA_ACC_ROWS = MLA_V + 16
MLA_AHEAD = 2
ROPE_BASE = 10000.0
ROPE_HALF = MLA_ROPE // 2

D_MIX = GLA_WIDTH + CONV_WIDTH + MLA_WIDTH

MEM_HEADS = 4
MEM_HEAD_DIM = 128
MEM_INNER = MEM_HEADS * MEM_HEAD_DIM

LOG2E = math.log2(math.e)
MASK_VALUE = -1e30

SEG_Q = 0
SEG_K = SEG_Q + GLA_QK_PAD
SEG_V = SEG_K + GLA_QK_PAD
SEG_GG = SEG_V + GLA_WIDTH
SEG_CC = SEG_GG + GLA_WIDTH
SEG_CB = SEG_CC + CONV_WIDTH
SEG_CH = SEG_CB + CONV_WIDTH
SEG_CG = SEG_CH + CONV_WIDTH
SEG_CQ = SEG_CG + CONV_WIDTH
SEG_CKV = SEG_CQ + MLA_Q_RANK
SEG_MG = SEG_CKV + MLA_KV_RANK
SEG_KR = SEG_MG + MLA_WIDTH
IN_PAD = SEG_KR + MLA_SLOT
KR_ROPE_OFF = MLA_NOPE
KR_GATE_OFF = MLA_NOPE + MLA_ROPE

VMEM_LIMIT_BYTES = 56 * 1024 * 1024
PROJ_TILE = 1024
PIECE_N = 256
GLA_FILL_STAGES = 4


def _silu(x):
    return x * (1.0 / (1.0 + jnp.exp(-x)))


def _rms(x, g):
    return x * lax.rsqrt(jnp.mean(x * x, axis=-1, keepdims=True) + EPS) * g


def _dot(a, b):
    return jnp.dot(a, b, preferred_element_type=F32)


def _dot_nt(a, b):
    return lax.dot_general(a, b, (((1,), (1,)), ((), ())), preferred_element_type=F32)


def _dot_tn(a, b):
    return lax.dot_general(a, b, (((0,), (0,)), ((), ())), preferred_element_type=F32)


def _gla_tile(q, k, la, v, gate, gn, st_ref, og_ref, fill):
    C = GLA_CHUNK
    tile = q.shape[0]
    lane_k = lax.broadcasted_iota(jnp.int32, (1, GLA_QK_PAD), 1)
    lane_v = lax.broadcasted_iota(jnp.int32, (1, GLA_WIDTH), 1)
    kmask = [(lane_k >= h * GLA_DK) & (lane_k < (h + 1) * GLA_DK) for h in range(GLA_HEADS)]
    vmask = [(lane_v >= h * GLA_DV) & (lane_v < (h + 1) * GLA_DV) for h in range(GLA_HEADS)]
    row = lax.broadcasted_iota(jnp.int32, (C, 2 * C), 0)
    col = lax.broadcasted_iota(jnp.int32, (C, 2 * C), 1)
    cum2 = (row >= (col & (C - 1))).astype(BF16)
    hrow = lax.broadcasted_iota(jnp.int32, (C, GLA_HEADS * C), 0)
    hcol = lax.broadcasted_iota(jnp.int32, (C, GLA_HEADS * C), 1)
    tril_heads = hrow >= (hcol & (C - 1))
    srow = lax.broadcasted_iota(jnp.int32, (GLA_WIDTH, GLA_QK_PAD), 0)
    slane = lax.broadcasted_iota(jnp.int32, (GLA_WIDTH, GLA_QK_PAD), 1)
    same_head = None
    for h in range(GLA_HEADS):
        blk = ((srow >= h * GLA_DV) & (srow < (h + 1) * GLA_DV)
               & (slane >= h * GLA_DK) & (slane < (h + 1) * GLA_DK))
        same_head = blk if same_head is None else (same_head | blk)

    st = st_ref[...]
    for c in range(tile // C):
        sl = slice(c * C, (c + 1) * C)
        qc, kc, lac, vc = q[sl], k[sl], la[sl], v[sl]
        la_hi = lac.astype(BF16)
        la_lo = (lac - la_hi.astype(F32)).astype(BF16)
        b = _dot(cum2, jnp.concatenate([la_hi, la_lo], axis=0))
        fill()
        b_last = b[C - 1:C, :]
        q_dec = (qc * (GLA_DK ** -0.5) * jnp.exp(b)).astype(BF16)
        k_inv = kc * jnp.exp(-b)
        k_end = (kc * jnp.exp(b_last - b)).astype(BF16)
        decay = jnp.exp(b_last)
        k_heads = jnp.concatenate(
            [jnp.where(kmask[h], k_inv, 0.0) for h in range(GLA_HEADS)], axis=0).astype(BF16)
        a = _dot_nt(q_dec, k_heads)
        fill()
        a = jnp.where(tril_heads, a, 0.0).astype(BF16)
        v_heads = jnp.concatenate(
            [jnp.where(vmask[h], vc, 0.0) for h in range(GLA_HEADS)], axis=0).astype(BF16)
        o = _dot(a, v_heads) + _dot_nt(q_dec, st.astype(BF16))
        fill()
        upd = _dot_tn(vc.astype(BF16), k_end)
        fill()
        st = st * decay + jnp.where(same_head, upd, 0.0)
        o2 = o * o
        ms = jnp.zeros_like(o)
        for h in range(GLA_HEADS):
            mh = jnp.sum(jnp.where(vmask[h], o2, 0.0), axis=1, keepdims=True) * (1.0 / GLA_DV)
            ms = jnp.where(vmask[h], mh, ms)
        y = o * lax.rsqrt(ms + EPS) * gn
        og_ref[0, sl, :] = (y * gate[sl]).astype(BF16)
    st_ref[...] = st


def _proj_kernel(h_ref, g_ref, win_ref, wg_ref, bg_ref, gn_ref, convw_ref, qn_ref, wuq_ref,
                 kvn_ref, wukv_ref, cos_ref, sin_ref,
                 og_ref, oconv_ref, mq_ref, mk_ref, mv_ref, mg_ref,
                 ubuf_ref, st_ref, *, tile, attn_tile, q_scale):
    t = pl.program_id(1)

    @pl.when(t == 0)
    def _():
        ubuf_ref[0:8, :] = jnp.zeros((8, CONV_WIDTH), F32)
        st_ref[...] = jnp.zeros(st_ref.shape, F32)

    x = h_ref[0]
    xn = _rms(x, g_ref[...]).astype(BF16)

    def seg(a, b):
        return _dot(xn, win_ref[:, a:b])

    val = {}
    work = []

    def queue_segment(name, a, b):
        parts = []
        for n0 in range(a, b, PIECE_N):
            work.append((4.0, lambda n0=n0: parts.append(seg(n0, min(n0 + PIECE_N, b)))))
        work.append((0.0, lambda: val.update({name: jnp.concatenate(parts, axis=1)})))

    def q_up():
        cqn = _rms(val["cq"], qn_ref[...]).astype(BF16)
        val["qn"] = _dot(cqn, wuq_ref[:, 0:MLA_Q_NOPE_W]) * q_scale
        val["qr"] = _dot(cqn, wuq_ref[:, MLA_Q_NOPE_W:])

    kv_parts = []
    kv_width = MLA_HEADS * MLA_SLOT + MLA_WIDTH

    def kv_piece(n0):
        if n0 == 0:
            val["ckvn"] = _rms(val["ckv"], kvn_ref[...]).astype(BF16)
        kv_parts.append(_dot(val["ckvn"], wukv_ref[:, n0:min(n0 + PIECE_N, kv_width)]))

    queue_segment("cq", SEG_CQ, SEG_CKV)
    queue_segment("ckv", SEG_CKV, SEG_MG)
    work.append((2.5, q_up))
    for n0 in range(0, kv_width, PIECE_N):
        work.append((1.0, functools.partial(kv_piece, n0)))
    queue_segment("cc", SEG_CC, SEG_CB)
    queue_segment("ch", SEG_CH, SEG_CG)
    queue_segment("cb", SEG_CB, SEG_CH)
    queue_segment("cg", SEG_CG, SEG_CQ)
    queue_segment("mg", SEG_MG, SEG_KR)
    work.reverse()
    total_cost = sum(c for c, _ in work)
    slots = GLA_FILL_STAGES * (tile // GLA_CHUNK)
    progress = {"slot": 0, "cost": 0.0}

    def fill():
        progress["slot"] += 1
        target = total_cost * progress["slot"] / slots
        while work and progress["cost"] + 0.5 * work[-1][0] <= target:
            cost, thunk = work.pop()
            progress["cost"] += cost
            thunk()

    kr = seg(SEG_KR, IN_PAD)
    gq, gk, gv, gg = seg(SEG_Q, SEG_K), seg(SEG_K, SEG_V), seg(SEG_V, SEG_GG), seg(SEG_GG, SEG_CC)
    z = _dot(kr.astype(BF16), wg_ref[...]) + bg_ref[...]
    la = (jnp.minimum(z, 0.0) - jnp.log1p(jnp.exp(-jnp.abs(z)))) * (1.0 / GLA_TAU)
    _gla_tile(gq, gk, la, gv, _silu(gg), gn_ref[...], st_ref, og_ref, fill)
    while work:
        work.pop()[1]()

    u = val["cc"] * val["ch"]
    ubuf_ref[8:8 + tile, :] = u
    cw = convw_ref[...]
    conv = (cw[0:1, :] * ubuf_ref[6:6 + tile, :] + cw[1:2, :] * ubuf_ref[7:7 + tile, :]
            + cw[2:3, :] * u)
    oconv_ref[0] = (val["cb"] * conv * _silu(val["cg"])).astype(BF16)
    ubuf_ref[0:8, :] = u[tile - 8:tile, :]

    cos = cos_ref[0]
    sin = sin_ref[0]
    qn, qr = val["qn"], val["qr"]
    x1, x2 = qr[:, 0:MLA_SLOT], qr[:, MLA_SLOT:]
    cq, sq = cos * q_scale, sin * q_scale
    r1 = x1 * cq - x2 * sq
    r2 = x2 * cq + x1 * sq
    pad_rows = jnp.zeros((MLA_SLOT - MLA_NOPE - MLA_ROPE, attn_tile), BF16)
    for i in range(tile // attn_tile):
        rows = slice(i * attn_tile, (i + 1) * attn_tile)
        qnt = qn[rows].T.astype(BF16)
        r1t = r1[rows].T.astype(BF16)
        r2t = r2[rows].T.astype(BF16)
        for h in range(MLA_HEADS):
            base = h * MLA_SLOT
            mq_ref[0, i, base:base + MLA_NOPE, :] = qnt[h * MLA_NOPE:(h + 1) * MLA_NOPE]
            mq_ref[0, i, base + MLA_NOPE:base + MLA_NOPE + ROPE_HALF, :] = (
                r1t[h * ROPE_HALF:(h + 1) * ROPE_HALF])
            mq_ref[0, i, base + MLA_NOPE + ROPE_HALF:base + MLA_NOPE + MLA_ROPE, :] = (
                r2t[h * ROPE_HALF:(h + 1) * ROPE_HALF])
            mq_ref[0, i, base + MLA_NOPE + MLA_ROPE:base + MLA_SLOT, :] = pad_rows

    kvh = jnp.concatenate(kv_parts, axis=1)
    lane = lax.broadcasted_iota(jnp.int32, (1, MLA_SLOT), 1)
    first = (lane >= KR_ROPE_OFF) & (lane < KR_ROPE_OFF + ROPE_HALF)
    second = (lane >= KR_ROPE_OFF + ROPE_HALF) & (lane < KR_ROPE_OFF + MLA_ROPE)
    krope = (kr * jnp.where(first | second, cos, 0.0)
             + pltpu.roll(kr, ROPE_HALF, 1) * jnp.where(second, sin, 0.0)
             - pltpu.roll(kr, MLA_SLOT - ROPE_HALF, 1) * jnp.where(first, sin, 0.0))
    for h in range(MLA_HEADS):
        sl = slice(h * MLA_SLOT, (h + 1) * MLA_SLOT)
        mk_ref[0, :, sl] = (kvh[:, sl] + krope).astype(BF16)
    vals = kvh[:, MLA_HEADS * MLA_SLOT:]
    for i in range(tile // attn_tile):
        mv_ref[0, i] = vals[i * attn_tile:(i + 1) * attn_tile].T.astype(BF16)
    mg_ref[0] = _silu(val["mg"])


def _proj_call(h, l, p, tables, tile, attn_tile):
    B, S, D = h.shape
    nt = S // tile
    sub = tile // attn_tile
    q_scale = LOG2E / math.sqrt(MLA_NOPE + MLA_ROPE)

    def tok(w):
        return pl.BlockSpec((1, tile, w), lambda b, t: (b, t, 0))

    def lay(shape):
        return pl.BlockSpec((None,) + shape, lambda b, t: (l,) + (0,) * len(shape))

    in_specs = [
        tok(D),
        lay((1, D)),
        lay((D, IN_PAD)),
        lay((MLA_SLOT, GLA_QK_PAD)),
        lay((1, GLA_QK_PAD)),
        lay((1, GLA_WIDTH)),
        lay((CONV_K, CONV_WIDTH)),
        lay((1, MLA_Q_RANK)),
        lay((MLA_Q_RANK, MLA_Q_NOPE_W + 2 * MLA_SLOT)),
        lay((1, MLA_KV_RANK)),
        lay((MLA_KV_RANK, MLA_HEADS * MLA_SLOT + MLA_WIDTH)),
        tok(MLA_SLOT), tok(MLA_SLOT),
    ]

    def tok_out(w, dt):
        return tok(w), jax.ShapeDtypeStruct((B, S, w), dt)

    def tile_t_out(w, dt):
        return (pl.BlockSpec((1, sub, w, attn_tile), lambda b, t: (b, t, 0, 0)),
                jax.ShapeDtypeStruct((B, S // attn_tile, w, attn_tile), dt))

    outs = [
        tok_out(GLA_WIDTH, BF16), tok_out(CONV_WIDTH, BF16),
        tile_t_out(MLA_HEADS * MLA_SLOT, BF16), tok_out(MLA_HEADS * MLA_SLOT, BF16),
        tile_t_out(MLA_WIDTH, BF16), tok_out(MLA_WIDTH, F32),
    ]
    return pl.pallas_call(
        functools.partial(_proj_kernel, tile=tile, attn_tile=attn_tile, q_scale=q_scale),
        grid=(B, nt),
        in_specs=in_specs,
        out_specs=[spec for spec, _ in outs],
        out_shape=[shape for _, shape in outs],
        scratch_shapes=[pltpu.VMEM((tile + 8, CONV_WIDTH), F32),
                        pltpu.VMEM((GLA_WIDTH, GLA_QK_PAD), F32)],
        compiler_params=pltpu.CompilerParams(
            dimension_semantics=("arbitrary", "arbitrary"),
            vmem_limit_bytes=VMEM_LIMIT_BYTES),
        name="proj",
    )(h, p["norm_mix"], p["w_in"], p["wg"], p["bg"], p["gla_norm"], p["conv_w"], p["q_norm"],
      p["w_uq"], p["kv_norm"], p["w_ukv"], *tables)


def _mla_kernel(qt_ref, k_ref, vt_ref, mg_ref, o_ref, m_ref, acc_ref, s_ref, *, tile):
    qi = pl.program_id(1)
    nt = pl.num_programs(1)
    half = tile // 2
    krow = lax.broadcasted_iota(jnp.int32, (half, tile), 0)
    qcol = lax.broadcasted_iota(jnp.int32, (half, tile), 1)
    causal_top = krow <= qcol
    causal_sq = causal_top[:, 0:half]
    ones_rows = jnp.ones((MLA_ACC_ROWS - MLA_V, tile), BF16)

    m_ref[...] = jnp.full(m_ref.shape, MASK_VALUE, F32)
    acc_ref[...] = jnp.zeros(acc_ref.shape, F32)

    def head(e):
        return slice(e * MLA_SLOT, (e + 1) * MLA_SLOT)

    def scores(j, e, q_tile):
        rows = pl.ds(pl.multiple_of(j * tile, tile), tile)
        s_ref[e] = _dot(k_ref[0, rows, head(e)], qt_ref[0, q_tile, head(e), :])

    def scores_diag(e):
        top = pl.ds(pl.multiple_of(qi * tile, tile), half)
        bot = pl.ds(pl.multiple_of(qi * tile + half, half), half)
        s_ref[e, 0:half, :] = _dot(k_ref[0, top, head(e)], qt_ref[0, qi, head(e), :])
        s_ref[e, half:tile, half:tile] = _dot(k_ref[0, bot, head(e)],
                                              qt_ref[0, qi, head(e), half:tile])

    def values(j, e, lo, hi):
        vt = vt_ref[0, j, e * MLA_V:(e + 1) * MLA_V, lo:hi]
        return jnp.concatenate([vt, ones_rows[:, lo:hi]], axis=0)

    def consume(j, e):
        s = s_ref[e]
        m_prev = m_ref[e]
        m_new = jnp.maximum(m_prev, jnp.max(s, axis=0, keepdims=True))
        alpha = jnp.exp2(m_prev - m_new)
        p = jnp.exp2(s - m_new).astype(BF16)
        acc_ref[e] = alpha * acc_ref[e] + _dot(values(j, e, 0, tile), p)
        m_ref[e] = m_new

    def consume_diag(e):
        s = jnp.where(causal_top, s_ref[e, 0:half, :], MASK_VALUE)
        m_prev = m_ref[e]
        m_new = jnp.maximum(m_prev, jnp.max(s, axis=0, keepdims=True))
        p = jnp.exp2(s - m_new).astype(BF16)
        acc = jnp.exp2(m_prev - m_new) * acc_ref[e] + _dot(values(qi, e, 0, half), p)
        s2 = jnp.where(causal_sq, s_ref[e, half:tile, half:tile], MASK_VALUE)
        m2_prev = m_new[:, half:tile]
        m2_new = jnp.maximum(m2_prev, jnp.max(s2, axis=0, keepdims=True))
        p2 = jnp.exp2(s2 - m2_new).astype(BF16)
        acc2 = (jnp.exp2(m2_prev - m2_new) * acc[:, half:tile]
                + _dot(values(qi, e, half, tile), p2))
        acc = jnp.concatenate([acc[:, 0:half], acc2], axis=1)
        return acc[0:MLA_V, :] / acc[MLA_V:MLA_V + 1, :]

    nxt = jnp.minimum(qi + 1, nt - 1)

    @pl.when(qi == 0)
    def _():
        for e in range(MLA_AHEAD):
            scores(0, e, 0)

    def body(j, carry):
        for e in range(MLA_HEADS):
            ahead = e + MLA_AHEAD
            if ahead < MLA_HEADS:
                scores(j, ahead, qi)
            else:
                scores(j + 1, ahead - MLA_HEADS, qi)
            consume(j, e)
        return carry

    lax.fori_loop(0, qi, body, 0)
    outs = []
    for e in range(MLA_HEADS):
        ahead = e + MLA_AHEAD
        if ahead < MLA_HEADS:
            scores_diag(ahead)
        else:
            scores(0, ahead - MLA_HEADS, nxt)
        outs.append(consume_diag(e))
    ot = jnp.concatenate(outs, axis=0)
    o_ref[0] = (ot.T * mg_ref[0]).astype(BF16)


def _mla_call(mqt, mk, mvt, mg, tile):
    B, nt, _, _ = mqt.shape
    S = nt * tile
    return pl.pallas_call(
        functools.partial(_mla_kernel, tile=tile),
        grid=(B, nt),
        in_specs=[
            pl.BlockSpec((1, nt, MLA_HEADS * MLA_SLOT, tile), lambda b, i: (b, 0, 0, 0)),
            pl.BlockSpec((1, S, MLA_HEADS * MLA_SLOT), lambda b, i: (b, 0, 0)),
            pl.BlockSpec((1, nt, MLA_WIDTH, tile), lambda b, i: (b, 0, 0, 0)),
            pl.BlockSpec((1, tile, MLA_WIDTH), lambda b, i: (b, i, 0)),
        ],
        out_specs=pl.BlockSpec((1, tile, MLA_WIDTH), lambda b, i: (b, i, 0)),
        out_shape=jax.ShapeDtypeStruct((B, S, MLA_WIDTH), BF16),
        scratch_shapes=[pltpu.VMEM((MLA_HEADS, 1, tile), F32),
                        pltpu.VMEM((MLA_HEADS, MLA_ACC_ROWS, tile), F32),
                        pltpu.VMEM((MLA_HEADS, tile, tile), F32)],
        compiler_params=pltpu.CompilerParams(
            dimension_semantics=("arbitrary", "arbitrary"),
            vmem_limit_bytes=VMEM_LIMIT_BYTES),
        name="mla",
    )(mqt, mk, mvt, mg)


def _memkv_kernel(mem_ref, g_ref, wk_ref, wv_ref, k_ref, v_ref):
    memn = _rms(mem_ref[0], g_ref[...]).astype(BF16)
    k_ref[...] = _dot(memn, wk_ref[...].astype(BF16)).astype(BF16)
    v_ref[...] = _dot(memn, wv_ref[...].astype(BF16)).astype(BF16)


def _memkv_call(mem, norm_mem, wk, wv):
    B, M, D = mem.shape
    L = wk.shape[0]
    kv_shape = jax.ShapeDtypeStruct((L, B, M, MEM_INNER), BF16)
    kv_spec = pl.BlockSpec((None, None, M, MEM_INNER), lambda l, b: (l, b, 0, 0))
    return pl.pallas_call(
        _memkv_kernel,
        grid=(L, B),
        in_specs=[pl.BlockSpec((1, M, D), lambda l, b: (b, 0, 0)),
                  pl.BlockSpec((None, 1, D), lambda l, b: (l, 0, 0)),
                  pl.BlockSpec((None, D, MEM_INNER), lambda l, b: (l, 0, 0)),
                  pl.BlockSpec((None, D, MEM_INNER), lambda l, b: (l, 0, 0))],
        out_specs=[kv_spec, kv_spec],
        out_shape=[kv_shape, kv_shape],
        compiler_params=pltpu.CompilerParams(
            dimension_semantics=("arbitrary", "arbitrary"),
            vmem_limit_bytes=VMEM_LIMIT_BYTES),
        name="memkv",
    )(mem, norm_mem, wk, wv)


def _post_kernel(h_ref, og_ref, oc_ref, om_ref, wout_ref, gx_ref, wq_ref, mk_ref, mv_ref,
                 wo_ref, gf_ref, o_ref, wout_bf, wq_bf, wo_bf, *, final_norm):
    @pl.when((pl.program_id(0) == 0) & (pl.program_id(1) == 0))
    def _():
        wout_bf[...] = wout_ref[...].astype(BF16)
        wq_bf[...] = wq_ref[...].astype(BF16)
        wo_bf[...] = wo_ref[...].astype(BF16)

    x = jnp.concatenate([og_ref[0], oc_ref[0], om_ref[0]], axis=1)
    h1 = h_ref[0] + _dot(x, wout_bf[...])
    hn = _rms(h1, gx_ref[...]).astype(BF16)
    q = (_dot(hn, wq_bf[...]) * (LOG2E / math.sqrt(MEM_HEAD_DIM))).astype(BF16)
    ones = jnp.ones((mv_ref.shape[0], MEM_HEAD_DIM), BF16)
    heads = []
    for hd in range(MEM_HEADS):
        sl = slice(hd * MEM_HEAD_DIM, (hd + 1) * MEM_HEAD_DIM)
        s = _dot_nt(q[:, sl], mk_ref[:, sl])
        p = jnp.exp2(s - jnp.max(s, axis=1, keepdims=True)).astype(BF16)
        nd = _dot(p, jnp.concatenate([mv_ref[:, sl], ones], axis=1))
        heads.append((nd[:, :MEM_HEAD_DIM] / nd[:, MEM_HEAD_DIM:]).astype(BF16))
    o = jnp.concatenate(heads, axis=1)
    h2 = h1 + _dot(o, wo_bf[...])
    if final_norm:
        h2 = _rms(h2, gf_ref[...])
    o_ref[0] = h2


def _post_call(h, og, oc, om, memk, memv, l, p, norm_final, tile, final_norm):
    B, S, D = h.shape
    nt = S // tile
    M = memk.shape[2]

    def tok(w):
        return pl.BlockSpec((1, tile, w), lambda b, t: (b, t, 0))

    def lay(shape):
        return pl.BlockSpec((None,) + shape, lambda b, t: (l,) + (0,) * len(shape))

    mem_spec = pl.BlockSpec((None, None, M, MEM_INNER), lambda b, t: (l, b, 0, 0))
    return pl.pallas_call(
        functools.partial(_post_kernel, final_norm=final_norm),
        grid=(B, nt),
        in_specs=[tok(D), tok(GLA_WIDTH), tok(CONV_WIDTH), tok(MLA_WIDTH),
                  lay((D_MIX, D)), lay((1, D)), lay((D, MEM_INNER)), mem_spec, mem_spec,
                  lay((MEM_INNER, D)), pl.BlockSpec((1, D), lambda b, t: (0, 0))],
        out_specs=tok(D),
        out_shape=jax.ShapeDtypeStruct((B, S, D), F32),
        scratch_shapes=[pltpu.VMEM((D_MIX, D), BF16), pltpu.VMEM((D, MEM_INNER), BF16),
                        pltpu.VMEM((MEM_INNER, D), BF16)],
        compiler_params=pltpu.CompilerParams(
            dimension_semantics=("arbitrary", "arbitrary"),
            vmem_limit_bytes=VMEM_LIMIT_BYTES),
        name="post",
    )(h, og, oc, om, p["w_out"], p["norm_xattn"], p["mem_wq"], memk, memv, p["mem_wo"],
      norm_final)


def _pad_cols(w, width):
    return jnp.pad(w, [(0, 0)] * (w.ndim - 1) + [(0, width - w.shape[-1])])


def _pack_w_in_kernel(wt_ref, o_ref):
    cols = wt_ref.shape[2]
    o_gv = 2 * GLA_QK
    o_glr = o_gv + GLA_WIDTH
    o_gg = o_glr + GLA_GATE_RANK
    o_cc = o_gg + GLA_WIDTH
    o_kr = o_cc + 4 * CONV_WIDTH + MLA_Q_RANK + MLA_KV_RANK
    o_mg = o_kr + MLA_ROPE
    zeros = lambda n: jnp.zeros((n, cols), F32)

    def rows(a, b):
        return wt_ref[0, a:b, :]

    pieces = [
        (SEG_Q, [rows(0, GLA_QK), zeros(GLA_QK_PAD - GLA_QK)]),
        (SEG_K, [rows(GLA_QK, o_gv), zeros(GLA_QK_PAD - GLA_QK)]),
        (SEG_V, [rows(o_gv, o_glr)]),
        (SEG_GG, [rows(o_gg, o_cc)]),
        (SEG_CC, [rows(o_cc, o_kr)]),
        (SEG_MG, [rows(o_mg, o_mg + MLA_WIDTH)]),
        (SEG_KR, [zeros(KR_ROPE_OFF), rows(o_kr, o_mg), rows(o_glr, o_gg),
                  zeros(MLA_SLOT - KR_GATE_OFF - GLA_GATE_RANK)]),
    ]
    for off, parts in pieces:
        blk = parts[0] if len(parts) == 1 else jnp.concatenate(parts, axis=0)
        o_ref[0, :, off:off + blk.shape[0]] = blk.T.astype(BF16)


def _pack_w_in(w_in):
    wt = jnp.swapaxes(w_in, 1, 2)
    L, W, D = wt.shape
    cols = 256
    return pl.pallas_call(
        _pack_w_in_kernel,
        grid=(L, D // cols),
        in_specs=[pl.BlockSpec((1, W, cols), lambda l, c: (l, 0, c))],
        out_specs=pl.BlockSpec((1, cols, IN_PAD), lambda l, c: (l, c, 0)),
        out_shape=jax.ShapeDtypeStruct((L, D, IN_PAD), BF16),
        compiler_params=pltpu.CompilerParams(
            dimension_semantics=("arbitrary", "arbitrary"),
            vmem_limit_bytes=VMEM_LIMIT_BYTES),
        name="pack_w_in",
    )(wt)


def _prep_params(norm_mix, w_in, gla_w_gate, gla_b_gate, gla_norm, conv_w, mla_q_norm, mla_w_uq,
                 mla_kv_norm, mla_w_ukv, w_out, norm_xattn, mem_wq, mem_wo):
    L = w_in.shape[0]
    wg = jnp.zeros((L, MLA_SLOT, GLA_QK_PAD), F32)
    wg = wg.at[:, KR_GATE_OFF:KR_GATE_OFF + GLA_GATE_RANK, :GLA_QK].set(gla_w_gate)
    w_uq = mla_w_uq.reshape(L, MLA_Q_RANK, MLA_HEADS, MLA_NOPE + MLA_ROPE)
    half = lambda lo: _pad_cols(
        w_uq[..., lo:lo + ROPE_HALF].reshape(L, MLA_Q_RANK, MLA_HEADS * ROPE_HALF), MLA_SLOT)
    w_uq = jnp.concatenate(
        [w_uq[..., :MLA_NOPE].reshape(L, MLA_Q_RANK, MLA_Q_NOPE_W), half(MLA_NOPE),
         half(MLA_NOPE + ROPE_HALF)], axis=-1)
    w_ukv = mla_w_ukv.reshape(L, MLA_KV_RANK, MLA_HEADS, MLA_NOPE + MLA_V)
    w_uk = _pad_cols(w_ukv[..., :MLA_NOPE], MLA_SLOT).reshape(L, MLA_KV_RANK, MLA_HEADS * MLA_SLOT)
    w_uv = w_ukv[..., MLA_NOPE:].reshape(L, MLA_KV_RANK, MLA_WIDTH)
    return {
        "norm_mix": norm_mix[:, None, :],
        "w_in": _pack_w_in(w_in),
        "wg": wg.astype(BF16),
        "bg": _pad_cols(gla_b_gate, GLA_QK_PAD)[:, None, :],
        "gla_norm": jnp.tile(gla_norm, (1, GLA_HEADS))[:, None, :],
        "conv_w": conv_w,
        "q_norm": mla_q_norm[:, None, :],
        "w_uq": w_uq.astype(BF16),
        "kv_norm": mla_kv_norm[:, None, :],
        "w_ukv": jnp.concatenate([w_uk, w_uv], axis=-1).astype(BF16),
        "w_out": w_out,
        "norm_xattn": norm_xattn[:, None, :],
        "mem_wq": mem_wq,
        "mem_wo": mem_wo,
    }


def _rope_tables(positions):
    inv_freq = 1.0 / (ROPE_BASE ** (jnp.arange(0, MLA_ROPE, 2, dtype=F32) / MLA_ROPE))
    lane_freq = jnp.tile(inv_freq, MLA_SLOT // ROPE_HALF)
    ang = positions.astype(F32)[..., None] * lane_freq
    return jnp.cos(ang), jnp.sin(ang)


def _pick_tile(S, want):
    t = min(S, want)
    assert S % t == 0 and t % GLA_CHUNK == 0
    return t


def kernel(x, mem, positions, norm_mix, w_in, gla_w_gate, gla_b_gate, gla_norm, conv_w, mla_q_norm,
           mla_w_uq, mla_kv_norm, mla_w_ukv, w_out, norm_xattn, norm_mem, mem_wq, mem_wk, mem_wv,
           mem_wo, norm_final):
    depth = w_in.shape[0]
    S = x.shape[1]
    attn_tile = _pick_tile(S, 512)
    tile = _pick_tile(S, PROJ_TILE)
    p = _prep_params(norm_mix, w_in, gla_w_gate, gla_b_gate, gla_norm, conv_w, mla_q_norm,
                     mla_w_uq, mla_kv_norm, mla_w_ukv, w_out, norm_xattn, mem_wq, mem_wo)
    tables = _rope_tables(positions)
    memk, memv = _memkv_call(mem, norm_mem[:, None, :], mem_wk, mem_wv)
    nf = norm_final[None, :]
    h = x
    for l in range(depth):
        og, oc, mq, mk, mv, mg = _proj_call(h, l, p, tables, tile, attn_tile)
        om = _mla_call(mq, mk, mv, mg, attn_tile)
        h = _post_call(h, og, oc, om, memk, memv, l, p, nf, tile, l == depth - 1)
    return h
```

```python
import functools
import math

import jax
import jax.numpy as jnp
from jax import lax
from jax.experimental import pallas as pl
from jax.experimental.pallas import tpu as pltpu

F32 = jnp.float32
BF16 = jnp.bfloat16

D_MODEL = 1024
EPS = 1e-6

GLA_HEADS = 4
GLA_DV = 96
GLA_DK = 48
GLA_QK = GLA_HEADS * GLA_DK
GLA_WIDTH = GLA_HEADS * GLA_DV
GLA_GATE_RANK = 16
GLA_TAU = 16.0
GLA_CHUNK = 64
GLA_QK_PAD = 256

CONV_WIDTH = 256
CONV_K = 3

MLA_HEADS = 6
MLA_NOPE = 64
MLA_ROPE = 32
MLA_V = 64
MLA_Q_RANK = 256
MLA_KV_RANK = 256
MLA_WIDTH = MLA_HEADS * MLA_V
MLA_SLOT = 128
MLA_Q_NOPE_W = MLA_HEADS * MLA_NOPE
MLA_ACC_ROWS = MLA_V + 16
MLA_AHEAD = 2
MLA_UNROLL = 2
ROPE_BASE = 10000.0
ROPE_HALF = MLA_ROPE // 2

D_MIX = GLA_WIDTH + CONV_WIDTH + MLA_WIDTH

MEM_HEADS = 4
MEM_HEAD_DIM = 128
MEM_INNER = MEM_HEADS * MEM_HEAD_DIM

LOG2E = math.log2(math.e)
MASK_VALUE = -1e30

SEG_Q = 0
SEG_K = SEG_Q + GLA_QK_PAD
SEG_V = SEG_K + GLA_QK_PAD
SEG_GG = SEG_V + GLA_WIDTH
SEG_CC = SEG_GG + GLA_WIDTH
SEG_CB = SEG_CC + CONV_WIDTH
SEG_CH = SEG_CB + CONV_WIDTH
SEG_CG = SEG_CH + CONV_WIDTH
SEG_CQ = SEG_CG + CONV_WIDTH
SEG_CKV = SEG_CQ + MLA_Q_RANK
SEG_MG = SEG_CKV + MLA_KV_RANK
SEG_KR = SEG_MG + MLA_WIDTH
IN_PAD = SEG_KR + MLA_SLOT
KR_ROPE_OFF = MLA_NOPE
KR_GATE_OFF = MLA_NOPE + MLA_ROPE

VMEM_LIMIT_BYTES = 56 * 1024 * 1024
PROJ_TILE = 1024
PIECE_N = 256
GLA_FILL_STAGES = 4


def _silu(x):
    return x * (1.0 / (1.0 + jnp.exp(-x)))


def _rms(x, g):
    return x * lax.rsqrt(jnp.mean(x * x, axis=-1, keepdims=True) + EPS) * g


def _dot(a, b):
    return jnp.dot(a, b, preferred_element_type=F32)


def _dot_nt(a, b):
    return lax.dot_general(a, b, (((1,), (1,)), ((), ())), preferred_element_type=F32)


def _dot_tn(a, b):
    return lax.dot_general(a, b, (((0,), (0,)), ((), ())), preferred_element_type=F32)


def _gla_tile(q, k, la, v, gate, gn, st_ref, og_ref, fill):
    C = GLA_CHUNK
    tile = q.shape[0]
    lane_k = lax.broadcasted_iota(jnp.int32, (1, GLA_QK_PAD), 1)
    lane_v = lax.broadcasted_iota(jnp.int32, (1, GLA_WIDTH), 1)
    kmask = [(lane_k >= h * GLA_DK) & (lane_k < (h + 1) * GLA_DK) for h in range(GLA_HEADS)]
    vmask = [(lane_v >= h * GLA_DV) & (lane_v < (h + 1) * GLA_DV) for h in range(GLA_HEADS)]
    row = lax.broadcasted_iota(jnp.int32, (C, 2 * C), 0)
    col = lax.broadcasted_iota(jnp.int32, (C, 2 * C), 1)
    cum2 = (row >= (col & (C - 1))).astype(BF16)
    hrow = lax.broadcasted_iota(jnp.int32, (C, GLA_HEADS * C), 0)
    hcol = lax.broadcasted_iota(jnp.int32, (C, GLA_HEADS * C), 1)
    tril_heads = hrow >= (hcol & (C - 1))
    srow = lax.broadcasted_iota(jnp.int32, (GLA_WIDTH, GLA_QK_PAD), 0)
    slane = lax.broadcasted_iota(jnp.int32, (GLA_WIDTH, GLA_QK_PAD), 1)
    same_head = None
    for h in range(GLA_HEADS):
        blk = ((srow >= h * GLA_DV) & (srow < (h + 1) * GLA_DV)
               & (slane >= h * GLA_DK) & (slane < (h + 1) * GLA_DK))
        same_head = blk if same_head is None else (same_head | blk)

    st = st_ref[...]
    for c in range(tile // C):
        sl = slice(c * C, (c + 1) * C)
        qc, kc, lac, vc = q[sl], k[sl], la[sl], v[sl]
        la_hi = lac.astype(BF16)
        la_lo = (lac - la_hi.astype(F32)).astype(BF16)
        b = _dot(cum2, jnp.concatenate([la_hi, la_lo], axis=0))
        fill()
        b_last = b[C - 1:C, :]
        q_dec = (qc * (GLA_DK ** -0.5) * jnp.exp(b)).astype(BF16)
        k_inv = kc * jnp.exp(-b)
        k_end = (kc * jnp.exp(b_last - b)).astype(BF16)
        decay = jnp.exp(b_last)
        k_heads = jnp.concatenate(
            [jnp.where(kmask[h], k_inv, 0.0) for h in range(GLA_HEADS)], axis=0).astype(BF16)
        a = _dot_nt(q_dec, k_heads)
        fill()
        a = jnp.where(tril_heads, a, 0.0).astype(BF16)
        v_heads = jnp.concatenate(
            [jnp.where(vmask[h], vc, 0.0) for h in range(GLA_HEADS)], axis=0).astype(BF16)
        o = _dot(a, v_heads) + _dot_nt(q_dec, st.astype(BF16))
        fill()
        upd = _dot_tn(vc.astype(BF16), k_end)
        fill()
        st = st * decay + jnp.where(same_head, upd, 0.0)
        o2 = o * o
        ms = jnp.zeros_like(o)
        for h in range(GLA_HEADS):
            mh = jnp.sum(jnp.where(vmask[h], o2, 0.0), axis=1, keepdims=True) * (1.0 / GLA_DV)
            ms = jnp.where(vmask[h], mh, ms)
        y = o * lax.rsqrt(ms + EPS) * gn
        og_ref[0, sl, :] = (y * gate[sl]).astype(BF16)
    st_ref[...] = st


def _proj_kernel(h_ref, g_ref, win_ref, wg_ref, bg_ref, gn_ref, convw_ref, qn_ref, wuq_ref,
                 kvn_ref, wukv_ref, cos_ref, sin_ref,
                 og_ref, oconv_ref, mq_ref, mk_ref, mv_ref, mg_ref,
                 ubuf_ref, st_ref, *, tile, attn_tile, q_scale):
    t = pl.program_id(1)

    @pl.when(t == 0)
    def _():
        ubuf_ref[0:8, :] = jnp.zeros((8, CONV_WIDTH), F32)
        st_ref[...] = jnp.zeros(st_ref.shape, F32)

    x = h_ref[0]
    xn = _rms(x, g_ref[...]).astype(BF16)

    def seg(a, b):
        return _dot(xn, win_ref[:, a:b])

    val = {}
    work = []

    def queue_segment(name, a, b):
        parts = []
        for n0 in range(a, b, PIECE_N):
            work.append((4.0, lambda n0=n0: parts.append(seg(n0, min(n0 + PIECE_N, b)))))
        work.append((0.0, lambda: val.update({name: jnp.concatenate(parts, axis=1)})))

    def q_up():
        cqn = _rms(val["cq"], qn_ref[...]).astype(BF16)
        val["qn"] = _dot(cqn, wuq_ref[:, 0:MLA_Q_NOPE_W]) * q_scale
        val["qr"] = _dot(cqn, wuq_ref[:, MLA_Q_NOPE_W:])

    kv_parts = []
    kv_width = MLA_HEADS * MLA_SLOT + MLA_WIDTH

    def kv_piece(n0):
        if n0 == 0:
            val["ckvn"] = _rms(val["ckv"], kvn_ref[...]).astype(BF16)
        kv_parts.append(_dot(val["ckvn"], wukv_ref[:, n0:min(n0 + PIECE_N, kv_width)]))

    queue_segment("cq", SEG_CQ, SEG_CKV)
    queue_segment("ckv", SEG_CKV, SEG_MG)
    work.append((2.5, q_up))
    for n0 in range(0, kv_width, PIECE_N):
        work.append((1.0, functools.partial(kv_piece, n0)))
    queue_segment("cc", SEG_CC, SEG_CB)
    queue_segment("ch", SEG_CH, SEG_CG)
    queue_segment("cb", SEG_CB, SEG_CH)
    queue_segment("cg", SEG_CG, SEG_CQ)
    queue_segment("mg", SEG_MG, SEG_KR)
    work.reverse()
    total_cost = sum(c for c, _ in work)
    slots = GLA_FILL_STAGES * (tile // GLA_CHUNK)
    progress = {"slot": 0, "cost": 0.0}

    def fill():
        progress["slot"] += 1
        target = total_cost * progress["slot"] / slots
        while work and progress["cost"] + 0.5 * work[-1][0] <= target:
            cost, thunk = work.pop()
            progress["cost"] += cost
            thunk()

    kr = seg(SEG_KR, IN_PAD)
    gq, gk, gv, gg = seg(SEG_Q, SEG_K), seg(SEG_K, SEG_V), seg(SEG_V, SEG_GG), seg(SEG_GG, SEG_CC)
    z = _dot(kr.astype(BF16), wg_ref[...]) + bg_ref[...]
    la = (jnp.minimum(z, 0.0) - jnp.log1p(jnp.exp(-jnp.abs(z)))) * (1.0 / GLA_TAU)
    _gla_tile(gq, gk, la, gv, _silu(gg), gn_ref[...], st_ref, og_ref, fill)
    while work:
        work.pop()[1]()

    u = val["cc"] * val["ch"]
    ubuf_ref[8:8 + tile, :] = u
    cw = convw_ref[...]
    conv = (cw[0:1, :] * ubuf_ref[6:6 + tile, :] + cw[1:2, :] * ubuf_ref[7:7 + tile, :]
            + cw[2:3, :] * u)
    oconv_ref[0] = (val["cb"] * conv * _silu(val["cg"])).astype(BF16)
    ubuf_ref[0:8, :] = u[tile - 8:tile, :]

    cos = cos_ref[0]
    sin = sin_ref[0]
    qn, qr = val["qn"], val["qr"]
    x1, x2 = qr[:, 0:MLA_SLOT], qr[:, MLA_SLOT:]
    cq, sq = cos * q_scale, sin * q_scale
    r1 = x1 * cq - x2 * sq
    r2 = x2 * cq + x1 * sq
    pad_rows = jnp.zeros((MLA_SLOT - MLA_NOPE - MLA_ROPE, attn_tile), BF16)
    for i in range(tile // attn_tile):
        rows = slice(i * attn_tile, (i + 1) * attn_tile)
        qnt = qn[rows].T.astype(BF16)
        r1t = r1[rows].T.astype(BF16)
        r2t = r2[rows].T.astype(BF16)
        for h in range(MLA_HEADS):
            base = h * MLA_SLOT
            mq_ref[0, i, base:base + MLA_NOPE, :] = qnt[h * MLA_NOPE:(h + 1) * MLA_NOPE]
            mq_ref[0, i, base + MLA_NOPE:base + MLA_NOPE + ROPE_HALF, :] = (
                r1t[h * ROPE_HALF:(h + 1) * ROPE_HALF])
            mq_ref[0, i, base + MLA_NOPE + ROPE_HALF:base + MLA_NOPE + MLA_ROPE, :] = (
                r2t[h * ROPE_HALF:(h + 1) * ROPE_HALF])
            mq_ref[0, i, base + MLA_NOPE + MLA_ROPE:base + MLA_SLOT, :] = pad_rows

    kvh = jnp.concatenate(kv_parts, axis=1)
    lane = lax.broadcasted_iota(jnp.int32, (1, MLA_SLOT), 1)
    first = (lane >= KR_ROPE_OFF) & (lane < KR_ROPE_OFF + ROPE_HALF)
    second = (lane >= KR_ROPE_OFF + ROPE_HALF) & (lane < KR_ROPE_OFF + MLA_ROPE)
    krope = (kr * jnp.where(first | second, cos, 0.0)
             + pltpu.roll(kr, ROPE_HALF, 1) * jnp.where(second, sin, 0.0)
             - pltpu.roll(kr, MLA_SLOT - ROPE_HALF, 1) * jnp.where(first, sin, 0.0))
    for h in range(MLA_HEADS):
        sl = slice(h * MLA_SLOT, (h + 1) * MLA_SLOT)
        mk_ref[0, :, sl] = (kvh[:, sl] + krope).astype(BF16)
    vals = kvh[:, MLA_HEADS * MLA_SLOT:]
    for i in range(tile // attn_tile):
        mv_ref[0, i] = vals[i * attn_tile:(i + 1) * attn_tile].T.astype(BF16)
    mg_ref[0] = _silu(val["mg"])


def _proj_call(h, l, p, tables, tile, attn_tile):
    B, S, D = h.shape
    nt = S // tile
    sub = tile // attn_tile
    q_scale = LOG2E / math.sqrt(MLA_NOPE + MLA_ROPE)

    def tok(w):
        return pl.BlockSpec((1, tile, w), lambda b, t: (b, t, 0))

    def lay(shape):
        return pl.BlockSpec((None,) + shape, lambda b, t: (l,) + (0,) * len(shape))

    in_specs = [
        tok(D),
        lay((1, D)),
        lay((D, IN_PAD)),
        lay((MLA_SLOT, GLA_QK_PAD)),
        lay((1, GLA_QK_PAD)),
        lay((1, GLA_WIDTH)),
        lay((CONV_K, CONV_WIDTH)),
        lay((1, MLA_Q_RANK)),
        lay((MLA_Q_RANK, MLA_Q_NOPE_W + 2 * MLA_SLOT)),
        lay((1, MLA_KV_RANK)),
        lay((MLA_KV_RANK, MLA_HEADS * MLA_SLOT + MLA_WIDTH)),
        tok(MLA_SLOT), tok(MLA_SLOT),
    ]

    def tok_out(w, dt):
        return tok(w), jax.ShapeDtypeStruct((B, S, w), dt)

    def tile_t_out(w, dt):
        return (pl.BlockSpec((1, sub, w, attn_tile), lambda b, t: (b, t, 0, 0)),
                jax.ShapeDtypeStruct((B, S // attn_tile, w, attn_tile), dt))

    outs = [
        tok_out(GLA_WIDTH, BF16), tok_out(CONV_WIDTH, BF16),
        tile_t_out(MLA_HEADS * MLA_SLOT, BF16), tok_out(MLA_HEADS * MLA_SLOT, BF16),
        tile_t_out(MLA_WIDTH, BF16), tok_out(MLA_WIDTH, F32),
    ]
    return pl.pallas_call(
        functools.partial(_proj_kernel, tile=tile, attn_tile=attn_tile, q_scale=q_scale),
        grid=(B, nt),
        in_specs=in_specs,
        out_specs=[spec for spec, _ in outs],
        out_shape=[shape for _, shape in outs],
        scratch_shapes=[pltpu.VMEM((tile + 8, CONV_WIDTH), F32),
                        pltpu.VMEM((GLA_WIDTH, GLA_QK_PAD), F32)],
        compiler_params=pltpu.CompilerParams(
            dimension_semantics=("arbitrary", "arbitrary"),
            vmem_limit_bytes=VMEM_LIMIT_BYTES),
        name="proj",
    )(h, p["norm_mix"], p["w_in"], p["wg"], p["bg"], p["gla_norm"], p["conv_w"], p["q_norm"],
      p["w_uq"], p["kv_norm"], p["w_ukv"], *tables)


def _mla_kernel(qt_ref, k_ref, vt_ref, mg_ref, o_ref, m_ref, acc_ref, s_ref, *, tile):
    qi = pl.program_id(1)
    nt = pl.num_programs(1)
    half = tile // 2
    krow = lax.broadcasted_iota(jnp.int32, (half, tile), 0)
    qcol = lax.broadcasted_iota(jnp.int32, (half, tile), 1)
    causal_top = krow <= qcol
    causal_sq = causal_top[:, 0:half]
    ones_rows = jnp.ones((MLA_ACC_ROWS - MLA_V, tile), BF16)

    m_ref[...] = jnp.full(m_ref.shape, MASK_VALUE, F32)
    acc_ref[...] = jnp.zeros(acc_ref.shape, F32)

    def head(e):
        return slice(e * MLA_SLOT, (e + 1) * MLA_SLOT)

    def scores(j, e, q_tile):
        rows = pl.ds(pl.multiple_of(j * tile, tile), tile)
        s_ref[e] = _dot(k_ref[0, rows, head(e)], qt_ref[0, q_tile, head(e), :])

    def scores_diag(e):
        top = pl.ds(pl.multiple_of(qi * tile, tile), half)
        bot = pl.ds(pl.multiple_of(qi * tile + half, half), half)
        s_ref[e, 0:half, :] = _dot(k_ref[0, top, head(e)], qt_ref[0, qi, head(e), :])
        s_ref[e, half:tile, half:tile] = _dot(k_ref[0, bot, head(e)],
                                              qt_ref[0, qi, head(e), half:tile])

    def values(j, e, lo, hi):
        vt = vt_ref[0, j, e * MLA_V:(e + 1) * MLA_V, lo:hi]
        return jnp.concatenate([vt, ones_rows[:, lo:hi]], axis=0)

    def consume(j, e):
        s = s_ref[e]
        m_prev = m_ref[e]
        m_new = jnp.maximum(m_prev, jnp.max(s, axis=0, keepdims=True))
        alpha = jnp.exp2(m_prev - m_new)
        p = jnp.exp2(s - m_new).astype(BF16)
        acc_ref[e] = alpha * acc_ref[e] + _dot(values(j, e, 0, tile), p)
        m_ref[e] = m_new

    def consume_diag(e):
        s = jnp.where(causal_top, s_ref[e, 0:half, :], MASK_VALUE)
        m_prev = m_ref[e]
        m_new = jnp.maximum(m_prev, jnp.max(s, axis=0, keepdims=True))
        p = jnp.exp2(s - m_new).astype(BF16)
        acc = jnp.exp2(m_prev - m_new) * acc_ref[e] + _dot(values(qi, e, 0, half), p)
        s2 = jnp.where(causal_sq, s_ref[e, half:tile, half:tile], MASK_VALUE)
        m2_prev = m_new[:, half:tile]
        m2_new = jnp.maximum(m2_prev, jnp.max(s2, axis=0, keepdims=True))
        p2 = jnp.exp2(s2 - m2_new).astype(BF16)
        acc2 = (jnp.exp2(m2_prev - m2_new) * acc[:, half:tile]
                + _dot(values(qi, e, half, tile), p2))
        acc = jnp.concatenate([acc[:, 0:half], acc2], axis=1)
        return acc[0:MLA_V, :] / acc[MLA_V:MLA_V + 1, :]

    nxt = jnp.minimum(qi + 1, nt - 1)

    @pl.when(qi == 0)
    def _():
        for e in range(MLA_AHEAD):
            scores(0, e, 0)

    def blocks(j0, count):
        for t in range(count * MLA_HEADS):
            ahead = t + MLA_AHEAD
            scores(j0 + ahead // MLA_HEADS, ahead % MLA_HEADS, qi)
            consume(j0 + t // MLA_HEADS, t % MLA_HEADS)

    def body(i, carry):
        blocks(MLA_UNROLL * i, MLA_UNROLL)
        return carry

    lax.fori_loop(0, qi // MLA_UNROLL, body, 0)
    rem = qi % MLA_UNROLL
    size = MLA_UNROLL // 2
    while size >= 1:
        @pl.when((rem & size) != 0)
        def _(size=size):
            blocks(qi - (rem & (2 * size - 1)), size)
        size //= 2

    outs = []
    for e in range(MLA_HEADS):
        ahead = e + MLA_AHEAD
        if ahead < MLA_HEADS:
            scores_diag(ahead)
        else:
            scores(0, ahead - MLA_HEADS, nxt)
        outs.append(consume_diag(e))
    ot = jnp.concatenate(outs, axis=0)
    o_ref[0] = (ot.T * mg_ref[0]).astype(BF16)


def _mla_call(mqt, mk, mvt, mg, tile):
    B, nt, _, _ = mqt.shape
    S = nt * tile
    return pl.pallas_call(
        functools.partial(_mla_kernel, tile=tile),
        grid=(B, nt),
        in_specs=[
            pl.BlockSpec((1, nt, MLA_HEADS * MLA_SLOT, tile), lambda b, i: (b, 0, 0, 0)),
            pl.BlockSpec((1, S, MLA_HEADS * MLA_SLOT), lambda b, i: (b, 0, 0)),
            pl.BlockSpec((1, nt, MLA_WIDTH, tile), lambda b, i: (b, 0, 0, 0)),
            pl.BlockSpec((1, tile, MLA_WIDTH), lambda b, i: (b, i, 0)),
        ],
        out_specs=pl.BlockSpec((1, tile, MLA_WIDTH), lambda b, i: (b, i, 0)),
        out_shape=jax.ShapeDtypeStruct((B, S, MLA_WIDTH), BF16),
        scratch_shapes=[pltpu.VMEM((MLA_HEADS, 1, tile), F32),
                        pltpu.VMEM((MLA_HEADS, MLA_ACC_ROWS, tile), F32),
                        pltpu.VMEM((MLA_HEADS, tile, tile), F32)],
        compiler_params=pltpu.CompilerParams(
            dimension_semantics=("arbitrary", "arbitrary"),
            vmem_limit_bytes=VMEM_LIMIT_BYTES),
        name="mla",
    )(mqt, mk, mvt, mg)


def _memkv_kernel(mem_ref, g_ref, wk_ref, wv_ref, k_ref, v_ref):
    B, M, D = mem_ref.shape
    memn = _rms(mem_ref[...].reshape(B * M, D), g_ref[...]).astype(BF16)
    k_ref[...] = _dot(memn, wk_ref[...].astype(BF16)).astype(BF16).reshape(B, M, MEM_INNER)
    v_ref[...] = _dot(memn, wv_ref[...].astype(BF16)).astype(BF16).reshape(B, M, MEM_INNER)


def _memkv_call(mem, norm_mem, wk, wv):
    B, M, D = mem.shape
    L = wk.shape[0]
    kv_shape = jax.ShapeDtypeStruct((L, B, M, MEM_INNER), BF16)
    kv_spec = pl.BlockSpec((None, B, M, MEM_INNER), lambda l: (l, 0, 0, 0))
    return pl.pallas_call(
        _memkv_kernel,
        grid=(L,),
        in_specs=[pl.BlockSpec((B, M, D), lambda l: (0, 0, 0)),
                  pl.BlockSpec((None, 1, D), lambda l: (l, 0, 0)),
                  pl.BlockSpec((None, D, MEM_INNER), lambda l: (l, 0, 0)),
                  pl.BlockSpec((None, D, MEM_INNER), lambda l: (l, 0, 0))],
        out_specs=[kv_spec, kv_spec],
        out_shape=[kv_shape, kv_shape],
        compiler_params=pltpu.CompilerParams(
            dimension_semantics=("arbitrary",),
            vmem_limit_bytes=VMEM_LIMIT_BYTES),
        name="memkv",
    )(mem, norm_mem, wk, wv)


def _post_kernel(h_ref, og_ref, oc_ref, om_ref, wout_ref, gx_ref, wq_ref, mk_ref, mv_ref,
                 wo_ref, gf_ref, o_ref, wout_bf, wq_bf, wo_bf, *, final_norm):
    @pl.when((pl.program_id(0) == 0) & (pl.program_id(1) == 0))
    def _():
        wout_bf[...] = wout_ref[...].astype(BF16)
        wq_bf[...] = wq_ref[...].astype(BF16)
        wo_bf[...] = wo_ref[...].astype(BF16)

    x = jnp.concatenate([og_ref[0], oc_ref[0], om_ref[0]], axis=1)
    h1 = h_ref[0] + _dot(x, wout_bf[...])
    hn = _rms(h1, gx_ref[...]).astype(BF16)
    q = (_dot(hn, wq_bf[...]) * (LOG2E / math.sqrt(MEM_HEAD_DIM))).astype(BF16)
    ones = jnp.ones((mv_ref.shape[0], MEM_HEAD_DIM), BF16)
    heads = []
    for hd in range(MEM_HEADS):
        sl = slice(hd * MEM_HEAD_DIM, (hd + 1) * MEM_HEAD_DIM)
        s = _dot_nt(q[:, sl], mk_ref[:, sl])
        p = jnp.exp2(s - jnp.max(s, axis=1, keepdims=True)).astype(BF16)
        nd = _dot(p, jnp.concatenate([mv_ref[:, sl], ones], axis=1))
        heads.append((nd[:, :MEM_HEAD_DIM] / nd[:, MEM_HEAD_DIM:]).astype(BF16))
    o = jnp.concatenate(heads, axis=1)
    h2 = h1 + _dot(o, wo_bf[...])
    if final_norm:
        h2 = _rms(h2, gf_ref[...])
    o_ref[0] = h2


def _post_call(h, og, oc, om, memk, memv, l, p, norm_final, tile, final_norm):
    B, S, D = h.shape
    nt = S // tile
    M = memk.shape[2]

    def tok(w):
        return pl.BlockSpec((1, tile, w), lambda b, t: (b, t, 0))

    def lay(shape):
        return pl.BlockSpec((None,) + shape, lambda b, t: (l,) + (0,) * len(shape))

    mem_spec = pl.BlockSpec((None, None, M, MEM_INNER), lambda b, t: (l, b, 0, 0))
    return pl.pallas_call(
        functools.partial(_post_kernel, final_norm=final_norm),
        grid=(B, nt),
        in_specs=[tok(D), tok(GLA_WIDTH), tok(CONV_WIDTH), tok(MLA_WIDTH),
                  lay((D_MIX, D)), lay((1, D)), lay((D, MEM_INNER)), mem_spec, mem_spec,
                  lay((MEM_INNER, D)), pl.BlockSpec((1, D), lambda b, t: (0, 0))],
        out_specs=tok(D),
        out_shape=jax.ShapeDtypeStruct((B, S, D), F32),
        scratch_shapes=[pltpu.VMEM((D_MIX, D), BF16), pltpu.VMEM((D, MEM_INNER), BF16),
                        pltpu.VMEM((MEM_INNER, D), BF16)],
        compiler_params=pltpu.CompilerParams(
            dimension_semantics=("arbitrary", "arbitrary"),
            vmem_limit_bytes=VMEM_LIMIT_BYTES),
        name="post",
    )(h, og, oc, om, p["w_out"], p["norm_xattn"], p["mem_wq"], memk, memv, p["mem_wo"],
      norm_final)


def _pad_cols(w, width):
    return jnp.pad(w, [(0, 0)] * (w.ndim - 1) + [(0, width - w.shape[-1])])


def _pack_w_in_kernel(wt_ref, o_ref):
    cols = wt_ref.shape[2]
    o_gv = 2 * GLA_QK
    o_glr = o_gv + GLA_WIDTH
    o_gg = o_glr + GLA_GATE_RANK
    o_cc = o_gg + GLA_WIDTH
    o_kr = o_cc + 4 * CONV_WIDTH + MLA_Q_RANK + MLA_KV_RANK
    o_mg = o_kr + MLA_ROPE
    zeros = lambda n: jnp.zeros((n, cols), F32)

    def rows(a, b):
        return wt_ref[0, a:b, :]

    pieces = [
        (SEG_Q, [rows(0, GLA_QK), zeros(GLA_QK_PAD - GLA_QK)]),
        (SEG_K, [rows(GLA_QK, o_gv), zeros(GLA_QK_PAD - GLA_QK)]),
        (SEG_V, [rows(o_gv, o_glr)]),
        (SEG_GG, [rows(o_gg, o_cc)]),
        (SEG_CC, [rows(o_cc, o_kr)]),
        (SEG_MG, [rows(o_mg, o_mg + MLA_WIDTH)]),
        (SEG_KR, [zeros(KR_ROPE_OFF), rows(o_kr, o_mg), rows(o_glr, o_gg),
                  zeros(MLA_SLOT - KR_GATE_OFF - GLA_GATE_RANK)]),
    ]
    for off, parts in pieces:
        blk = parts[0] if len(parts) == 1 else jnp.concatenate(parts, axis=0)
        o_ref[0, :, off:off + blk.shape[0]] = blk.T.astype(BF16)


def _pack_w_in(w_in):
    wt = jnp.swapaxes(w_in, 1, 2)
    L, W, D = wt.shape
    cols = 256
    return pl.pallas_call(
        _pack_w_in_kernel,
        grid=(L, D // cols),
        in_specs=[pl.BlockSpec((1, W, cols), lambda l, c: (l, 0, c))],
        out_specs=pl.BlockSpec((1, cols, IN_PAD), lambda l, c: (l, c, 0)),
        out_shape=jax.ShapeDtypeStruct((L, D, IN_PAD), BF16),
        compiler_params=pltpu.CompilerParams(
            dimension_semantics=("arbitrary", "arbitrary"),
            vmem_limit_bytes=VMEM_LIMIT_BYTES),
        name="pack_w_in",
    )(wt)


def _prep_params(norm_mix, w_in, gla_w_gate, gla_b_gate, gla_norm, conv_w, mla_q_norm, mla_w_uq,
                 mla_kv_norm, mla_w_ukv, w_out, norm_xattn, mem_wq, mem_wo):
    L = w_in.shape[0]
    wg = jnp.zeros((L, MLA_SLOT, GLA_QK_PAD), F32)
    wg = wg.at[:, KR_GATE_OFF:KR_GATE_OFF + GLA_GATE_RANK, :GLA_QK].set(gla_w_gate)
    w_uq = mla_w_uq.reshape(L, MLA_Q_RANK, MLA_HEADS, MLA_NOPE + MLA_ROPE)
    half = lambda lo: _pad_cols(
        w_uq[..., lo:lo + ROPE_HALF].reshape(L, MLA_Q_RANK, MLA_HEADS * ROPE_HALF), MLA_SLOT)
    w_uq = jnp.concatenate(
        [w_uq[..., :MLA_NOPE].reshape(L, MLA_Q_RANK, MLA_Q_NOPE_W), half(MLA_NOPE),
         half(MLA_NOPE + ROPE_HALF)], axis=-1)
    w_ukv = mla_w_ukv.reshape(L, MLA_KV_RANK, MLA_HEADS, MLA_NOPE + MLA_V)
    w_uk = _pad_cols(w_ukv[..., :MLA_NOPE], MLA_SLOT).reshape(L, MLA_KV_RANK, MLA_HEADS * MLA_SLOT)
    w_uv = w_ukv[..., MLA_NOPE:].reshape(L, MLA_KV_RANK, MLA_WIDTH)
    return {
        "norm_mix": norm_mix[:, None, :],
        "w_in": _pack_w_in(w_in),
        "wg": wg.astype(BF16),
        "bg": _pad_cols(gla_b_gate, GLA_QK_PAD)[:, None, :],
        "gla_norm": jnp.tile(gla_norm, (1, GLA_HEADS))[:, None, :],
        "conv_w": conv_w,
        "q_norm": mla_q_norm[:, None, :],
        "w_uq": w_uq.astype(BF16),
        "kv_norm": mla_kv_norm[:, None, :],
        "w_ukv": jnp.concatenate([w_uk, w_uv], axis=-1).astype(BF16),
        "w_out": w_out,
        "norm_xattn": norm_xattn[:, None, :],
        "mem_wq": mem_wq,
        "mem_wo": mem_wo,
    }


def _rope_tables(positions):
    inv_freq = 1.0 / (ROPE_BASE ** (jnp.arange(0, MLA_ROPE, 2, dtype=F32) / MLA_ROPE))
    lane_freq = jnp.tile(inv_freq, MLA_SLOT // ROPE_HALF)
    ang = positions.astype(F32)[..., None] * lane_freq
    return jnp.cos(ang), jnp.sin(ang)


def _pick_tile(S, want):
    t = min(S, want)
    assert S % t == 0 and t % GLA_CHUNK == 0
    return t


def kernel(x, mem, positions, norm_mix, w_in, gla_w_gate, gla_b_gate, gla_norm, conv_w, mla_q_norm,
           mla_w_uq, mla_kv_norm, mla_w_ukv, w_out, norm_xattn, norm_mem, mem_wq, mem_wk, mem_wv,
           mem_wo, norm_final):
    depth = w_in.shape[0]
    S = x.shape[1]
    attn_tile = _pick_tile(S, 512)
    tile = _pick_tile(S, PROJ_TILE)
    p = _prep_params(norm_mix, w_in, gla_w_gate, gla_b_gate, gla_norm, conv_w, mla_q_norm,
                     mla_w_uq, mla_kv_norm, mla_w_ukv, w_out, norm_xattn, mem_wq, mem_wo)
    tables = _rope_tables(positions)
    memk, memv = _memkv_call(mem, norm_mem[:, None, :], mem_wk, mem_wv)
    nf = norm_final[None, :]
    h = x
    for l in range(depth):
        og, oc, mq, mk, mv, mg = _proj_call(h, l, p, tables, tile, attn_tile)
        om = _mla_call(mq, mk, mv, mg, attn_tile)
        h = _post_call(h, og, oc, om, memk, memv, l, p, nf, tile, l == depth - 1)
    return h
```

```python
import functools
import math

import jax
import jax.numpy as jnp
from jax import lax
from jax.experimental import pallas as pl
from jax.experimental.pallas import tpu as pltpu

F32 = jnp.float32
BF16 = jnp.bfloat16

D_MODEL = 1024
EPS = 1e-6

GLA_HEADS = 4
GLA_DV = 96
GLA_DK = 48
GLA_QK = GLA_HEADS * GLA_DK
GLA_WIDTH = GLA_HEADS * GLA_DV
GLA_GATE_RANK = 16
GLA_TAU = 16.0
GLA_CHUNK = 64
GLA_QK_PAD = 256

CONV_WIDTH = 256
CONV_K = 3

MLA_HEADS = 6
MLA_NOPE = 64
MLA_ROPE = 32
MLA_V = 64
MLA_Q_RANK = 256
MLA_KV_RANK = 256
MLA_WIDTH = MLA_HEADS * MLA_V
MLA_SLOT = 128
MLA_Q_NOPE_W = MLA_HEADS * MLA_NOPE
MLA_ACC_ROWS = MLA_V + 16
MLA_AHEAD = 2
MLA_UNROLL = 2
ROPE_BASE = 10000.0
ROPE_HALF = MLA_ROPE // 2

D_MIX = GLA_WIDTH + CONV_WIDTH + MLA_WIDTH

MEM_HEADS = 4
MEM_HEAD_DIM = 128
MEM_INNER = MEM_HEADS * MEM_HEAD_DIM

LOG2E = math.log2(math.e)
MASK_VALUE = -1e30

SEG_Q = 0
SEG_K = SEG_Q + GLA_QK_PAD
SEG_V = SEG_K + GLA_QK_PAD
SEG_GG = SEG_V + GLA_WIDTH
SEG_CC = SEG_GG + GLA_WIDTH
SEG_CB = SEG_CC + CONV_WIDTH
SEG_CH = SEG_CB + CONV_WIDTH
SEG_CG = SEG_CH + CONV_WIDTH
SEG_CQ = SEG_CG + CONV_WIDTH
SEG_CKV = SEG_CQ + MLA_Q_RANK
SEG_MG = SEG_CKV + MLA_KV_RANK
SEG_KR = SEG_MG + MLA_WIDTH
IN_PAD = SEG_KR + MLA_SLOT
KR_ROPE_OFF = MLA_NOPE
KR_GATE_OFF = MLA_NOPE + MLA_ROPE

VMEM_LIMIT_BYTES = 56 * 1024 * 1024
PROJ_TILE = 1024
PIECE_N = 256
GLA_FILL_STAGES = 4


def _silu(x):
    return x * (1.0 / (1.0 + jnp.exp(-x)))


def _rms(x, g):
    return x * lax.rsqrt(jnp.mean(x * x, axis=-1, keepdims=True) + EPS) * g


def _dot(a, b):
    return jnp.dot(a, b, preferred_element_type=F32)


def _dot_nt(a, b):
    return lax.dot_general(a, b, (((1,), (1,)), ((), ())), preferred_element_type=F32)


def _dot_tn(a, b):
    return lax.dot_general(a, b, (((0,), (0,)), ((), ())), preferred_element_type=F32)


def _gla_tile(q, k, la, v, gate, gn, st_ref, og_ref, fill):
    C = GLA_CHUNK
    tile = q.shape[0]
    lane_k = lax.broadcasted_iota(jnp.int32, (1, GLA_QK_PAD), 1)
    lane_v = lax.broadcasted_iota(jnp.int32, (1, GLA_WIDTH), 1)
    kmask = [(lane_k >= h * GLA_DK) & (lane_k < (h + 1) * GLA_DK) for h in range(GLA_HEADS)]
    vmask = [(lane_v >= h * GLA_DV) & (lane_v < (h + 1) * GLA_DV) for h in range(GLA_HEADS)]
    row = lax.broadcasted_iota(jnp.int32, (C, 2 * C), 0)
    col = lax.broadcasted_iota(jnp.int32, (C, 2 * C), 1)
    cum2 = (row >= (col & (C - 1))).astype(BF16)
    hrow = lax.broadcasted_iota(jnp.int32, (C, GLA_HEADS * C), 0)
    hcol = lax.broadcasted_iota(jnp.int32, (C, GLA_HEADS * C), 1)
    tril_heads = hrow >= (hcol & (C - 1))
    srow = lax.broadcasted_iota(jnp.int32, (GLA_WIDTH, GLA_QK_PAD), 0)
    slane = lax.broadcasted_iota(jnp.int32, (GLA_WIDTH, GLA_QK_PAD), 1)
    same_head = None
    for h in range(GLA_HEADS):
        blk = ((srow >= h * GLA_DV) & (srow < (h + 1) * GLA_DV)
               & (slane >= h * GLA_DK) & (slane < (h + 1) * GLA_DK))
        same_head = blk if same_head is None else (same_head | blk)

    st = st_ref[...]
    for c in range(tile // C):
        sl = slice(c * C, (c + 1) * C)
        qc, kc, lac, vc = q[sl], k[sl], la[sl], v[sl]
        la_hi = lac.astype(BF16)
        la_lo = (lac - la_hi.astype(F32)).astype(BF16)
        b = _dot(cum2, jnp.concatenate([la_hi, la_lo], axis=0))
        fill()
        b_last = b[C - 1:C, :]
        q_dec = (qc * (GLA_DK ** -0.5) * jnp.exp(b)).astype(BF16)
        k_inv = kc * jnp.exp(-b)
        k_end = (kc * jnp.exp(b_last - b)).astype(BF16)
        decay = jnp.exp(b_last)
        k_heads = jnp.concatenate(
            [jnp.where(kmask[h], k_inv, 0.0) for h in range(GLA_HEADS)], axis=0).astype(BF16)
        a = _dot_nt(q_dec, k_heads)
        fill()
        a = jnp.where(tril_heads, a, 0.0).astype(BF16)
        v_heads = jnp.concatenate(
            [jnp.where(vmask[h], vc, 0.0) for h in range(GLA_HEADS)], axis=0).astype(BF16)
        o = _dot(a, v_heads) + _dot_nt(q_dec, st.astype(BF16))
        fill()
        upd = _dot_tn(vc.astype(BF16), k_end)
        fill()
        st = st * decay + jnp.where(same_head, upd, 0.0)
        o2 = o * o
        ms = jnp.zeros_like(o)
        for h in range(GLA_HEADS):
            mh = jnp.sum(jnp.where(vmask[h], o2, 0.0), axis=1, keepdims=True) * (1.0 / GLA_DV)
            ms = jnp.where(vmask[h], mh, ms)
        y = o * lax.rsqrt(ms + EPS) * gn
        og_ref[0, sl, :] = (y * gate[sl]).astype(BF16)
    st_ref[...] = st


def _proj_kernel(h_ref, g_ref, win_ref, wg_ref, bg_ref, gn_ref, convw_ref, qn_ref, wuq_ref,
                 kvn_ref, wukv_ref, cos_ref, sin_ref,
                 og_ref, oconv_ref, mq_ref, mk_ref, mv_ref, mg_ref,
                 ubuf_ref, st_ref, *, tile, attn_tile, q_scale):
    t = pl.program_id(1)

    @pl.when(t == 0)
    def _():
        ubuf_ref[0:8, :] = jnp.zeros((8, CONV_WIDTH), F32)
        st_ref[...] = jnp.zeros(st_ref.shape, F32)

    x = h_ref[0]
    xn = _rms(x, g_ref[...]).astype(BF16)

    def seg(a, b):
        return _dot(xn, win_ref[:, a:b])

    val = {}
    work = []

    def queue_segment(name, a, b):
        parts = []
        for n0 in range(a, b, PIECE_N):
            work.append((4.0, lambda n0=n0: parts.append(seg(n0, min(n0 + PIECE_N, b)))))
        work.append((0.0, lambda: val.update({name: jnp.concatenate(parts, axis=1)})))

    def q_up():
        cqn = _rms(val["cq"], qn_ref[...]).astype(BF16)
        val["qn"] = _dot(cqn, wuq_ref[:, 0:MLA_Q_NOPE_W]) * q_scale
        val["qr"] = _dot(cqn, wuq_ref[:, MLA_Q_NOPE_W:])

    kv_parts = []
    kv_width = MLA_HEADS * MLA_SLOT + MLA_WIDTH

    def kv_piece(n0):
        if n0 == 0:
            val["ckvn"] = _rms(val["ckv"], kvn_ref[...]).astype(BF16)
        kv_parts.append(_dot(val["ckvn"], wukv_ref[:, n0:min(n0 + PIECE_N, kv_width)]))

    queue_segment("cq", SEG_CQ, SEG_CKV)
    queue_segment("ckv", SEG_CKV, SEG_MG)
    work.append((2.5, q_up))
    for n0 in range(0, kv_width, PIECE_N):
        work.append((1.0, functools.partial(kv_piece, n0)))
    queue_segment("cc", SEG_CC, SEG_CB)
    queue_segment("ch", SEG_CH, SEG_CG)
    queue_segment("cb", SEG_CB, SEG_CH)
    queue_segment("cg", SEG_CG, SEG_CQ)
    queue_segment("mg", SEG_MG, SEG_KR)
    work.reverse()
    total_cost = sum(c for c, _ in work)
    slots = GLA_FILL_STAGES * (tile // GLA_CHUNK)
    progress = {"slot": 0, "cost": 0.0}

    def fill():
        progress["slot"] += 1
        target = total_cost * progress["slot"] / slots
        while work and progress["cost"] + 0.5 * work[-1][0] <= target:
            cost, thunk = work.pop()
            progress["cost"] += cost
            thunk()

    kr = seg(SEG_KR, IN_PAD)
    gq, gk, gv, gg = seg(SEG_Q, SEG_K), seg(SEG_K, SEG_V), seg(SEG_V, SEG_GG), seg(SEG_GG, SEG_CC)
    z = _dot(kr.astype(BF16), wg_ref[...]) + bg_ref[...]
    la = (jnp.minimum(z, 0.0) - jnp.log1p(jnp.exp(-jnp.abs(z)))) * (1.0 / GLA_TAU)
    _gla_tile(gq, gk, la, gv, _silu(gg), gn_ref[...], st_ref, og_ref, fill)
    while work:
        work.pop()[1]()

    u = val["cc"] * val["ch"]
    ubuf_ref[8:8 + tile, :] = u
    cw = convw_ref[...]
    conv = (cw[0:1, :] * ubuf_ref[6:6 + tile, :] + cw[1:2, :] * ubuf_ref[7:7 + tile, :]
            + cw[2:3, :] * u)
    oconv_ref[0] = (val["cb"] * conv * _silu(val["cg"])).astype(BF16)
    ubuf_ref[0:8, :] = u[tile - 8:tile, :]

    reps = MLA_SLOT // ROPE_HALF
    cos = jnp.concatenate([cos_ref[0]] * reps, axis=0).T
    sin = jnp.concatenate([sin_ref[0]] * reps, axis=0).T
    qn, qr = val["qn"], val["qr"]
    x1, x2 = qr[:, 0:MLA_SLOT], qr[:, MLA_SLOT:]
    cq, sq = cos * q_scale, sin * q_scale
    r1 = x1 * cq - x2 * sq
    r2 = x2 * cq + x1 * sq
    pad_rows = jnp.zeros((MLA_SLOT - MLA_NOPE - MLA_ROPE, attn_tile), BF16)
    for i in range(tile // attn_tile):
        rows = slice(i * attn_tile, (i + 1) * attn_tile)
        qnt = qn[rows].T.astype(BF16)
        r1t = r1[rows].T.astype(BF16)
        r2t = r2[rows].T.astype(BF16)
        for h in range(MLA_HEADS):
            base = h * MLA_SLOT
            mq_ref[0, i, base:base + MLA_NOPE, :] = qnt[h * MLA_NOPE:(h + 1) * MLA_NOPE]
            mq_ref[0, i, base + MLA_NOPE:base + MLA_NOPE + ROPE_HALF, :] = (
                r1t[h * ROPE_HALF:(h + 1) * ROPE_HALF])
            mq_ref[0, i, base + MLA_NOPE + ROPE_HALF:base + MLA_NOPE + MLA_ROPE, :] = (
                r2t[h * ROPE_HALF:(h + 1) * ROPE_HALF])
            mq_ref[0, i, base + MLA_NOPE + MLA_ROPE:base + MLA_SLOT, :] = pad_rows

    kvh = jnp.concatenate(kv_parts, axis=1)
    lane = lax.broadcasted_iota(jnp.int32, (1, MLA_SLOT), 1)
    first = (lane >= KR_ROPE_OFF) & (lane < KR_ROPE_OFF + ROPE_HALF)
    second = (lane >= KR_ROPE_OFF + ROPE_HALF) & (lane < KR_ROPE_OFF + MLA_ROPE)
    krope = (kr * jnp.where(first | second, cos, 0.0)
             + pltpu.roll(kr, ROPE_HALF, 1) * jnp.where(second, sin, 0.0)
             - pltpu.roll(kr, MLA_SLOT - ROPE_HALF, 1) * jnp.where(first, sin, 0.0))
    for h in range(MLA_HEADS):
        sl = slice(h * MLA_SLOT, (h + 1) * MLA_SLOT)
        mk_ref[0, :, sl] = (kvh[:, sl] + krope).astype(BF16)
    vals = kvh[:, MLA_HEADS * MLA_SLOT:]
    for i in range(tile // attn_tile):
        mv_ref[0, i] = vals[i * attn_tile:(i + 1) * attn_tile].T.astype(BF16)
    mg_ref[0] = _silu(val["mg"])


def _proj_call(h, l, p, tables, tile, attn_tile):
    B, S, D = h.shape
    nt = S // tile
    sub = tile // attn_tile
    q_scale = LOG2E / math.sqrt(MLA_NOPE + MLA_ROPE)

    def tok(w):
        return pl.BlockSpec((1, tile, w), lambda b, t: (b, t, 0))

    def lay(shape):
        return pl.BlockSpec((None,) + shape, lambda b, t: (l,) + (0,) * len(shape))

    in_specs = [
        tok(D),
        lay((1, D)),
        lay((D, IN_PAD)),
        lay((MLA_SLOT, GLA_QK_PAD)),
        lay((1, GLA_QK_PAD)),
        lay((1, GLA_WIDTH)),
        lay((CONV_K, CONV_WIDTH)),
        lay((1, MLA_Q_RANK)),
        lay((MLA_Q_RANK, MLA_Q_NOPE_W + 2 * MLA_SLOT)),
        lay((1, MLA_KV_RANK)),
        lay((MLA_KV_RANK, MLA_HEADS * MLA_SLOT + MLA_WIDTH)),
        pl.BlockSpec((1, ROPE_HALF, tile), lambda b, t: (b, 0, t)),
        pl.BlockSpec((1, ROPE_HALF, tile), lambda b, t: (b, 0, t)),
    ]

    def tok_out(w, dt):
        return tok(w), jax.ShapeDtypeStruct((B, S, w), dt)

    def tile_t_out(w, dt):
        return (pl.BlockSpec((1, sub, w, attn_tile), lambda b, t: (b, t, 0, 0)),
                jax.ShapeDtypeStruct((B, S // attn_tile, w, attn_tile), dt))

    outs = [
        tok_out(GLA_WIDTH, BF16), tok_out(CONV_WIDTH, BF16),
        tile_t_out(MLA_HEADS * MLA_SLOT, BF16), tok_out(MLA_HEADS * MLA_SLOT, BF16),
        tile_t_out(MLA_WIDTH, BF16), tok_out(MLA_WIDTH, F32),
    ]
    return pl.pallas_call(
        functools.partial(_proj_kernel, tile=tile, attn_tile=attn_tile, q_scale=q_scale),
        grid=(B, nt),
        in_specs=in_specs,
        out_specs=[spec for spec, _ in outs],
        out_shape=[shape for _, shape in outs],
        scratch_shapes=[pltpu.VMEM((tile + 8, CONV_WIDTH), F32),
                        pltpu.VMEM((GLA_WIDTH, GLA_QK_PAD), F32)],
        compiler_params=pltpu.CompilerParams(
            dimension_semantics=("arbitrary", "arbitrary"),
            vmem_limit_bytes=VMEM_LIMIT_BYTES),
        name="proj",
    )(h, p["norm_mix"], p["w_in"], p["wg"], p["bg"], p["gla_norm"], p["conv_w"], p["q_norm"],
      p["w_uq"], p["kv_norm"], p["w_ukv"], *tables)


def _mla_kernel(qt_ref, k_ref, vt_ref, mg_ref, o_ref, m_ref, acc_ref, s_ref, *, tile):
    qi = pl.program_id(1)
    nt = pl.num_programs(1)
    half = tile // 2
    krow = lax.broadcasted_iota(jnp.int32, (half, tile), 0)
    qcol = lax.broadcasted_iota(jnp.int32, (half, tile), 1)
    causal_top = krow <= qcol
    causal_sq = causal_top[:, 0:half]
    ones_rows = jnp.ones((MLA_ACC_ROWS - MLA_V, tile), BF16)

    m_ref[...] = jnp.full(m_ref.shape, MASK_VALUE, F32)
    acc_ref[...] = jnp.zeros(acc_ref.shape, F32)

    def head(e):
        return slice(e * MLA_SLOT, (e + 1) * MLA_SLOT)

    def scores(j, e, q_tile):
        rows = pl.ds(pl.multiple_of(j * tile, tile), tile)
        s_ref[e] = _dot(k_ref[0, rows, head(e)], qt_ref[0, q_tile, head(e), :])

    def scores_diag(e):
        top = pl.ds(pl.multiple_of(qi * tile, tile), half)
        bot = pl.ds(pl.multiple_of(qi * tile + half, half), half)
        s_ref[e, 0:half, :] = _dot(k_ref[0, top, head(e)], qt_ref[0, qi, head(e), :])
        s_ref[e, half:tile, half:tile] = _dot(k_ref[0, bot, head(e)],
                                              qt_ref[0, qi, head(e), half:tile])

    def values(j, e, lo, hi):
        vt = vt_ref[0, j, e * MLA_V:(e + 1) * MLA_V, lo:hi]
        return jnp.concatenate([vt, ones_rows[:, lo:hi]], axis=0)

    def consume(j, e):
        s = s_ref[e]
        m_prev = m_ref[e]
        m_new = jnp.maximum(m_prev, jnp.max(s, axis=0, keepdims=True))
        alpha = jnp.exp2(m_prev - m_new)
        p = jnp.exp2(s - m_new).astype(BF16)
        acc_ref[e] = alpha * acc_ref[e] + _dot(values(j, e, 0, tile), p)
        m_ref[e] = m_new

    def consume_diag(e):
        s = jnp.where(causal_top, s_ref[e, 0:half, :], MASK_VALUE)
        m_prev = m_ref[e]
        m_new = jnp.maximum(m_prev, jnp.max(s, axis=0, keepdims=True))
        p = jnp.exp2(s - m_new).astype(BF16)
        acc = jnp.exp2(m_prev - m_new) * acc_ref[e] + _dot(values(qi, e, 0, half), p)
        s2 = jnp.where(causal_sq, s_ref[e, half:tile, half:tile], MASK_VALUE)
        m2_prev = m_new[:, half:tile]
        m2_new = jnp.maximum(m2_prev, jnp.max(s2, axis=0, keepdims=True))
        p2 = jnp.exp2(s2 - m2_new).astype(BF16)
        acc2 = (jnp.exp2(m2_prev - m2_new) * acc[:, half:tile]
                + _dot(values(qi, e, half, tile), p2))
        acc = jnp.concatenate([acc[:, 0:half], acc2], axis=1)
        return acc[0:MLA_V, :] / acc[MLA_V:MLA_V + 1, :]

    nxt = jnp.minimum(qi + 1, nt - 1)

    @pl.when(qi == 0)
    def _():
        for e in range(MLA_AHEAD):
            scores(0, e, 0)

    def blocks(j0, count):
        for t in range(count * MLA_HEADS):
            ahead = t + MLA_AHEAD
            scores(j0 + ahead // MLA_HEADS, ahead % MLA_HEADS, qi)
            consume(j0 + t // MLA_HEADS, t % MLA_HEADS)

    def body(i, carry):
        blocks(MLA_UNROLL * i, MLA_UNROLL)
        return carry

    lax.fori_loop(0, qi // MLA_UNROLL, body, 0)
    rem = qi % MLA_UNROLL
    size = MLA_UNROLL // 2
    while size >= 1:
        @pl.when((rem & size) != 0)
        def _(size=size):
            blocks(qi - (rem & (2 * size - 1)), size)
        size //= 2

    outs = []
    for e in range(MLA_HEADS):
        ahead = e + MLA_AHEAD
        if ahead < MLA_HEADS:
            scores_diag(ahead)
        else:
            scores(0, ahead - MLA_HEADS, nxt)
        outs.append(consume_diag(e))
    ot = jnp.concatenate(outs, axis=0)
    o_ref[0] = (ot.T * mg_ref[0]).astype(BF16)


def _mla_call(mqt, mk, mvt, mg, tile):
    B, nt, _, _ = mqt.shape
    S = nt * tile
    return pl.pallas_call(
        functools.partial(_mla_kernel, tile=tile),
        grid=(B, nt),
        in_specs=[
            pl.BlockSpec((1, nt, MLA_HEADS * MLA_SLOT, tile), lambda b, i: (b, 0, 0, 0)),
            pl.BlockSpec((1, S, MLA_HEADS * MLA_SLOT), lambda b, i: (b, 0, 0)),
            pl.BlockSpec((1, nt, MLA_WIDTH, tile), lambda b, i: (b, 0, 0, 0)),
            pl.BlockSpec((1, tile, MLA_WIDTH), lambda b, i: (b, i, 0)),
        ],
        out_specs=pl.BlockSpec((1, tile, MLA_WIDTH), lambda b, i: (b, i, 0)),
        out_shape=jax.ShapeDtypeStruct((B, S, MLA_WIDTH), BF16),
        scratch_shapes=[pltpu.VMEM((MLA_HEADS, 1, tile), F32),
                        pltpu.VMEM((MLA_HEADS, MLA_ACC_ROWS, tile), F32),
                        pltpu.VMEM((MLA_HEADS, tile, tile), F32)],
        compiler_params=pltpu.CompilerParams(
            dimension_semantics=("arbitrary", "arbitrary"),
            vmem_limit_bytes=VMEM_LIMIT_BYTES),
        name="mla",
    )(mqt, mk, mvt, mg)


def _memkv_kernel(mem_ref, g_ref, wk_ref, wv_ref, k_ref, v_ref):
    B, M, D = mem_ref.shape
    memn = _rms(mem_ref[...].reshape(B * M, D), g_ref[...]).astype(BF16)
    k_ref[...] = _dot(memn, wk_ref[...].astype(BF16)).astype(BF16).reshape(B, M, MEM_INNER)
    v_ref[...] = _dot(memn, wv_ref[...].astype(BF16)).astype(BF16).reshape(B, M, MEM_INNER)


def _memkv_call(mem, norm_mem, wk, wv):
    B, M, D = mem.shape
    L = wk.shape[0]
    kv_shape = jax.ShapeDtypeStruct((L, B, M, MEM_INNER), BF16)
    kv_spec = pl.BlockSpec((None, B, M, MEM_INNER), lambda l: (l, 0, 0, 0))
    return pl.pallas_call(
        _memkv_kernel,
        grid=(L,),
        in_specs=[pl.BlockSpec((B, M, D), lambda l: (0, 0, 0)),
                  pl.BlockSpec((None, 1, D), lambda l: (l, 0, 0)),
                  pl.BlockSpec((None, D, MEM_INNER), lambda l: (l, 0, 0)),
                  pl.BlockSpec((None, D, MEM_INNER), lambda l: (l, 0, 0))],
        out_specs=[kv_spec, kv_spec],
        out_shape=[kv_shape, kv_shape],
        compiler_params=pltpu.CompilerParams(
            dimension_semantics=("arbitrary",),
            vmem_limit_bytes=VMEM_LIMIT_BYTES),
        name="memkv",
    )(mem, norm_mem, wk, wv)


def _post_kernel(h_ref, og_ref, oc_ref, om_ref, wout_ref, gx_ref, wq_ref, mk_ref, mv_ref,
                 wo_ref, gf_ref, o_ref, wout_bf, wq_bf, wo_bf, *, final_norm):
    @pl.when((pl.program_id(0) == 0) & (pl.program_id(1) == 0))
    def _():
        wout_bf[...] = wout_ref[...].astype(BF16)
        wq_bf[...] = wq_ref[...].astype(BF16)
        wo_bf[...] = wo_ref[...].astype(BF16)

    x = jnp.concatenate([og_ref[0], oc_ref[0], om_ref[0]], axis=1)
    h1 = h_ref[0] + _dot(x, wout_bf[...])
    hn = _rms(h1, gx_ref[...]).astype(BF16)
    q = (_dot(hn, wq_bf[...]) * (LOG2E / math.sqrt(MEM_HEAD_DIM))).astype(BF16)
    ones = jnp.ones((mv_ref.shape[0], MEM_HEAD_DIM), BF16)
    heads = []
    for hd in range(MEM_HEADS):
        sl = slice(hd * MEM_HEAD_DIM, (hd + 1) * MEM_HEAD_DIM)
        s = _dot_nt(q[:, sl], mk_ref[:, sl])
        p = jnp.exp2(s - jnp.max(s, axis=1, keepdims=True)).astype(BF16)
        nd = _dot(p, jnp.concatenate([mv_ref[:, sl], ones], axis=1))
        heads.append((nd[:, :MEM_HEAD_DIM] / nd[:, MEM_HEAD_DIM:]).astype(BF16))
    o = jnp.concatenate(heads, axis=1)
    h2 = h1 + _dot(o, wo_bf[...])
    if final_norm:
        h2 = _rms(h2, gf_ref[...])
    o_ref[0] = h2


def _post_call(h, og, oc, om, memk, memv, l, p, norm_final, tile, final_norm):
    B, S, D = h.shape
    nt = S // tile
    M = memk.shape[2]

    def tok(w):
        return pl.BlockSpec((1, tile, w), lambda b, t: (b, t, 0))

    def lay(shape):
        return pl.BlockSpec((None,) + shape, lambda b, t: (l,) + (0,) * len(shape))

    mem_spec = pl.BlockSpec((None, None, M, MEM_INNER), lambda b, t: (l, b, 0, 0))
    return pl.pallas_call(
        functools.partial(_post_kernel, final_norm=final_norm),
        grid=(B, nt),
        in_specs=[tok(D), tok(GLA_WIDTH), tok(CONV_WIDTH), tok(MLA_WIDTH),
                  lay((D_MIX, D)), lay((1, D)), lay((D, MEM_INNER)), mem_spec, mem_spec,
                  lay((MEM_INNER, D)), pl.BlockSpec((1, D), lambda b, t: (0, 0))],
        out_specs=tok(D),
        out_shape=jax.ShapeDtypeStruct((B, S, D), F32),
        scratch_shapes=[pltpu.VMEM((D_MIX, D), BF16), pltpu.VMEM((D, MEM_INNER), BF16),
                        pltpu.VMEM((MEM_INNER, D), BF16)],
        compiler_params=pltpu.CompilerParams(
            dimension_semantics=("arbitrary", "arbitrary"),
            vmem_limit_bytes=VMEM_LIMIT_BYTES),
        name="post",
    )(h, og, oc, om, p["w_out"], p["norm_xattn"], p["mem_wq"], memk, memv, p["mem_wo"],
      norm_final)


def _pad_cols(w, width):
    return jnp.pad(w, [(0, 0)] * (w.ndim - 1) + [(0, width - w.shape[-1])])


def _pack_w_in_kernel(wt_ref, o_ref):
    cols = wt_ref.shape[2]
    o_gv = 2 * GLA_QK
    o_glr = o_gv + GLA_WIDTH
    o_gg = o_glr + GLA_GATE_RANK
    o_cc = o_gg + GLA_WIDTH
    o_kr = o_cc + 4 * CONV_WIDTH + MLA_Q_RANK + MLA_KV_RANK
    o_mg = o_kr + MLA_ROPE
    zeros = lambda n: jnp.zeros((n, cols), F32)

    def rows(a, b):
        return wt_ref[0, a:b, :]

    pieces = [
        (SEG_Q, [rows(0, GLA_QK), zeros(GLA_QK_PAD - GLA_QK)]),
        (SEG_K, [rows(GLA_QK, o_gv), zeros(GLA_QK_PAD - GLA_QK)]),
        (SEG_V, [rows(o_gv, o_glr)]),
        (SEG_GG, [rows(o_gg, o_cc)]),
        (SEG_CC, [rows(o_cc, o_kr)]),
        (SEG_MG, [rows(o_mg, o_mg + MLA_WIDTH)]),
        (SEG_KR, [zeros(KR_ROPE_OFF), rows(o_kr, o_mg), rows(o_glr, o_gg),
                  zeros(MLA_SLOT - KR_GATE_OFF - GLA_GATE_RANK)]),
    ]
    for off, parts in pieces:
        blk = parts[0] if len(parts) == 1 else jnp.concatenate(parts, axis=0)
        o_ref[0, :, off:off + blk.shape[0]] = blk.T.astype(BF16)


def _pack_w_in(w_in):
    wt = jnp.swapaxes(w_in, 1, 2)
    L, W, D = wt.shape
    cols = 256
    return pl.pallas_call(
        _pack_w_in_kernel,
        grid=(L, D // cols),
        in_specs=[pl.BlockSpec((1, W, cols), lambda l, c: (l, 0, c))],
        out_specs=pl.BlockSpec((1, cols, IN_PAD), lambda l, c: (l, c, 0)),
        out_shape=jax.ShapeDtypeStruct((L, D, IN_PAD), BF16),
        compiler_params=pltpu.CompilerParams(
            dimension_semantics=("arbitrary", "arbitrary"),
            vmem_limit_bytes=VMEM_LIMIT_BYTES),
        name="pack_w_in",
    )(wt)


def _prep_params(norm_mix, w_in, gla_w_gate, gla_b_gate, gla_norm, conv_w, mla_q_norm, mla_w_uq,
                 mla_kv_norm, mla_w_ukv, w_out, norm_xattn, mem_wq, mem_wo):
    L = w_in.shape[0]
    wg = jnp.zeros((L, MLA_SLOT, GLA_QK_PAD), F32)
    wg = wg.at[:, KR_GATE_OFF:KR_GATE_OFF + GLA_GATE_RANK, :GLA_QK].set(gla_w_gate)
    w_uq = mla_w_uq.reshape(L, MLA_Q_RANK, MLA_HEADS, MLA_NOPE + MLA_ROPE)
    half = lambda lo: _pad_cols(
        w_uq[..., lo:lo + ROPE_HALF].reshape(L, MLA_Q_RANK, MLA_HEADS * ROPE_HALF), MLA_SLOT)
    w_uq = jnp.concatenate(
        [w_uq[..., :MLA_NOPE].reshape(L, MLA_Q_RANK, MLA_Q_NOPE_W), half(MLA_NOPE),
         half(MLA_NOPE + ROPE_HALF)], axis=-1)
    w_ukv = mla_w_ukv.reshape(L, MLA_KV_RANK, MLA_HEADS, MLA_NOPE + MLA_V)
    w_uk = _pad_cols(w_ukv[..., :MLA_NOPE], MLA_SLOT).reshape(L, MLA_KV_RANK, MLA_HEADS * MLA_SLOT)
    w_uv = w_ukv[..., MLA_NOPE:].reshape(L, MLA_KV_RANK, MLA_WIDTH)
    return {
        "norm_mix": norm_mix[:, None, :],
        "w_in": _pack_w_in(w_in),
        "wg": wg.astype(BF16),
        "bg": _pad_cols(gla_b_gate, GLA_QK_PAD)[:, None, :],
        "gla_norm": jnp.tile(gla_norm, (1, GLA_HEADS))[:, None, :],
        "conv_w": conv_w,
        "q_norm": mla_q_norm[:, None, :],
        "w_uq": w_uq.astype(BF16),
        "kv_norm": mla_kv_norm[:, None, :],
        "w_ukv": jnp.concatenate([w_uk, w_uv], axis=-1).astype(BF16),
        "w_out": w_out,
        "norm_xattn": norm_xattn[:, None, :],
        "mem_wq": mem_wq,
        "mem_wo": mem_wo,
    }


def _rope_tables(positions):
    inv_freq = 1.0 / (ROPE_BASE ** (jnp.arange(0, MLA_ROPE, 2, dtype=F32) / MLA_ROPE))
    ang = positions.astype(F32)[:, None, :] * inv_freq[None, :, None]
    return jnp.cos(ang), jnp.sin(ang)


def _pick_tile(S, want):
    t = min(S, want)
    assert S % t == 0 and t % GLA_CHUNK == 0
    return t


def kernel(x, mem, positions, norm_mix, w_in, gla_w_gate, gla_b_gate, gla_norm, conv_w, mla_q_norm,
           mla_w_uq, mla_kv_norm, mla_w_ukv, w_out, norm_xattn, norm_mem, mem_wq, mem_wk, mem_wv,
           mem_wo, norm_final):
    depth = w_in.shape[0]
    S = x.shape[1]
    attn_tile = _pick_tile(S, 512)
    tile = _pick_tile(S, PROJ_TILE)
    p = _prep_params(norm_mix, w_in, gla_w_gate, gla_b_gate, gla_norm, conv_w, mla_q_norm,
                     mla_w_uq, mla_kv_norm, mla_w_ukv, w_out, norm_xattn, mem_wq, mem_wo)
    tables = _rope_tables(positions)
    memk, memv = _memkv_call(mem, norm_mem[:, None, :], mem_wk, mem_wv)
    nf = norm_final[None, :]
    h = x
    for l in range(depth):
        og, oc, mq, mk, mv, mg = _proj_call(h, l, p, tables, tile, attn_tile)
        om = _mla_call(mq, mk, mv, mg, attn_tile)
        h = _post_call(h, og, oc, om, memk, memv, l, p, nf, tile, l == depth - 1)
    return h
```

```python
import functools
import math

import jax
import jax.numpy as jnp
from jax import lax
from jax.experimental import pallas as pl
from jax.experimental.pallas import tpu as pltpu

F32 = jnp.float32
BF16 = jnp.bfloat16

D_MODEL = 1024
EPS = 1e-6

GLA_HEADS = 4
GLA_DV = 96
GLA_DK = 48
GLA_QK = GLA_HEADS * GLA_DK
GLA_WIDTH = GLA_HEADS * GLA_DV
GLA_GATE_RANK = 16
GLA_TAU = 16.0
GLA_CHUNK = 64
GLA_QK_PAD = 256

CONV_WIDTH = 256
CONV_K = 3

MLA_HEADS = 6
MLA_NOPE = 64
MLA_ROPE = 32
MLA_V = 64
MLA_Q_RANK = 256
MLA_KV_RANK = 256
MLA_WIDTH = MLA_HEADS * MLA_V
MLA_SLOT = 128
MLA_Q_NOPE_W = MLA_HEADS * MLA_NOPE
MLA_ACC_ROWS = MLA_V + 16
MLA_AHEAD = 2
MLA_UNROLL = 4
ROPE_BASE = 10000.0
ROPE_HALF = MLA_ROPE // 2

D_MIX = GLA_WIDTH + CONV_WIDTH + MLA_WIDTH

MEM_HEADS = 4
MEM_HEAD_DIM = 128
MEM_INNER = MEM_HEADS * MEM_HEAD_DIM

LOG2E = math.log2(math.e)
MASK_VALUE = -1e30

SEG_Q = 0
SEG_K = SEG_Q + GLA_QK_PAD
SEG_V = SEG_K + GLA_QK_PAD
SEG_GG = SEG_V + GLA_WIDTH
SEG_CC = SEG_GG + GLA_WIDTH
SEG_CB = SEG_CC + CONV_WIDTH
SEG_CH = SEG_CB + CONV_WIDTH
SEG_CG = SEG_CH + CONV_WIDTH
SEG_CQ = SEG_CG + CONV_WIDTH
SEG_CKV = SEG_CQ + MLA_Q_RANK
SEG_MG = SEG_CKV + MLA_KV_RANK
SEG_KR = SEG_MG + MLA_WIDTH
IN_PAD = SEG_KR + MLA_SLOT
KR_ROPE_OFF = MLA_NOPE
KR_GATE_OFF = MLA_NOPE + MLA_ROPE

VMEM_LIMIT_BYTES = 56 * 1024 * 1024
PROJ_TILE = 1024
PIECE_N = 256
GLA_FILL_STAGES = 4


def _silu(x):
    return x * (1.0 / (1.0 + jnp.exp(-x)))


def _rms(x, g):
    return x * lax.rsqrt(jnp.mean(x * x, axis=-1, keepdims=True) + EPS) * g


def _dot(a, b):
    return jnp.dot(a, b, preferred_element_type=F32)


def _dot_nt(a, b):
    return lax.dot_general(a, b, (((1,), (1,)), ((), ())), preferred_element_type=F32)


def _dot_tn(a, b):
    return lax.dot_general(a, b, (((0,), (0,)), ((), ())), preferred_element_type=F32)


def _gla_tile(q, k, la, v, gate, gn, st_ref, og_ref, fill):
    C = GLA_CHUNK
    tile = q.shape[0]
    lane_k = lax.broadcasted_iota(jnp.int32, (1, GLA_QK_PAD), 1)
    lane_v = lax.broadcasted_iota(jnp.int32, (1, GLA_WIDTH), 1)
    kmask = [(lane_k >= h * GLA_DK) & (lane_k < (h + 1) * GLA_DK) for h in range(GLA_HEADS)]
    vmask = [(lane_v >= h * GLA_DV) & (lane_v < (h + 1) * GLA_DV) for h in range(GLA_HEADS)]
    row = lax.broadcasted_iota(jnp.int32, (C, 2 * C), 0)
    col = lax.broadcasted_iota(jnp.int32, (C, 2 * C), 1)
    cum2 = (row >= (col & (C - 1))).astype(BF16)
    hrow = lax.broadcasted_iota(jnp.int32, (C, GLA_HEADS * C), 0)
    hcol = lax.broadcasted_iota(jnp.int32, (C, GLA_HEADS * C), 1)
    tril_heads = hrow >= (hcol & (C - 1))
    srow = lax.broadcasted_iota(jnp.int32, (GLA_WIDTH, GLA_QK_PAD), 0)
    slane = lax.broadcasted_iota(jnp.int32, (GLA_WIDTH, GLA_QK_PAD), 1)
    same_head = None
    for h in range(GLA_HEADS):
        blk = ((srow >= h * GLA_DV) & (srow < (h + 1) * GLA_DV)
               & (slane >= h * GLA_DK) & (slane < (h + 1) * GLA_DK))
        same_head = blk if same_head is None else (same_head | blk)

    st = st_ref[...]
    for c in range(tile // C):
        sl = slice(c * C, (c + 1) * C)
        qc, kc, lac, vc = q[sl], k[sl], la[sl], v[sl]
        la_hi = lac.astype(BF16)
        la_lo = (lac - la_hi.astype(F32)).astype(BF16)
        b = _dot(cum2, jnp.concatenate([la_hi, la_lo], axis=0))
        fill()
        b_last = b[C - 1:C, :]
        q_dec = (qc * (GLA_DK ** -0.5) * jnp.exp(b)).astype(BF16)
        k_inv = kc * jnp.exp(-b)
        k_end = (kc * jnp.exp(b_last - b)).astype(BF16)
        decay = jnp.exp(b_last)
        k_heads = jnp.concatenate(
            [jnp.where(kmask[h], k_inv, 0.0) for h in range(GLA_HEADS)], axis=0).astype(BF16)
        a = _dot_nt(q_dec, k_heads)
        fill()
        a = jnp.where(tril_heads, a, 0.0).astype(BF16)
        v_heads = jnp.concatenate(
            [jnp.where(vmask[h], vc, 0.0) for h in range(GLA_HEADS)], axis=0).astype(BF16)
        o = _dot(a, v_heads) + _dot_nt(q_dec, st.astype(BF16))
        fill()
        upd = _dot_tn(vc.astype(BF16), k_end)
        fill()
        st = st * decay + jnp.where(same_head, upd, 0.0)
        o2 = o * o
        ms = jnp.zeros_like(o)
        for h in range(GLA_HEADS):
            mh = jnp.sum(jnp.where(vmask[h], o2, 0.0), axis=1, keepdims=True) * (1.0 / GLA_DV)
            ms = jnp.where(vmask[h], mh, ms)
        y = o * lax.rsqrt(ms + EPS) * gn
        og_ref[0, sl, :] = (y * gate[sl]).astype(BF16)
    st_ref[...] = st


def _proj_kernel(h_ref, g_ref, win_ref, wg_ref, bg_ref, gn_ref, convw_ref, qn_ref, wuq_ref,
                 kvn_ref, wukv_ref, cos_ref, sin_ref,
                 og_ref, oconv_ref, mq_ref, mk_ref, mv_ref, mg_ref,
                 ubuf_ref, st_ref, *, tile, attn_tile, q_scale):
    t = pl.program_id(1)

    @pl.when(t == 0)
    def _():
        ubuf_ref[0:8, :] = jnp.zeros((8, CONV_WIDTH), F32)
        st_ref[...] = jnp.zeros(st_ref.shape, F32)

    x = h_ref[0]
    xn = _rms(x, g_ref[...]).astype(BF16)

    def seg(a, b):
        return _dot(xn, win_ref[:, a:b])

    val = {}
    work = []

    def queue_segment(name, a, b):
        parts = []
        for n0 in range(a, b, PIECE_N):
            work.append((4.0, lambda n0=n0: parts.append(seg(n0, min(n0 + PIECE_N, b)))))
        work.append((0.0, lambda: val.update({name: jnp.concatenate(parts, axis=1)})))

    def q_up():
        cqn = _rms(val["cq"], qn_ref[...]).astype(BF16)
        val["qn"] = _dot(cqn, wuq_ref[:, 0:MLA_Q_NOPE_W]) * q_scale
        val["qr"] = _dot(cqn, wuq_ref[:, MLA_Q_NOPE_W:])

    kv_parts = []
    kv_width = MLA_HEADS * MLA_SLOT + MLA_WIDTH

    def kv_piece(n0):
        if n0 == 0:
            val["ckvn"] = _rms(val["ckv"], kvn_ref[...]).astype(BF16)
        kv_parts.append(_dot(val["ckvn"], wukv_ref[:, n0:min(n0 + PIECE_N, kv_width)]))

    queue_segment("cq", SEG_CQ, SEG_CKV)
    queue_segment("ckv", SEG_CKV, SEG_MG)
    work.append((2.5, q_up))
    for n0 in range(0, kv_width, PIECE_N):
        work.append((1.0, functools.partial(kv_piece, n0)))
    queue_segment("cc", SEG_CC, SEG_CB)
    queue_segment("ch", SEG_CH, SEG_CG)
    queue_segment("cb", SEG_CB, SEG_CH)
    queue_segment("cg", SEG_CG, SEG_CQ)
    queue_segment("mg", SEG_MG, SEG_KR)
    work.reverse()
    total_cost = sum(c for c, _ in work)
    slots = GLA_FILL_STAGES * (tile // GLA_CHUNK)
    progress = {"slot": 0, "cost": 0.0}

    def fill():
        progress["slot"] += 1
        target = total_cost * progress["slot"] / slots
        while work and progress["cost"] + 0.5 * work[-1][0] <= target:
            cost, thunk = work.pop()
            progress["cost"] += cost
            thunk()

    kr = seg(SEG_KR, IN_PAD)
    gq, gk, gv, gg = seg(SEG_Q, SEG_K), seg(SEG_K, SEG_V), seg(SEG_V, SEG_GG), seg(SEG_GG, SEG_CC)
    z = _dot(kr.astype(BF16), wg_ref[...]) + bg_ref[...]
    la = (jnp.minimum(z, 0.0) - jnp.log1p(jnp.exp(-jnp.abs(z)))) * (1.0 / GLA_TAU)
    _gla_tile(gq, gk, la, gv, _silu(gg), gn_ref[...], st_ref, og_ref, fill)
    while work:
        work.pop()[1]()

    u = val["cc"] * val["ch"]
    ubuf_ref[8:8 + tile, :] = u
    cw = convw_ref[...]
    conv = (cw[0:1, :] * ubuf_ref[6:6 + tile, :] + cw[1:2, :] * ubuf_ref[7:7 + tile, :]
            + cw[2:3, :] * u)
    oconv_ref[0] = (val["cb"] * conv * _silu(val["cg"])).astype(BF16)
    ubuf_ref[0:8, :] = u[tile - 8:tile, :]

    reps = MLA_SLOT // ROPE_HALF
    cos = jnp.concatenate([cos_ref[0]] * reps, axis=0).T
    sin = jnp.concatenate([sin_ref[0]] * reps, axis=0).T
    qn, qr = val["qn"], val["qr"]
    x1, x2 = qr[:, 0:MLA_SLOT], qr[:, MLA_SLOT:]
    cq, sq = cos * q_scale, sin * q_scale
    r1 = x1 * cq - x2 * sq
    r2 = x2 * cq + x1 * sq
    pad_rows = jnp.zeros((MLA_SLOT - MLA_NOPE - MLA_ROPE, attn_tile), BF16)
    for i in range(tile // attn_tile):
        rows = slice(i * attn_tile, (i + 1) * attn_tile)
        qnt = qn[rows].T.astype(BF16)
        r1t = r1[rows].T.astype(BF16)
        r2t = r2[rows].T.astype(BF16)
        for h in range(MLA_HEADS):
            base = h * MLA_SLOT
            mq_ref[0, i, base:base + MLA_NOPE, :] = qnt[h * MLA_NOPE:(h + 1) * MLA_NOPE]
            mq_ref[0, i, base + MLA_NOPE:base + MLA_NOPE + ROPE_HALF, :] = (
                r1t[h * ROPE_HALF:(h + 1) * ROPE_HALF])
            mq_ref[0, i, base + MLA_NOPE + ROPE_HALF:base + MLA_NOPE + MLA_ROPE, :] = (
                r2t[h * ROPE_HALF:(h + 1) * ROPE_HALF])
            mq_ref[0, i, base + MLA_NOPE + MLA_ROPE:base + MLA_SLOT, :] = pad_rows

    kvh = jnp.concatenate(kv_parts, axis=1)
    lane = lax.broadcasted_iota(jnp.int32, (1, MLA_SLOT), 1)
    first = (lane >= KR_ROPE_OFF) & (lane < KR_ROPE_OFF + ROPE_HALF)
    second = (lane >= KR_ROPE_OFF + ROPE_HALF) & (lane < KR_ROPE_OFF + MLA_ROPE)
    krope = (kr * jnp.where(first | second, cos, 0.0)
             + pltpu.roll(kr, ROPE_HALF, 1) * jnp.where(second, sin, 0.0)
             - pltpu.roll(kr, MLA_SLOT - ROPE_HALF, 1) * jnp.where(first, sin, 0.0))
    for h in range(MLA_HEADS):
        sl = slice(h * MLA_SLOT, (h + 1) * MLA_SLOT)
        mk_ref[0, :, sl] = (kvh[:, sl] + krope).astype(BF16)
    vals = kvh[:, MLA_HEADS * MLA_SLOT:]
    for i in range(tile // attn_tile):
        mv_ref[0, i] = vals[i * attn_tile:(i + 1) * attn_tile].T.astype(BF16)
    mg_ref[0] = _silu(val["mg"])


def _proj_call(h, l, p, tables, tile, attn_tile):
    B, S, D = h.shape
    nt = S // tile
    sub = tile // attn_tile
    q_scale = LOG2E / math.sqrt(MLA_NOPE + MLA_ROPE)

    def tok(w):
        return pl.BlockSpec((1, tile, w), lambda b, t: (b, t, 0))

    def lay(shape):
        return pl.BlockSpec((None,) + shape, lambda b, t: (l,) + (0,) * len(shape))

    in_specs = [
        tok(D),
        lay((1, D)),
        lay((D, IN_PAD)),
        lay((MLA_SLOT, GLA_QK_PAD)),
        lay((1, GLA_QK_PAD)),
        lay((1, GLA_WIDTH)),
        lay((CONV_K, CONV_WIDTH)),
        lay((1, MLA_Q_RANK)),
        lay((MLA_Q_RANK, MLA_Q_NOPE_W + 2 * MLA_SLOT)),
        lay((1, MLA_KV_RANK)),
        lay((MLA_KV_RANK, MLA_HEADS * MLA_SLOT + MLA_WIDTH)),
        pl.BlockSpec((1, ROPE_HALF, tile), lambda b, t: (b, 0, t)),
        pl.BlockSpec((1, ROPE_HALF, tile), lambda b, t: (b, 0, t)),
    ]

    def tok_out(w, dt):
        return tok(w), jax.ShapeDtypeStruct((B, S, w), dt)

    def tile_t_out(w, dt):
        return (pl.BlockSpec((1, sub, w, attn_tile), lambda b, t: (b, t, 0, 0)),
                jax.ShapeDtypeStruct((B, S // attn_tile, w, attn_tile), dt))

    outs = [
        tok_out(GLA_WIDTH, BF16), tok_out(CONV_WIDTH, BF16),
        tile_t_out(MLA_HEADS * MLA_SLOT, BF16), tok_out(MLA_HEADS * MLA_SLOT, BF16),
        tile_t_out(MLA_WIDTH, BF16), tok_out(MLA_WIDTH, F32),
    ]
    return pl.pallas_call(
        functools.partial(_proj_kernel, tile=tile, attn_tile=attn_tile, q_scale=q_scale),
        grid=(B, nt),
        in_specs=in_specs,
        out_specs=[spec for spec, _ in outs],
        out_shape=[shape for _, shape in outs],
        scratch_shapes=[pltpu.VMEM((tile + 8, CONV_WIDTH), F32),
                        pltpu.VMEM((GLA_WIDTH, GLA_QK_PAD), F32)],
        compiler_params=pltpu.CompilerParams(
            dimension_semantics=("arbitrary", "arbitrary"),
            vmem_limit_bytes=VMEM_LIMIT_BYTES),
        name="proj",
    )(h, p["norm_mix"], p["w_in"], p["wg"], p["bg"], p["gla_norm"], p["conv_w"], p["q_norm"],
      p["w_uq"], p["kv_norm"], p["w_ukv"], *tables)


def _mla_kernel(qt_ref, k_ref, vt_ref, mg_ref, o_ref, m_ref, acc_ref, s_ref, *, tile):
    qi = pl.program_id(1)
    nt = pl.num_programs(1)
    half = tile // 2
    krow = lax.broadcasted_iota(jnp.int32, (half, tile), 0)
    qcol = lax.broadcasted_iota(jnp.int32, (half, tile), 1)
    causal_top = krow <= qcol
    causal_sq = causal_top[:, 0:half]
    ones_rows = jnp.ones((MLA_ACC_ROWS - MLA_V, tile), BF16)

    m_ref[...] = jnp.full(m_ref.shape, MASK_VALUE, F32)
    acc_ref[...] = jnp.zeros(acc_ref.shape, F32)

    def head(e):
        return slice(e * MLA_SLOT, (e + 1) * MLA_SLOT)

    def scores(j, e, q_tile):
        rows = pl.ds(pl.multiple_of(j * tile, tile), tile)
        s_ref[e] = _dot(k_ref[0, rows, head(e)], qt_ref[0, q_tile, head(e), :])

    def scores_diag(e):
        top = pl.ds(pl.multiple_of(qi * tile, tile), half)
        bot = pl.ds(pl.multiple_of(qi * tile + half, half), half)
        s_ref[e, 0:half, :] = _dot(k_ref[0, top, head(e)], qt_ref[0, qi, head(e), :])
        s_ref[e, half:tile, half:tile] = _dot(k_ref[0, bot, head(e)],
                                              qt_ref[0, qi, head(e), half:tile])

    def values(j, e, lo, hi):
        vt = vt_ref[0, j, e * MLA_V:(e + 1) * MLA_V, lo:hi]
        return jnp.concatenate([vt, ones_rows[:, lo:hi]], axis=0)

    def consume(j, e):
        s = s_ref[e]
        m_prev = m_ref[e]
        m_new = jnp.maximum(m_prev, jnp.max(s, axis=0, keepdims=True))
        alpha = jnp.exp2(m_prev - m_new)
        p = jnp.exp2(s - m_new).astype(BF16)
        acc_ref[e] = alpha * acc_ref[e] + _dot(values(j, e, 0, tile), p)
        m_ref[e] = m_new

    def consume_diag(e):
        s = jnp.where(causal_top, s_ref[e, 0:half, :], MASK_VALUE)
        m_prev = m_ref[e]
        m_new = jnp.maximum(m_prev, jnp.max(s, axis=0, keepdims=True))
        p = jnp.exp2(s - m_new).astype(BF16)
        acc = jnp.exp2(m_prev - m_new) * acc_ref[e] + _dot(values(qi, e, 0, half), p)
        s2 = jnp.where(causal_sq, s_ref[e, half:tile, half:tile], MASK_VALUE)
        m2_prev = m_new[:, half:tile]
        m2_new = jnp.maximum(m2_prev, jnp.max(s2, axis=0, keepdims=True))
        p2 = jnp.exp2(s2 - m2_new).astype(BF16)
        acc2 = (jnp.exp2(m2_prev - m2_new) * acc[:, half:tile]
                + _dot(values(qi, e, half, tile), p2))
        acc = jnp.concatenate([acc[:, 0:half], acc2], axis=1)
        return acc[0:MLA_V, :] / acc[MLA_V:MLA_V + 1, :]

    nxt = jnp.minimum(qi + 1, nt - 1)

    @pl.when(qi == 0)
    def _():
        for e in range(MLA_AHEAD):
            scores(0, e, 0)

    def blocks(j0, count):
        for t in range(count * MLA_HEADS):
            ahead = t + MLA_AHEAD
            scores(j0 + ahead // MLA_HEADS, ahead % MLA_HEADS, qi)
            consume(j0 + t // MLA_HEADS, t % MLA_HEADS)

    def body(i, carry):
        blocks(MLA_UNROLL * i, MLA_UNROLL)
        return carry

    lax.fori_loop(0, qi // MLA_UNROLL, body, 0)
    rem = qi % MLA_UNROLL
    size = MLA_UNROLL // 2
    while size >= 1:
        @pl.when((rem & size) != 0)
        def _(size=size):
            blocks(qi - (rem & (2 * size - 1)), size)
        size //= 2

    outs = []
    for e in range(MLA_HEADS):
        ahead = e + MLA_AHEAD
        if ahead < MLA_HEADS:
            scores_diag(ahead)
        else:
            scores(0, ahead - MLA_HEADS, nxt)
        outs.append(consume_diag(e))
    ot = jnp.concatenate(outs, axis=0)
    o_ref[0] = (ot.T * mg_ref[0]).astype(BF16)


def _mla_call(mqt, mk, mvt, mg, tile):
    B, nt, _, _ = mqt.shape
    S = nt * tile
    return pl.pallas_call(
        functools.partial(_mla_kernel, tile=tile),
        grid=(B, nt),
        in_specs=[
            pl.BlockSpec((1, nt, MLA_HEADS * MLA_SLOT, tile), lambda b, i: (b, 0, 0, 0)),
            pl.BlockSpec((1, S, MLA_HEADS * MLA_SLOT), lambda b, i: (b, 0, 0)),
            pl.BlockSpec((1, nt, MLA_WIDTH, tile), lambda b, i: (b, 0, 0, 0)),
            pl.BlockSpec((1, tile, MLA_WIDTH), lambda b, i: (b, i, 0)),
        ],
        out_specs=pl.BlockSpec((1, tile, MLA_WIDTH), lambda b, i: (b, i, 0)),
        out_shape=jax.ShapeDtypeStruct((B, S, MLA_WIDTH), BF16),
        scratch_shapes=[pltpu.VMEM((MLA_HEADS, 1, tile), F32),
                        pltpu.VMEM((MLA_HEADS, MLA_ACC_ROWS, tile), F32),
                        pltpu.VMEM((MLA_HEADS, tile, tile), F32)],
        compiler_params=pltpu.CompilerParams(
            dimension_semantics=("arbitrary", "arbitrary"),
            vmem_limit_bytes=VMEM_LIMIT_BYTES),
        name="mla",
    )(mqt, mk, mvt, mg)


def _memkv_kernel(mem_ref, g_ref, wk_ref, wv_ref, k_ref, v_ref):
    B, M, D = mem_ref.shape
    memn = _rms(mem_ref[...].reshape(B * M, D), g_ref[...]).astype(BF16)
    k_ref[...] = _dot(memn, wk_ref[...].astype(BF16)).astype(BF16).reshape(B, M, MEM_INNER)
    v_ref[...] = _dot(memn, wv_ref[...].astype(BF16)).astype(BF16).reshape(B, M, MEM_INNER)


def _memkv_call(mem, norm_mem, wk, wv):
    B, M, D = mem.shape
    L = wk.shape[0]
    kv_shape = jax.ShapeDtypeStruct((L, B, M, MEM_INNER), BF16)
    kv_spec = pl.BlockSpec((None, B, M, MEM_INNER), lambda l: (l, 0, 0, 0))
    return pl.pallas_call(
        _memkv_kernel,
        grid=(L,),
        in_specs=[pl.BlockSpec((B, M, D), lambda l: (0, 0, 0)),
                  pl.BlockSpec((None, 1, D), lambda l: (l, 0, 0)),
                  pl.BlockSpec((None, D, MEM_INNER), lambda l: (l, 0, 0)),
                  pl.BlockSpec((None, D, MEM_INNER), lambda l: (l, 0, 0))],
        out_specs=[kv_spec, kv_spec],
        out_shape=[kv_shape, kv_shape],
        compiler_params=pltpu.CompilerParams(
            dimension_semantics=("arbitrary",),
            vmem_limit_bytes=VMEM_LIMIT_BYTES),
        name="memkv",
    )(mem, norm_mem, wk, wv)


def _post_kernel(h_ref, og_ref, oc_ref, om_ref, wout_ref, gx_ref, wq_ref, mk_ref, mv_ref,
                 wo_ref, gf_ref, o_ref, wout_bf, wq_bf, wo_bf, *, final_norm):
    @pl.when((pl.program_id(0) == 0) & (pl.program_id(1) == 0))
    def _():
        wout_bf[...] = wout_ref[...].astype(BF16)
        wq_bf[...] = wq_ref[...].astype(BF16)
        wo_bf[...] = wo_ref[...].astype(BF16)

    x = jnp.concatenate([og_ref[0], oc_ref[0], om_ref[0]], axis=1)
    h1 = h_ref[0] + _dot(x, wout_bf[...])
    hn = _rms(h1, gx_ref[...]).astype(BF16)
    q = (_dot(hn, wq_bf[...]) * (LOG2E / math.sqrt(MEM_HEAD_DIM))).astype(BF16)
    ones = jnp.ones((mv_ref.shape[0], MEM_HEAD_DIM), BF16)
    heads = []
    for hd in range(MEM_HEADS):
        sl = slice(hd * MEM_HEAD_DIM, (hd + 1) * MEM_HEAD_DIM)
        s = _dot_nt(q[:, sl], mk_ref[:, sl])
        p = jnp.exp2(s - jnp.max(s, axis=1, keepdims=True)).astype(BF16)
        nd = _dot(p, jnp.concatenate([mv_ref[:, sl], ones], axis=1))
        heads.append((nd[:, :MEM_HEAD_DIM] / nd[:, MEM_HEAD_DIM:]).astype(BF16))
    o = jnp.concatenate(heads, axis=1)
    h2 = h1 + _dot(o, wo_bf[...])
    if final_norm:
        h2 = _rms(h2, gf_ref[...])
    o_ref[0] = h2


def _post_call(h, og, oc, om, memk, memv, l, p, norm_final, tile, final_norm):
    B, S, D = h.shape
    nt = S // tile
    M = memk.shape[2]

    def tok(w):
        return pl.BlockSpec((1, tile, w), lambda b, t: (b, t, 0))

    def lay(shape):
        return pl.BlockSpec((None,) + shape, lambda b, t: (l,) + (0,) * len(shape))

    mem_spec = pl.BlockSpec((None, None, M, MEM_INNER), lambda b, t: (l, b, 0, 0))
    return pl.pallas_call(
        functools.partial(_post_kernel, final_norm=final_norm),
        grid=(B, nt),
        in_specs=[tok(D), tok(GLA_WIDTH), tok(CONV_WIDTH), tok(MLA_WIDTH),
                  lay((D_MIX, D)), lay((1, D)), lay((D, MEM_INNER)), mem_spec, mem_spec,
                  lay((MEM_INNER, D)), pl.BlockSpec((1, D), lambda b, t: (0, 0))],
        out_specs=tok(D),
        out_shape=jax.ShapeDtypeStruct((B, S, D), F32),
        scratch_shapes=[pltpu.VMEM((D_MIX, D), BF16), pltpu.VMEM((D, MEM_INNER), BF16),
                        pltpu.VMEM((MEM_INNER, D), BF16)],
        compiler_params=pltpu.CompilerParams(
            dimension_semantics=("arbitrary", "arbitrary"),
            vmem_limit_bytes=VMEM_LIMIT_BYTES),
        name="post",
    )(h, og, oc, om, p["w_out"], p["norm_xattn"], p["mem_wq"], memk, memv, p["mem_wo"],
      norm_final)


def _pad_cols(w, width):
    return jnp.pad(w, [(0, 0)] * (w.ndim - 1) + [(0, width - w.shape[-1])])


def _pack_w_in_kernel(wt_ref, o_ref):
    cols = wt_ref.shape[2]
    o_gv = 2 * GLA_QK
    o_glr = o_gv + GLA_WIDTH
    o_gg = o_glr + GLA_GATE_RANK
    o_cc = o_gg + GLA_WIDTH
    o_kr = o_cc + 4 * CONV_WIDTH + MLA_Q_RANK + MLA_KV_RANK
    o_mg = o_kr + MLA_ROPE
    zeros = lambda n: jnp.zeros((n, cols), F32)

    def rows(a, b):
        return wt_ref[0, a:b, :]

    pieces = [
        (SEG_Q, [rows(0, GLA_QK), zeros(GLA_QK_PAD - GLA_QK)]),
        (SEG_K, [rows(GLA_QK, o_gv), zeros(GLA_QK_PAD - GLA_QK)]),
        (SEG_V, [rows(o_gv, o_glr)]),
        (SEG_GG, [rows(o_gg, o_cc)]),
        (SEG_CC, [rows(o_cc, o_kr)]),
        (SEG_MG, [rows(o_mg, o_mg + MLA_WIDTH)]),
        (SEG_KR, [zeros(KR_ROPE_OFF), rows(o_kr, o_mg), rows(o_glr, o_gg),
                  zeros(MLA_SLOT - KR_GATE_OFF - GLA_GATE_RANK)]),
    ]
    for off, parts in pieces:
        blk = parts[0] if len(parts) == 1 else jnp.concatenate(parts, axis=0)
        o_ref[0, :, off:off + blk.shape[0]] = blk.T.astype(BF16)


def _pack_w_in(w_in):
    wt = jnp.swapaxes(w_in, 1, 2)
    L, W, D = wt.shape
    cols = 512
    return pl.pallas_call(
        _pack_w_in_kernel,
        grid=(L, D // cols),
        in_specs=[pl.BlockSpec((1, W, cols), lambda l, c: (l, 0, c))],
        out_specs=pl.BlockSpec((1, cols, IN_PAD), lambda l, c: (l, c, 0)),
        out_shape=jax.ShapeDtypeStruct((L, D, IN_PAD), BF16),
        compiler_params=pltpu.CompilerParams(
            dimension_semantics=("arbitrary", "arbitrary"),
            vmem_limit_bytes=VMEM_LIMIT_BYTES),
        name="pack_w_in",
    )(wt)


def _prep_params(norm_mix, w_in, gla_w_gate, gla_b_gate, gla_norm, conv_w, mla_q_norm, mla_w_uq,
                 mla_kv_norm, mla_w_ukv, w_out, norm_xattn, mem_wq, mem_wo):
    L = w_in.shape[0]
    wg = jnp.zeros((L, MLA_SLOT, GLA_QK_PAD), F32)
    wg = wg.at[:, KR_GATE_OFF:KR_GATE_OFF + GLA_GATE_RANK, :GLA_QK].set(gla_w_gate)
    w_uq = mla_w_uq.reshape(L, MLA_Q_RANK, MLA_HEADS, MLA_NOPE + MLA_ROPE)
    half = lambda lo: _pad_cols(
        w_uq[..., lo:lo + ROPE_HALF].reshape(L, MLA_Q_RANK, MLA_HEADS * ROPE_HALF), MLA_SLOT)
    w_uq = jnp.concatenate(
        [w_uq[..., :MLA_NOPE].reshape(L, MLA_Q_RANK, MLA_Q_NOPE_W), half(MLA_NOPE),
         half(MLA_NOPE + ROPE_HALF)], axis=-1)
    w_ukv = mla_w_ukv.reshape(L, MLA_KV_RANK, MLA_HEADS, MLA_NOPE + MLA_V)
    w_uk = _pad_cols(w_ukv[..., :MLA_NOPE], MLA_SLOT).reshape(L, MLA_KV_RANK, MLA_HEADS * MLA_SLOT)
    w_uv = w_ukv[..., MLA_NOPE:].reshape(L, MLA_KV_RANK, MLA_WIDTH)
    return {
        "norm_mix": norm_mix[:, None, :],
        "w_in": _pack_w_in(w_in),
        "wg": wg.astype(BF16),
        "bg": _pad_cols(gla_b_gate, GLA_QK_PAD)[:, None, :],
        "gla_norm": jnp.tile(gla_norm, (1, GLA_HEADS))[:, None, :],
        "conv_w": conv_w,
        "q_norm": mla_q_norm[:, None, :],
        "w_uq": w_uq.astype(BF16),
        "kv_norm": mla_kv_norm[:, None, :],
        "w_ukv": jnp.concatenate([w_uk, w_uv], axis=-1).astype(BF16),
        "w_out": w_out,
        "norm_xattn": norm_xattn[:, None, :],
        "mem_wq": mem_wq,
        "mem_wo": mem_wo,
    }


def _rope_tables(positions):
    inv_freq = 1.0 / (ROPE_BASE ** (jnp.arange(0, MLA_ROPE, 2, dtype=F32) / MLA_ROPE))
    ang = positions.astype(F32)[:, None, :] * inv_freq[None, :, None]
    return jnp.cos(ang), jnp.sin(ang)


def _pick_tile(S, want):
    t = min(S, want)
    assert S % t == 0 and t % GLA_CHUNK == 0
    return t


def kernel(x, mem, positions, norm_mix, w_in, gla_w_gate, gla_b_gate, gla_norm, conv_w, mla_q_norm,
           mla_w_uq, mla_kv_norm, mla_w_ukv, w_out, norm_xattn, norm_mem, mem_wq, mem_wk, mem_wv,
           mem_wo, norm_final):
    depth = w_in.shape[0]
    S = x.shape[1]
    attn_tile = _pick_tile(S, 512)
    tile = _pick_tile(S, PROJ_TILE)
    p = _prep_params(norm_mix, w_in, gla_w_gate, gla_b_gate, gla_norm, conv_w, mla_q_norm,
                     mla_w_uq, mla_kv_norm, mla_w_ukv, w_out, norm_xattn, mem_wq, mem_wo)
    tables = _rope_tables(positions)
    memk, memv = _memkv_call(mem, norm_mem[:, None, :], mem_wk, mem_wv)
    nf = norm_final[None, :]
    h = x
    for l in range(depth):
        og, oc, mq, mk, mv, mg = _proj_call(h, l, p, tables, tile, attn_tile)
        om = _mla_call(mq, mk, mv, mg, attn_tile)
        h = _post_call(h, og, oc, om, memk, memv, l, p, nf, tile, l == depth - 1)
    return h
```

```python
import functools
import math

import jax
import jax.numpy as jnp
from jax import lax
from jax.experimental import pallas as pl
from jax.experimental.pallas import tpu as pltpu

F32 = jnp.float32
BF16 = jnp.bfloat16

D_MODEL = 1024
EPS = 1e-6

GLA_HEADS = 4
GLA_DV = 96
GLA_DK = 48
GLA_QK = GLA_HEADS * GLA_DK
GLA_WIDTH = GLA_HEADS * GLA_DV
GLA_GATE_RANK = 16
GLA_TAU = 16.0
GLA_CHUNK = 64
GLA_QK_PAD = 256

CONV_WIDTH = 256
CONV_K = 3

MLA_HEADS = 6
MLA_NOPE = 64
MLA_ROPE = 32
MLA_V = 64
MLA_Q_RANK = 256
MLA_KV_RANK = 256
MLA_WIDTH = MLA_HEADS * MLA_V
MLA_SLOT = 128
MLA_Q_NOPE_W = MLA_HEADS * MLA_NOPE
MLA_ACC_ROWS = MLA_V + 16
MLA_AHEAD = 2
MLA_UNROLL = 4
ROPE_BASE = 10000.0
ROPE_HALF = MLA_ROPE // 2

D_MIX = GLA_WIDTH + CONV_WIDTH + MLA_WIDTH

MEM_HEADS = 4
MEM_HEAD_DIM = 128
MEM_INNER = MEM_HEADS * MEM_HEAD_DIM

LOG2E = math.log2(math.e)
MASK_VALUE = -1e30

SEG_Q = 0
SEG_K = SEG_Q + GLA_QK_PAD
SEG_V = SEG_K + GLA_QK_PAD
SEG_GG = SEG_V + GLA_WIDTH
SEG_CC = SEG_GG + GLA_WIDTH
SEG_CB = SEG_CC + CONV_WIDTH
SEG_CH = SEG_CB + CONV_WIDTH
SEG_CG = SEG_CH + CONV_WIDTH
SEG_CQ = SEG_CG + CONV_WIDTH
SEG_CKV = SEG_CQ + MLA_Q_RANK
SEG_MG = SEG_CKV + MLA_KV_RANK
SEG_KR = SEG_MG + MLA_WIDTH
IN_PAD = SEG_KR + MLA_SLOT
KR_ROPE_OFF = MLA_NOPE
KR_GATE_OFF = MLA_NOPE + MLA_ROPE

VMEM_LIMIT_BYTES = 56 * 1024 * 1024
PROJ_TILE = 1024
PIECE_N = 256
GLA_FILL_STAGES = 4


def _silu(x):
    return x * (1.0 / (1.0 + jnp.exp(-x)))


def _rms(x, g):
    return x * lax.rsqrt(jnp.mean(x * x, axis=-1, keepdims=True) + EPS) * g


def _dot(a, b):
    return jnp.dot(a, b, preferred_element_type=F32)


def _dot_nt(a, b):
    return lax.dot_general(a, b, (((1,), (1,)), ((), ())), preferred_element_type=F32)


def _dot_tn(a, b):
    return lax.dot_general(a, b, (((0,), (0,)), ((), ())), preferred_element_type=F32)


def _gla_tile(q, k, la, v, gate, gn, st_ref, og_ref, fill):
    C = GLA_CHUNK
    tile = q.shape[0]
    lane_k = lax.broadcasted_iota(jnp.int32, (1, GLA_QK_PAD), 1)
    lane_v = lax.broadcasted_iota(jnp.int32, (1, GLA_WIDTH), 1)
    kmask = [(lane_k >= h * GLA_DK) & (lane_k < (h + 1) * GLA_DK) for h in range(GLA_HEADS)]
    vmask = [(lane_v >= h * GLA_DV) & (lane_v < (h + 1) * GLA_DV) for h in range(GLA_HEADS)]
    row = lax.broadcasted_iota(jnp.int32, (C, 2 * C), 0)
    col = lax.broadcasted_iota(jnp.int32, (C, 2 * C), 1)
    cum2 = (row >= (col & (C - 1))).astype(BF16)
    hrow = lax.broadcasted_iota(jnp.int32, (C, GLA_HEADS * C), 0)
    hcol = lax.broadcasted_iota(jnp.int32, (C, GLA_HEADS * C), 1)
    tril_heads = hrow >= (hcol & (C - 1))
    srow = lax.broadcasted_iota(jnp.int32, (GLA_WIDTH, GLA_QK_PAD), 0)
    slane = lax.broadcasted_iota(jnp.int32, (GLA_WIDTH, GLA_QK_PAD), 1)
    same_head = None
    for h in range(GLA_HEADS):
        blk = ((srow >= h * GLA_DV) & (srow < (h + 1) * GLA_DV)
               & (slane >= h * GLA_DK) & (slane < (h + 1) * GLA_DK))
        same_head = blk if same_head is None else (same_head | blk)

    st = st_ref[...]
    for c in range(tile // C):
        sl = slice(c * C, (c + 1) * C)
        qc, kc, lac, vc = q[sl], k[sl], la[sl], v[sl]
        la_hi = lac.astype(BF16)
        la_lo = (lac - la_hi.astype(F32)).astype(BF16)
        b = _dot(cum2, jnp.concatenate([la_hi, la_lo], axis=0))
        fill()
        b_last = b[C - 1:C, :]
        q_dec = (qc * (GLA_DK ** -0.5) * jnp.exp(b)).astype(BF16)
        k_inv = kc * jnp.exp(-b)
        k_end = (kc * jnp.exp(b_last - b)).astype(BF16)
        decay = jnp.exp(b_last)
        k_heads = jnp.concatenate(
            [jnp.where(kmask[h], k_inv, 0.0) for h in range(GLA_HEADS)], axis=0).astype(BF16)
        a = _dot_nt(q_dec, k_heads)
        fill()
        a = jnp.where(tril_heads, a, 0.0).astype(BF16)
        v_heads = jnp.concatenate(
            [jnp.where(vmask[h], vc, 0.0) for h in range(GLA_HEADS)], axis=0).astype(BF16)
        o = _dot(a, v_heads) + _dot_nt(q_dec, st.astype(BF16))
        fill()
        upd = _dot_tn(vc.astype(BF16), k_end)
        fill()
        st = st * decay + jnp.where(same_head, upd, 0.0)
        o2 = o * o
        ms = jnp.zeros_like(o)
        for h in range(GLA_HEADS):
            mh = jnp.sum(jnp.where(vmask[h], o2, 0.0), axis=1, keepdims=True) * (1.0 / GLA_DV)
            ms = jnp.where(vmask[h], mh, ms)
        y = o * lax.rsqrt(ms + EPS) * gn
        og_ref[0, sl, :] = (y * gate[sl]).astype(BF16)
    st_ref[...] = st


def _proj_kernel(h_ref, g_ref, win_ref, wg_ref, bg_ref, gn_ref, convw_ref, qn_ref, wuq_ref,
                 kvn_ref, wukv_ref, cos_ref, sin_ref,
                 og_ref, oconv_ref, mq_ref, mk_ref, mv_ref, mg_ref,
                 ubuf_ref, st_ref, *, layer, tile, attn_tile, q_scale):
    t = pl.program_id(1)

    @pl.when(t == 0)
    def _():
        ubuf_ref[0:8, :] = jnp.zeros((8, CONV_WIDTH), F32)
        st_ref[...] = jnp.zeros(st_ref.shape, F32)

    x = h_ref[0]
    row = slice(layer, layer + 1)
    xn = _rms(x, g_ref[row, :]).astype(BF16)

    def seg(a, b):
        return _dot(xn, win_ref[:, a:b])

    val = {}
    work = []

    def queue_segment(name, a, b):
        parts = []
        for n0 in range(a, b, PIECE_N):
            work.append((D_MODEL / PIECE_N, lambda n0=n0: parts.append(seg(n0, min(n0 + PIECE_N, b)))))
        work.append((0.0, lambda: val.update({name: jnp.concatenate(parts, axis=1)})))

    def q_up():
        cqn = _rms(val["cq"], qn_ref[row, :]).astype(BF16)
        val["qn"] = _dot(cqn, wuq_ref[:, 0:MLA_Q_NOPE_W]) * q_scale
        val["qr"] = _dot(cqn, wuq_ref[:, MLA_Q_NOPE_W:])

    kv_parts = []
    kv_width = MLA_HEADS * MLA_SLOT + MLA_WIDTH

    def kv_piece(n0):
        if n0 == 0:
            val["ckvn"] = _rms(val["ckv"], kvn_ref[row, :]).astype(BF16)
        kv_parts.append(_dot(val["ckvn"], wukv_ref[:, n0:min(n0 + PIECE_N, kv_width)]))

    queue_segment("cq", SEG_CQ, SEG_CKV)
    queue_segment("ckv", SEG_CKV, SEG_MG)
    work.append(((MLA_Q_NOPE_W + 2 * MLA_SLOT) / PIECE_N, q_up))
    for n0 in range(0, kv_width, PIECE_N):
        work.append((1.0, functools.partial(kv_piece, n0)))
    queue_segment("cc", SEG_CC, SEG_CB)
    queue_segment("ch", SEG_CH, SEG_CG)
    queue_segment("cb", SEG_CB, SEG_CH)
    queue_segment("cg", SEG_CG, SEG_CQ)
    queue_segment("mg", SEG_MG, SEG_KR)
    work.reverse()
    total_cost = sum(c for c, _ in work)
    slots = GLA_FILL_STAGES * (tile // GLA_CHUNK)
    progress = {"slot": 0, "cost": 0.0}

    def fill():
        progress["slot"] += 1
        target = total_cost * progress["slot"] / slots
        while work and progress["cost"] + 0.5 * work[-1][0] <= target:
            cost, thunk = work.pop()
            progress["cost"] += cost
            thunk()

    kr = seg(SEG_KR, IN_PAD)
    gq, gk, gv, gg = seg(SEG_Q, SEG_K), seg(SEG_K, SEG_V), seg(SEG_V, SEG_GG), seg(SEG_GG, SEG_CC)
    wg = jnp.concatenate(
        [jnp.zeros((KR_GATE_OFF, GLA_QK_PAD), F32),
         jnp.concatenate([wg_ref[...], jnp.zeros((GLA_GATE_RANK, GLA_QK_PAD - GLA_QK), F32)], axis=1),
         jnp.zeros((MLA_SLOT - KR_GATE_OFF - GLA_GATE_RANK, GLA_QK_PAD), F32)], axis=0)
    bg = jnp.concatenate([bg_ref[row, :], jnp.zeros((1, GLA_QK_PAD - GLA_QK), F32)], axis=1)
    gn = jnp.concatenate([gn_ref[row, :]] * GLA_HEADS, axis=1)
    z = _dot(kr.astype(BF16), wg.astype(BF16)) + bg
    la = (jnp.minimum(z, 0.0) - jnp.log1p(jnp.exp(-jnp.abs(z)))) * (1.0 / GLA_TAU)
    _gla_tile(gq, gk, la, gv, _silu(gg), gn, st_ref, og_ref, fill)
    while work:
        work.pop()[1]()

    u = val["cc"] * val["ch"]
    ubuf_ref[8:8 + tile, :] = u
    cw = convw_ref[...]
    conv = (cw[0:1, :] * ubuf_ref[6:6 + tile, :] + cw[1:2, :] * ubuf_ref[7:7 + tile, :]
            + cw[2:3, :] * u)
    oconv_ref[0] = (val["cb"] * conv * _silu(val["cg"])).astype(BF16)
    ubuf_ref[0:8, :] = u[tile - 8:tile, :]

    reps = MLA_SLOT // ROPE_HALF
    cos = jnp.concatenate([cos_ref[0]] * reps, axis=0).T
    sin = jnp.concatenate([sin_ref[0]] * reps, axis=0).T
    qn, qr = val["qn"], val["qr"]
    x1, x2 = qr[:, 0:MLA_SLOT], qr[:, MLA_SLOT:]
    cq, sq = cos * q_scale, sin * q_scale
    r1 = x1 * cq - x2 * sq
    r2 = x2 * cq + x1 * sq
    pad_rows = jnp.zeros((MLA_SLOT - MLA_NOPE - MLA_ROPE, attn_tile), BF16)
    for i in range(tile // attn_tile):
        rows = slice(i * attn_tile, (i + 1) * attn_tile)
        qnt = qn[rows].T.astype(BF16)
        r1t = r1[rows].T.astype(BF16)
        r2t = r2[rows].T.astype(BF16)
        for h in range(MLA_HEADS):
            base = h * MLA_SLOT
            mq_ref[0, i, base:base + MLA_NOPE, :] = qnt[h * MLA_NOPE:(h + 1) * MLA_NOPE]
            mq_ref[0, i, base + MLA_NOPE:base + MLA_NOPE + ROPE_HALF, :] = (
                r1t[h * ROPE_HALF:(h + 1) * ROPE_HALF])
            mq_ref[0, i, base + MLA_NOPE + ROPE_HALF:base + MLA_NOPE + MLA_ROPE, :] = (
                r2t[h * ROPE_HALF:(h + 1) * ROPE_HALF])
            mq_ref[0, i, base + MLA_NOPE + MLA_ROPE:base + MLA_SLOT, :] = pad_rows

    kvh = jnp.concatenate(kv_parts, axis=1)
    lane = lax.broadcasted_iota(jnp.int32, (1, MLA_SLOT), 1)
    first = (lane >= KR_ROPE_OFF) & (lane < KR_ROPE_OFF + ROPE_HALF)
    second = (lane >= KR_ROPE_OFF + ROPE_HALF) & (lane < KR_ROPE_OFF + MLA_ROPE)
    krope = (kr * jnp.where(first | second, cos, 0.0)
             + pltpu.roll(kr, ROPE_HALF, 1) * jnp.where(second, sin, 0.0)
             - pltpu.roll(kr, MLA_SLOT - ROPE_HALF, 1) * jnp.where(first, sin, 0.0))
    for h in range(MLA_HEADS):
        sl = slice(h * MLA_SLOT, (h + 1) * MLA_SLOT)
        mk_ref[0, :, sl] = (kvh[:, sl] + krope).astype(BF16)
    vals = kvh[:, MLA_HEADS * MLA_SLOT:]
    for i in range(tile // attn_tile):
        mv_ref[0, i] = vals[i * attn_tile:(i + 1) * attn_tile].T.astype(BF16)
    mg_ref[0] = _silu(val["mg"])


def _proj_call(h, l, p, tables, tile, attn_tile):
    B, S, D = h.shape
    nt = S // tile
    sub = tile // attn_tile
    q_scale = LOG2E / math.sqrt(MLA_NOPE + MLA_ROPE)

    def tok(w):
        return pl.BlockSpec((1, tile, w), lambda b, t: (b, t, 0))

    def lay(shape):
        return pl.BlockSpec((None,) + shape, lambda b, t: (l,) + (0,) * len(shape))

    def stacked(w):
        return pl.BlockSpec((depth, w), lambda b, t: (0, 0))

    depth = p["w_in"].shape[0]
    in_specs = [
        tok(D),
        stacked(D),
        lay((D, IN_PAD)),
        lay((GLA_GATE_RANK, GLA_QK)),
        stacked(GLA_QK),
        stacked(GLA_DV),
        lay((CONV_K, CONV_WIDTH)),
        stacked(MLA_Q_RANK),
        lay((MLA_Q_RANK, MLA_Q_NOPE_W + 2 * MLA_SLOT)),
        stacked(MLA_KV_RANK),
        lay((MLA_KV_RANK, MLA_HEADS * MLA_SLOT + MLA_WIDTH)),
        pl.BlockSpec((1, ROPE_HALF, tile), lambda b, t: (b, 0, t)),
        pl.BlockSpec((1, ROPE_HALF, tile), lambda b, t: (b, 0, t)),
    ]

    def tok_out(w, dt):
        return tok(w), jax.ShapeDtypeStruct((B, S, w), dt)

    def tile_t_out(w, dt):
        return (pl.BlockSpec((1, sub, w, attn_tile), lambda b, t: (b, t, 0, 0)),
                jax.ShapeDtypeStruct((B, S // attn_tile, w, attn_tile), dt))

    outs = [
        tok_out(GLA_WIDTH, BF16), tok_out(CONV_WIDTH, BF16),
        tile_t_out(MLA_HEADS * MLA_SLOT, BF16), tok_out(MLA_HEADS * MLA_SLOT, BF16),
        tile_t_out(MLA_WIDTH, BF16), tok_out(MLA_WIDTH, F32),
    ]
    return pl.pallas_call(
        functools.partial(_proj_kernel, layer=l, tile=tile, attn_tile=attn_tile, q_scale=q_scale),
        grid=(B, nt),
        in_specs=in_specs,
        out_specs=[spec for spec, _ in outs],
        out_shape=[shape for _, shape in outs],
        scratch_shapes=[pltpu.VMEM((tile + 8, CONV_WIDTH), F32),
                        pltpu.VMEM((GLA_WIDTH, GLA_QK_PAD), F32)],
        compiler_params=pltpu.CompilerParams(
            dimension_semantics=("arbitrary", "arbitrary"),
            vmem_limit_bytes=VMEM_LIMIT_BYTES),
        name="proj",
    )(h, p["norm_mix"], p["w_in"], p["wg"], p["bg"], p["gla_norm"], p["conv_w"], p["q_norm"],
      p["w_uq"], p["kv_norm"], p["w_ukv"], *tables)


def _mla_kernel(qt_ref, k_ref, vt_ref, mg_ref, o_ref, m_ref, acc_ref, s_ref, *, tile):
    qi = pl.program_id(1)
    nt = pl.num_programs(1)
    half = tile // 2
    krow = lax.broadcasted_iota(jnp.int32, (half, tile), 0)
    qcol = lax.broadcasted_iota(jnp.int32, (half, tile), 1)
    causal_top = krow <= qcol
    causal_sq = causal_top[:, 0:half]
    ones_rows = jnp.ones((MLA_ACC_ROWS - MLA_V, tile), BF16)

    m_ref[...] = jnp.full(m_ref.shape, MASK_VALUE, F32)
    acc_ref[...] = jnp.zeros(acc_ref.shape, F32)

    def head(e):
        return slice(e * MLA_SLOT, (e + 1) * MLA_SLOT)

    def scores(j, e, q_tile):
        rows = pl.ds(pl.multiple_of(j * tile, tile), tile)
        s_ref[e] = _dot(k_ref[0, rows, head(e)], qt_ref[0, q_tile, head(e), :])

    def scores_diag(e):
        top = pl.ds(pl.multiple_of(qi * tile, tile), half)
        bot = pl.ds(pl.multiple_of(qi * tile + half, half), half)
        s_ref[e, 0:half, :] = _dot(k_ref[0, top, head(e)], qt_ref[0, qi, head(e), :])
        s_ref[e, half:tile, half:tile] = _dot(k_ref[0, bot, head(e)],
                                              qt_ref[0, qi, head(e), half:tile])

    def values(j, e, lo, hi):
        vt = vt_ref[0, j, e * MLA_V:(e + 1) * MLA_V, lo:hi]
        return jnp.concatenate([vt, ones_rows[:, lo:hi]], axis=0)

    def consume(j, e):
        s = s_ref[e]
        m_prev = m_ref[e]
        m_new = jnp.maximum(m_prev, jnp.max(s, axis=0, keepdims=True))
        alpha = jnp.exp2(m_prev - m_new)
        p = jnp.exp2(s - m_new).astype(BF16)
        acc_ref[e] = alpha * acc_ref[e] + _dot(values(j, e, 0, tile), p)
        m_ref[e] = m_new

    def consume_diag(e):
        s = jnp.where(causal_top, s_ref[e, 0:half, :], MASK_VALUE)
        m_prev = m_ref[e]
        m_new = jnp.maximum(m_prev, jnp.max(s, axis=0, keepdims=True))
        p = jnp.exp2(s - m_new).astype(BF16)
        acc = jnp.exp2(m_prev - m_new) * acc_ref[e] + _dot(values(qi, e, 0, half), p)
        s2 = jnp.where(causal_sq, s_ref[e, half:tile, half:tile], MASK_VALUE)
        m2_prev = m_new[:, half:tile]
        m2_new = jnp.maximum(m2_prev, jnp.max(s2, axis=0, keepdims=True))
        p2 = jnp.exp2(s2 - m2_new).astype(BF16)
        acc2 = (jnp.exp2(m2_prev - m2_new) * acc[:, half:tile]
                + _dot(values(qi, e, half, tile), p2))
        acc = jnp.concatenate([acc[:, 0:half], acc2], axis=1)
        return acc[0:MLA_V, :] / acc[MLA_V:MLA_V + 1, :]

    nxt = jnp.minimum(qi + 1, nt - 1)

    @pl.when(qi == 0)
    def _():
        for e in range(MLA_AHEAD):
            scores(0, e, 0)

    def blocks(j0, count):
        for t in range(count * MLA_HEADS):
            ahead = t + MLA_AHEAD
            scores(j0 + ahead // MLA_HEADS, ahead % MLA_HEADS, qi)
            consume(j0 + t // MLA_HEADS, t % MLA_HEADS)

    def body(i, carry):
        blocks(MLA_UNROLL * i, MLA_UNROLL)
        return carry

    lax.fori_loop(0, qi // MLA_UNROLL, body, 0)
    rem = qi % MLA_UNROLL
    size = MLA_UNROLL // 2
    while size >= 1:
        @pl.when((rem & size) != 0)
        def _(size=size):
            blocks(qi - (rem & (2 * size - 1)), size)
        size //= 2

    outs = []
    for e in range(MLA_HEADS):
        ahead = e + MLA_AHEAD
        if ahead < MLA_HEADS:
            scores_diag(ahead)
        else:
            scores(0, ahead - MLA_HEADS, nxt)
        outs.append(consume_diag(e))
    ot = jnp.concatenate(outs, axis=0)
    o_ref[0] = (ot.T * mg_ref[0]).astype(BF16)


def _mla_call(mqt, mk, mvt, mg, tile):
    B, nt, _, _ = mqt.shape
    S = nt * tile
    return pl.pallas_call(
        functools.partial(_mla_kernel, tile=tile),
        grid=(B, nt),
        in_specs=[
            pl.BlockSpec((1, nt, MLA_HEADS * MLA_SLOT, tile), lambda b, i: (b, 0, 0, 0)),
            pl.BlockSpec((1, S, MLA_HEADS * MLA_SLOT), lambda b, i: (b, 0, 0)),
            pl.BlockSpec((1, nt, MLA_WIDTH, tile), lambda b, i: (b, 0, 0, 0)),
            pl.BlockSpec((1, tile, MLA_WIDTH), lambda b, i: (b, i, 0)),
        ],
        out_specs=pl.BlockSpec((1, tile, MLA_WIDTH), lambda b, i: (b, i, 0)),
        out_shape=jax.ShapeDtypeStruct((B, S, MLA_WIDTH), BF16),
        scratch_shapes=[pltpu.VMEM((MLA_HEADS, 1, tile), F32),
                        pltpu.VMEM((MLA_HEADS, MLA_ACC_ROWS, tile), F32),
                        pltpu.VMEM((MLA_HEADS, tile, tile), F32)],
        compiler_params=pltpu.CompilerParams(
            dimension_semantics=("arbitrary", "arbitrary"),
            vmem_limit_bytes=VMEM_LIMIT_BYTES),
        name="mla",
    )(mqt, mk, mvt, mg)


def _post_kernel(h_ref, og_ref, oc_ref, om_ref, wout_ref, gx_ref, wq_ref, mk_ref, mv_ref,
                 wo_ref, gf_ref, o_ref, wout_bf, wq_bf, wo_bf, *, layer, final_norm):
    @pl.when((pl.program_id(0) == 0) & (pl.program_id(1) == 0))
    def _():
        wout_bf[...] = wout_ref[...].astype(BF16)
        wq_bf[...] = wq_ref[...].astype(BF16)
        wo_bf[...] = wo_ref[...].astype(BF16)

    x = jnp.concatenate([og_ref[0], oc_ref[0], om_ref[0]], axis=1)
    h1 = h_ref[0] + _dot(x, wout_bf[...])
    hn = _rms(h1, gx_ref[layer:layer + 1, :]).astype(BF16)
    q = (_dot(hn, wq_bf[...]) * (LOG2E / math.sqrt(MEM_HEAD_DIM))).astype(BF16)
    ones = jnp.ones((mv_ref.shape[0], MEM_HEAD_DIM), BF16)
    heads = []
    for hd in range(MEM_HEADS):
        sl = slice(hd * MEM_HEAD_DIM, (hd + 1) * MEM_HEAD_DIM)
        s = _dot_nt(q[:, sl], mk_ref[:, sl])
        p = jnp.exp2(s - jnp.max(s, axis=1, keepdims=True)).astype(BF16)
        nd = _dot(p, jnp.concatenate([mv_ref[:, sl], ones], axis=1))
        heads.append((nd[:, :MEM_HEAD_DIM] / nd[:, MEM_HEAD_DIM:]).astype(BF16))
    o = jnp.concatenate(heads, axis=1)
    h2 = h1 + _dot(o, wo_bf[...])
    if final_norm:
        h2 = _rms(h2, gf_ref[...])
    o_ref[0] = h2


def _post_call(h, og, oc, om, memk, memv, l, p, norm_final, tile, final_norm):
    B, S, D = h.shape
    nt = S // tile
    M = memk.shape[2]

    def tok(w):
        return pl.BlockSpec((1, tile, w), lambda b, t: (b, t, 0))

    def lay(shape):
        return pl.BlockSpec((None,) + shape, lambda b, t: (l,) + (0,) * len(shape))

    mem_spec = pl.BlockSpec((None, None, M, MEM_INNER), lambda b, t: (l, b, 0, 0))
    return pl.pallas_call(
        functools.partial(_post_kernel, layer=l, final_norm=final_norm),
        grid=(B, nt),
        in_specs=[tok(D), tok(GLA_WIDTH), tok(CONV_WIDTH), tok(MLA_WIDTH),
                  lay((D_MIX, D)), pl.BlockSpec(p["norm_xattn"].shape, lambda b, t: (0, 0)),
                  lay((D, MEM_INNER)), mem_spec, mem_spec,
                  lay((MEM_INNER, D)), pl.BlockSpec((1, D), lambda b, t: (0, 0))],
        out_specs=tok(D),
        out_shape=jax.ShapeDtypeStruct((B, S, D), F32),
        scratch_shapes=[pltpu.VMEM((D_MIX, D), BF16), pltpu.VMEM((D, MEM_INNER), BF16),
                        pltpu.VMEM((MEM_INNER, D), BF16)],
        compiler_params=pltpu.CompilerParams(
            dimension_semantics=("arbitrary", "arbitrary"),
            vmem_limit_bytes=VMEM_LIMIT_BYTES),
        name="post",
    )(h, og, oc, om, p["w_out"], p["norm_xattn"], p["mem_wq"], memk, memv, p["mem_wo"],
      norm_final)


def _pad_cols(w, width):
    return jnp.pad(w, [(0, 0)] * (w.ndim - 1) + [(0, width - w.shape[-1])])


def _prep_kernel(wt_ref, mem_ref, gm_ref, wk_ref, wv_ref, o_ref, k_ref, v_ref):
    @pl.when(pl.program_id(1) == 0)
    def _():
        B, M, D = mem_ref.shape
        g = gm_ref[pl.ds(pl.program_id(0), 1), :]
        memn = _rms(mem_ref[...].reshape(B * M, D), g).astype(BF16)
        k_ref[...] = _dot(memn, wk_ref[...].astype(BF16)).astype(BF16).reshape(B, M, MEM_INNER)
        v_ref[...] = _dot(memn, wv_ref[...].astype(BF16)).astype(BF16).reshape(B, M, MEM_INNER)

    cols = wt_ref.shape[2]
    o_gv = 2 * GLA_QK
    o_glr = o_gv + GLA_WIDTH
    o_gg = o_glr + GLA_GATE_RANK
    o_cc = o_gg + GLA_WIDTH
    o_kr = o_cc + 4 * CONV_WIDTH + MLA_Q_RANK + MLA_KV_RANK
    o_mg = o_kr + MLA_ROPE
    zeros = lambda n: jnp.zeros((n, cols), F32)

    def rows(a, b):
        return wt_ref[0, a:b, :]

    pieces = [
        (SEG_Q, [rows(0, GLA_QK), zeros(GLA_QK_PAD - GLA_QK)]),
        (SEG_K, [rows(GLA_QK, o_gv), zeros(GLA_QK_PAD - GLA_QK)]),
        (SEG_V, [rows(o_gv, o_glr)]),
        (SEG_GG, [rows(o_gg, o_cc)]),
        (SEG_CC, [rows(o_cc, o_kr)]),
        (SEG_MG, [rows(o_mg, o_mg + MLA_WIDTH)]),
        (SEG_KR, [zeros(KR_ROPE_OFF), rows(o_kr, o_mg), rows(o_glr, o_gg),
                  zeros(MLA_SLOT - KR_GATE_OFF - GLA_GATE_RANK)]),
    ]
    for off, parts in pieces:
        blk = parts[0] if len(parts) == 1 else jnp.concatenate(parts, axis=0)
        o_ref[0, :, off:off + blk.shape[0]] = blk.T.astype(BF16)


def _prep_call(w_in, mem, norm_mem, wk, wv):
    wt = jnp.swapaxes(w_in, 1, 2)
    L, W, D = wt.shape
    B, M, _ = mem.shape
    cols = 512
    kv_shape = jax.ShapeDtypeStruct((L, B, M, MEM_INNER), BF16)
    kv_spec = pl.BlockSpec((None, B, M, MEM_INNER), lambda l, c: (l, 0, 0, 0))
    return pl.pallas_call(
        _prep_kernel,
        grid=(L, D // cols),
        in_specs=[pl.BlockSpec((1, W, cols), lambda l, c: (l, 0, c)),
                  pl.BlockSpec((B, M, D), lambda l, c: (0, 0, 0)),
                  pl.BlockSpec((L, D), lambda l, c: (0, 0)),
                  pl.BlockSpec((None, D, MEM_INNER), lambda l, c: (l, 0, 0)),
                  pl.BlockSpec((None, D, MEM_INNER), lambda l, c: (l, 0, 0))],
        out_specs=[pl.BlockSpec((1, cols, IN_PAD), lambda l, c: (l, c, 0)), kv_spec, kv_spec],
        out_shape=[jax.ShapeDtypeStruct((L, D, IN_PAD), BF16), kv_shape, kv_shape],
        compiler_params=pltpu.CompilerParams(
            dimension_semantics=("arbitrary", "arbitrary"),
            vmem_limit_bytes=VMEM_LIMIT_BYTES),
        name="prep",
    )(wt, mem, norm_mem, wk, wv)


def _prep_params(norm_mix, w_in_packed, gla_w_gate, gla_b_gate, gla_norm, conv_w, mla_q_norm, mla_w_uq,
                 mla_kv_norm, mla_w_ukv, w_out, norm_xattn, mem_wq, mem_wo):
    L = w_in_packed.shape[0]
    w_uq = mla_w_uq.reshape(L, MLA_Q_RANK, MLA_HEADS, MLA_NOPE + MLA_ROPE)
    half = lambda lo: _pad_cols(
        w_uq[..., lo:lo + ROPE_HALF].reshape(L, MLA_Q_RANK, MLA_HEADS * ROPE_HALF), MLA_SLOT)
    w_uq = jnp.concatenate(
        [w_uq[..., :MLA_NOPE].reshape(L, MLA_Q_RANK, MLA_Q_NOPE_W), half(MLA_NOPE),
         half(MLA_NOPE + ROPE_HALF)], axis=-1)
    w_ukv = mla_w_ukv.reshape(L, MLA_KV_RANK, MLA_HEADS, MLA_NOPE + MLA_V)
    w_uk = _pad_cols(w_ukv[..., :MLA_NOPE], MLA_SLOT).reshape(L, MLA_KV_RANK, MLA_HEADS * MLA_SLOT)
    w_uv = w_ukv[..., MLA_NOPE:].reshape(L, MLA_KV_RANK, MLA_WIDTH)
    return {
        "norm_mix": norm_mix,
        "w_in": w_in_packed,
        "wg": gla_w_gate,
        "bg": gla_b_gate,
        "gla_norm": gla_norm,
        "conv_w": conv_w,
        "q_norm": mla_q_norm,
        "w_uq": w_uq.astype(BF16),
        "kv_norm": mla_kv_norm,
        "w_ukv": jnp.concatenate([w_uk, w_uv], axis=-1).astype(BF16),
        "w_out": w_out,
        "norm_xattn": norm_xattn,
        "mem_wq": mem_wq,
        "mem_wo": mem_wo,
    }


def _rope_tables(positions):
    inv_freq = 1.0 / (ROPE_BASE ** (jnp.arange(0, MLA_ROPE, 2, dtype=F32) / MLA_ROPE))
    ang = positions.astype(F32)[:, None, :] * inv_freq[None, :, None]
    return jnp.cos(ang), jnp.sin(ang)


def _pick_tile(S, want):
    t = min(S, want)
    assert S % t == 0 and t % GLA_CHUNK == 0
    return t


def kernel(x, mem, positions, norm_mix, w_in, gla_w_gate, gla_b_gate, gla_norm, conv_w, mla_q_norm,
           mla_w_uq, mla_kv_norm, mla_w_ukv, w_out, norm_xattn, norm_mem, mem_wq, mem_wk, mem_wv,
           mem_wo, norm_final):
    depth = w_in.shape[0]
    S = x.shape[1]
    attn_tile = _pick_tile(S, 512)
    tile = _pick_tile(S, PROJ_TILE)
    w_in_packed, memk, memv = _prep_call(w_in, mem, norm_mem, mem_wk, mem_wv)
    p = _prep_params(norm_mix, w_in_packed, gla_w_gate, gla_b_gate, gla_norm, conv_w, mla_q_norm,
                     mla_w_uq, mla_kv_norm, mla_w_ukv, w_out, norm_xattn, mem_wq, mem_wo)
    tables = _rope_tables(positions)
    nf = norm_final[None, :]
    h = x
    for l in range(depth):
        og, oc, mq, mk, mv, mg = _proj_call(h, l, p, tables, tile, attn_tile)
        om = _mla_call(mq, mk, mv, mg, attn_tile)
        h = _post_call(h, og, oc, om, memk, memv, l, p, nf, tile, l == depth - 1)
    return h
```

```python
import functools
import math

import jax
import jax.numpy as jnp
from jax import lax
from jax.experimental import pallas as pl
from jax.experimental.pallas import tpu as pltpu

F32 = jnp.float32
BF16 = jnp.bfloat16

D_MODEL = 1024
EPS = 1e-6

GLA_HEADS = 4
GLA_DV = 96
GLA_DK = 48
GLA_QK = GLA_HEADS * GLA_DK
GLA_WIDTH = GLA_HEADS * GLA_DV
GLA_GATE_RANK = 16
GLA_TAU = 16.0
GLA_CHUNK = 64
GLA_QK_PAD = 256

CONV_WIDTH = 256
CONV_K = 3

MLA_HEADS = 6
MLA_NOPE = 64
MLA_ROPE = 32
MLA_V = 64
MLA_Q_RANK = 256
MLA_KV_RANK = 256
MLA_WIDTH = MLA_HEADS * MLA_V
MLA_SLOT = 128
MLA_Q_NOPE_W = MLA_HEADS * MLA_NOPE
MLA_ACC_ROWS = MLA_V + 16
MLA_AHEAD = 2
MLA_UNROLL = 4
ROPE_BASE = 10000.0
ROPE_HALF = MLA_ROPE // 2

D_MIX = GLA_WIDTH + CONV_WIDTH + MLA_WIDTH

MEM_HEADS = 4
MEM_HEAD_DIM = 128
MEM_INNER = MEM_HEADS * MEM_HEAD_DIM

LOG2E = math.log2(math.e)
MASK_VALUE = -1e30

SEG_Q = 0
SEG_K = SEG_Q + GLA_QK_PAD
SEG_V = SEG_K + GLA_QK_PAD
SEG_GG = SEG_V + GLA_WIDTH
SEG_CC = SEG_GG + GLA_WIDTH
SEG_CB = SEG_CC + CONV_WIDTH
SEG_CH = SEG_CB + CONV_WIDTH
SEG_CG = SEG_CH + CONV_WIDTH
SEG_CQ = SEG_CG + CONV_WIDTH
SEG_CKV = SEG_CQ + MLA_Q_RANK
SEG_MG = SEG_CKV + MLA_KV_RANK
SEG_KR = SEG_MG + MLA_WIDTH
IN_PAD = SEG_KR + MLA_SLOT
KR_ROPE_OFF = MLA_NOPE
KR_GATE_OFF = MLA_NOPE + MLA_ROPE

VMEM_LIMIT_BYTES = 56 * 1024 * 1024
PROJ_TILE = 1024
PIECE_N = 256
GLA_FILL_STAGES = 4


def _silu(x):
    return x * (1.0 / (1.0 + jnp.exp(-x)))


def _rms(x, g):
    return x * lax.rsqrt(jnp.mean(x * x, axis=-1, keepdims=True) + EPS) * g


def _dot(a, b):
    return jnp.dot(a, b, preferred_element_type=F32)


def _dot_nt(a, b):
    return lax.dot_general(a, b, (((1,), (1,)), ((), ())), preferred_element_type=F32)


def _dot_tn(a, b):
    return lax.dot_general(a, b, (((0,), (0,)), ((), ())), preferred_element_type=F32)


def _gla_tile(q, k, la, v, gate, gn, st_ref, og_ref, fill):
    C = GLA_CHUNK
    tile = q.shape[0]
    lane_k = lax.broadcasted_iota(jnp.int32, (1, GLA_QK_PAD), 1)
    lane_v = lax.broadcasted_iota(jnp.int32, (1, GLA_WIDTH), 1)
    kmask = [(lane_k >= h * GLA_DK) & (lane_k < (h + 1) * GLA_DK) for h in range(GLA_HEADS)]
    vmask = [(lane_v >= h * GLA_DV) & (lane_v < (h + 1) * GLA_DV) for h in range(GLA_HEADS)]
    row = lax.broadcasted_iota(jnp.int32, (C, 2 * C), 0)
    col = lax.broadcasted_iota(jnp.int32, (C, 2 * C), 1)
    cum2 = (row >= (col & (C - 1))).astype(BF16)
    hrow = lax.broadcasted_iota(jnp.int32, (C, GLA_HEADS * C), 0)
    hcol = lax.broadcasted_iota(jnp.int32, (C, GLA_HEADS * C), 1)
    tril_heads = hrow >= (hcol & (C - 1))
    srow = lax.broadcasted_iota(jnp.int32, (GLA_WIDTH, GLA_QK_PAD), 0)
    slane = lax.broadcasted_iota(jnp.int32, (GLA_WIDTH, GLA_QK_PAD), 1)
    same_head = None
    for h in range(GLA_HEADS):
        blk = ((srow >= h * GLA_DV) & (srow < (h + 1) * GLA_DV)
               & (slane >= h * GLA_DK) & (slane < (h + 1) * GLA_DK))
        same_head = blk if same_head is None else (same_head | blk)

    st = st_ref[...]
    for c in range(tile // C):
        sl = slice(c * C, (c + 1) * C)
        qc, kc, lac, vc = q[sl], k[sl], la[sl], v[sl]
        la_hi = lac.astype(BF16)
        la_lo = (lac - la_hi.astype(F32)).astype(BF16)
        b = _dot(cum2, jnp.concatenate([la_hi, la_lo], axis=0))
        fill()
        b_last = b[C - 1:C, :]
        q_dec = (qc * (GLA_DK ** -0.5) * jnp.exp(b)).astype(BF16)
        k_inv = kc * jnp.exp(-b)
        k_end = (kc * jnp.exp(b_last - b)).astype(BF16)
        decay = jnp.exp(b_last)
        k_heads = jnp.concatenate(
            [jnp.where(kmask[h], k_inv, 0.0) for h in range(GLA_HEADS)], axis=0).astype(BF16)
        a = _dot_nt(q_dec, k_heads)
        fill()
        a = jnp.where(tril_heads, a, 0.0).astype(BF16)
        v_heads = jnp.concatenate(
            [jnp.where(vmask[h], vc, 0.0) for h in range(GLA_HEADS)], axis=0).astype(BF16)
        o = _dot(a, v_heads) + _dot_nt(q_dec, st.astype(BF16))
        fill()
        upd = _dot_tn(vc.astype(BF16), k_end)
        fill()
        st = st * decay + jnp.where(same_head, upd, 0.0)
        o2 = o * o
        ms = jnp.zeros_like(o)
        for h in range(GLA_HEADS):
            mh = jnp.sum(jnp.where(vmask[h], o2, 0.0), axis=1, keepdims=True) * (1.0 / GLA_DV)
            ms = jnp.where(vmask[h], mh, ms)
        y = o * lax.rsqrt(ms + EPS) * gn
        og_ref[0, sl, :] = (y * gate[sl]).astype(BF16)
    st_ref[...] = st


def _proj_kernel(h_ref, g_ref, win_ref, wg_ref, bg_ref, gn_ref, convw_ref, qn_ref, wuq_ref,
                 kvn_ref, wukv_ref, cos_ref, sin_ref,
                 og_ref, oconv_ref, mq_ref, mk_ref, mv_ref, mg_ref,
                 ubuf_ref, st_ref, *, layer, tile, attn_tile, q_scale):
    t = pl.program_id(1)

    @pl.when(t == 0)
    def _():
        ubuf_ref[0:8, :] = jnp.zeros((8, CONV_WIDTH), F32)
        st_ref[...] = jnp.zeros(st_ref.shape, F32)

    x = h_ref[0]
    row = slice(layer, layer + 1)
    xn = _rms(x, g_ref[row, :]).astype(BF16)

    def seg(a, b):
        return _dot(xn, win_ref[:, a:b])

    val = {}
    work = []

    def queue_segment(name, a, b):
        parts = []
        for n0 in range(a, b, PIECE_N):
            work.append((D_MODEL / PIECE_N, lambda n0=n0: parts.append(seg(n0, min(n0 + PIECE_N, b)))))
        work.append((0.0, lambda: val.update({name: jnp.concatenate(parts, axis=1)})))

    def q_up():
        cqn = _rms(val["cq"], qn_ref[row, :]).astype(BF16)
        val["qn"] = _dot(cqn, wuq_ref[:, 0:MLA_Q_NOPE_W]) * q_scale
        val["qr"] = _dot(cqn, wuq_ref[:, MLA_Q_NOPE_W:])

    kv_parts = []
    kv_width = MLA_HEADS * MLA_SLOT + MLA_WIDTH

    def kv_piece(n0):
        if n0 == 0:
            val["ckvn"] = _rms(val["ckv"], kvn_ref[row, :]).astype(BF16)
        kv_parts.append(_dot(val["ckvn"], wukv_ref[:, n0:min(n0 + PIECE_N, kv_width)]))

    queue_segment("cq", SEG_CQ, SEG_CKV)
    queue_segment("ckv", SEG_CKV, SEG_MG)
    work.append(((MLA_Q_NOPE_W + 2 * MLA_SLOT) / PIECE_N, q_up))
    for n0 in range(0, kv_width, PIECE_N):
        work.append((1.0, functools.partial(kv_piece, n0)))
    queue_segment("cc", SEG_CC, SEG_CB)
    queue_segment("ch", SEG_CH, SEG_CG)
    queue_segment("cb", SEG_CB, SEG_CH)
    queue_segment("cg", SEG_CG, SEG_CQ)
    queue_segment("mg", SEG_MG, SEG_KR)
    work.reverse()
    total_cost = sum(c for c, _ in work)
    slots = GLA_FILL_STAGES * (tile // GLA_CHUNK)
    progress = {"slot": 0, "cost": 0.0}

    def fill():
        progress["slot"] += 1
        target = total_cost * progress["slot"] / slots
        while work and progress["cost"] + 0.5 * work[-1][0] <= target:
            cost, thunk = work.pop()
            progress["cost"] += cost
            thunk()

    kr = seg(SEG_KR, IN_PAD)
    gq, gk, gv, gg = seg(SEG_Q, SEG_K), seg(SEG_K, SEG_V), seg(SEG_V, SEG_GG), seg(SEG_GG, SEG_CC)
    wg = jnp.concatenate(
        [jnp.zeros((KR_GATE_OFF, GLA_QK_PAD), F32),
         jnp.concatenate([wg_ref[...], jnp.zeros((GLA_GATE_RANK, GLA_QK_PAD - GLA_QK), F32)], axis=1),
         jnp.zeros((MLA_SLOT - KR_GATE_OFF - GLA_GATE_RANK, GLA_QK_PAD), F32)], axis=0)
    bg = jnp.concatenate([bg_ref[row, :], jnp.zeros((1, GLA_QK_PAD - GLA_QK), F32)], axis=1)
    gn = jnp.concatenate([gn_ref[row, :]] * GLA_HEADS, axis=1)
    z = _dot(kr.astype(BF16), wg.astype(BF16)) + bg
    la = (jnp.minimum(z, 0.0) - jnp.log1p(jnp.exp(-jnp.abs(z)))) * (1.0 / GLA_TAU)
    _gla_tile(gq, gk, la, gv, _silu(gg), gn, st_ref, og_ref, fill)
    while work:
        work.pop()[1]()

    u = val["cc"] * val["ch"]
    ubuf_ref[8:8 + tile, :] = u
    cw = convw_ref[...]
    conv = (cw[0:1, :] * ubuf_ref[6:6 + tile, :] + cw[1:2, :] * ubuf_ref[7:7 + tile, :]
            + cw[2:3, :] * u)
    oconv_ref[0] = (val["cb"] * conv * _silu(val["cg"])).astype(BF16)
    ubuf_ref[0:8, :] = u[tile - 8:tile, :]

    reps = MLA_SLOT // ROPE_HALF
    cos = jnp.concatenate([cos_ref[0]] * reps, axis=0).T
    sin = jnp.concatenate([sin_ref[0]] * reps, axis=0).T
    qn, qr = val["qn"], val["qr"]
    x1, x2 = qr[:, 0:MLA_SLOT], qr[:, MLA_SLOT:]
    cq, sq = cos * q_scale, sin * q_scale
    r1 = x1 * cq - x2 * sq
    r2 = x2 * cq + x1 * sq
    pad_rows = jnp.zeros((MLA_SLOT - MLA_NOPE - MLA_ROPE, attn_tile), BF16)
    for i in range(tile // attn_tile):
        rows = slice(i * attn_tile, (i + 1) * attn_tile)
        qnt = qn[rows].T.astype(BF16)
        r1t = r1[rows].T.astype(BF16)
        r2t = r2[rows].T.astype(BF16)
        for h in range(MLA_HEADS):
            base = h * MLA_SLOT
            mq_ref[0, i, base:base + MLA_NOPE, :] = qnt[h * MLA_NOPE:(h + 1) * MLA_NOPE]
            mq_ref[0, i, base + MLA_NOPE:base + MLA_NOPE + ROPE_HALF, :] = (
                r1t[h * ROPE_HALF:(h + 1) * ROPE_HALF])
            mq_ref[0, i, base + MLA_NOPE + ROPE_HALF:base + MLA_NOPE + MLA_ROPE, :] = (
                r2t[h * ROPE_HALF:(h + 1) * ROPE_HALF])
            mq_ref[0, i, base + MLA_NOPE + MLA_ROPE:base + MLA_SLOT, :] = pad_rows

    kvh = jnp.concatenate(kv_parts, axis=1)
    lane = lax.broadcasted_iota(jnp.int32, (1, MLA_SLOT), 1)
    first = (lane >= KR_ROPE_OFF) & (lane < KR_ROPE_OFF + ROPE_HALF)
    second = (lane >= KR_ROPE_OFF + ROPE_HALF) & (lane < KR_ROPE_OFF + MLA_ROPE)
    krope = (kr * jnp.where(first | second, cos, 0.0)
             + pltpu.roll(kr, ROPE_HALF, 1) * jnp.where(second, sin, 0.0)
             - pltpu.roll(kr, MLA_SLOT - ROPE_HALF, 1) * jnp.where(first, sin, 0.0))
    for h in range(MLA_HEADS):
        sl = slice(h * MLA_SLOT, (h + 1) * MLA_SLOT)
        mk_ref[0, :, sl] = (kvh[:, sl] + krope).astype(BF16)
    vals = kvh[:, MLA_HEADS * MLA_SLOT:]
    for i in range(tile // attn_tile):
        mv_ref[0, i] = vals[i * attn_tile:(i + 1) * attn_tile].T.astype(BF16)
    mg_ref[0] = _silu(val["mg"])


def _proj_call(h, l, p, tables, tile, attn_tile):
    B, S, D = h.shape
    nt = S // tile
    sub = tile // attn_tile
    q_scale = LOG2E / math.sqrt(MLA_NOPE + MLA_ROPE)

    def tok(w):
        return pl.BlockSpec((1, tile, w), lambda b, t: (b, t, 0))

    def lay(shape):
        return pl.BlockSpec((None,) + shape, lambda b, t: (l,) + (0,) * len(shape))

    def stacked(w):
        return pl.BlockSpec((depth, w), lambda b, t: (0, 0))

    depth = p["w_in"].shape[0]
    in_specs = [
        tok(D),
        stacked(D),
        lay((D, IN_PAD)),
        lay((GLA_GATE_RANK, GLA_QK)),
        stacked(GLA_QK),
        stacked(GLA_DV),
        lay((CONV_K, CONV_WIDTH)),
        stacked(MLA_Q_RANK),
        lay((MLA_Q_RANK, MLA_Q_NOPE_W + 2 * MLA_SLOT)),
        stacked(MLA_KV_RANK),
        lay((MLA_KV_RANK, MLA_HEADS * MLA_SLOT + MLA_WIDTH)),
        pl.BlockSpec((1, ROPE_HALF, tile), lambda b, t: (b, 0, t)),
        pl.BlockSpec((1, ROPE_HALF, tile), lambda b, t: (b, 0, t)),
    ]

    def tok_out(w, dt):
        return tok(w), jax.ShapeDtypeStruct((B, S, w), dt)

    def tile_t_out(w, dt):
        return (pl.BlockSpec((1, sub, w, attn_tile), lambda b, t: (b, t, 0, 0)),
                jax.ShapeDtypeStruct((B, S // attn_tile, w, attn_tile), dt))

    outs = [
        tok_out(GLA_WIDTH, BF16), tok_out(CONV_WIDTH, BF16),
        tile_t_out(MLA_HEADS * MLA_SLOT, BF16), tok_out(MLA_HEADS * MLA_SLOT, BF16),
        tile_t_out(MLA_WIDTH, BF16), tok_out(MLA_WIDTH, F32),
    ]
    return pl.pallas_call(
        functools.partial(_proj_kernel, layer=l, tile=tile, attn_tile=attn_tile, q_scale=q_scale),
        grid=(B, nt),
        in_specs=in_specs,
        out_specs=[spec for spec, _ in outs],
        out_shape=[shape for _, shape in outs],
        scratch_shapes=[pltpu.VMEM((tile + 8, CONV_WIDTH), F32),
                        pltpu.VMEM((GLA_WIDTH, GLA_QK_PAD), F32)],
        compiler_params=pltpu.CompilerParams(
            dimension_semantics=("arbitrary", "arbitrary"),
            vmem_limit_bytes=VMEM_LIMIT_BYTES),
        name="proj",
    )(h, p["norm_mix"], p["w_in"], p["wg"], p["bg"], p["gla_norm"], p["conv_w"], p["q_norm"],
      p["w_uq"], p["kv_norm"], p["w_ukv"], *tables)


def _mla_kernel(qt_ref, k_ref, vt_ref, mg_ref, o_ref, m_ref, acc_ref, s_ref, *, tile):
    qi = pl.program_id(1)
    nt = pl.num_programs(1)
    half = tile // 2
    krow = lax.broadcasted_iota(jnp.int32, (half, tile), 0)
    qcol = lax.broadcasted_iota(jnp.int32, (half, tile), 1)
    causal_top = krow <= qcol
    causal_sq = causal_top[:, 0:half]
    ones_rows = jnp.ones((MLA_ACC_ROWS - MLA_V, tile), BF16)

    m_ref[...] = jnp.full(m_ref.shape, MASK_VALUE, F32)
    acc_ref[...] = jnp.zeros(acc_ref.shape, F32)

    def head(e):
        return slice(e * MLA_SLOT, (e + 1) * MLA_SLOT)

    def scores(j, e, q_tile):
        rows = pl.ds(pl.multiple_of(j * tile, tile), tile)
        s_ref[e] = _dot(k_ref[0, rows, head(e)], qt_ref[0, q_tile, head(e), :])

    def scores_diag(e):
        top = pl.ds(pl.multiple_of(qi * tile, tile), half)
        bot = pl.ds(pl.multiple_of(qi * tile + half, half), half)
        s_ref[e, 0:half, :] = _dot(k_ref[0, top, head(e)], qt_ref[0, qi, head(e), :])
        s_ref[e, half:tile, half:tile] = _dot(k_ref[0, bot, head(e)],
                                              qt_ref[0, qi, head(e), half:tile])

    def values(j, e, lo, hi):
        vt = vt_ref[0, j, e * MLA_V:(e + 1) * MLA_V, lo:hi]
        return jnp.concatenate([vt, ones_rows[:, lo:hi]], axis=0)

    def consume(j, e):
        s = s_ref[e]
        m_prev = m_ref[e]
        m_new = jnp.maximum(m_prev, jnp.max(s, axis=0, keepdims=True))
        alpha = jnp.exp2(m_prev - m_new)
        p = jnp.exp2(s - m_new).astype(BF16)
        acc_ref[e] = alpha * acc_ref[e] + _dot(values(j, e, 0, tile), p)
        m_ref[e] = m_new

    def consume_diag(e):
        s = jnp.where(causal_top, s_ref[e, 0:half, :], MASK_VALUE)
        m_prev = m_ref[e]
        m_new = jnp.maximum(m_prev, jnp.max(s, axis=0, keepdims=True))
        p = jnp.exp2(s - m_new).astype(BF16)
        acc = jnp.exp2(m_prev - m_new) * acc_ref[e] + _dot(values(qi, e, 0, half), p)
        s2 = jnp.where(causal_sq, s_ref[e, half:tile, half:tile], MASK_VALUE)
        m2_prev = m_new[:, half:tile]
        m2_new = jnp.maximum(m2_prev, jnp.max(s2, axis=0, keepdims=True))
        p2 = jnp.exp2(s2 - m2_new).astype(BF16)
        acc2 = (jnp.exp2(m2_prev - m2_new) * acc[:, half:tile]
                + _dot(values(qi, e, half, tile), p2))
        acc = jnp.concatenate([acc[:, 0:half], acc2], axis=1)
        return acc[0:MLA_V, :] / acc[MLA_V:MLA_V + 1, :]

    nxt = jnp.minimum(qi + 1, nt - 1)

    @pl.when(qi == 0)
    def _():
        for e in range(MLA_AHEAD):
            scores(0, e, 0)

    def blocks(j0, count):
        for t in range(count * MLA_HEADS):
            ahead = t + MLA_AHEAD
            scores(j0 + ahead // MLA_HEADS, ahead % MLA_HEADS, qi)
            consume(j0 + t // MLA_HEADS, t % MLA_HEADS)

    def body(i, carry):
        blocks(MLA_UNROLL * i, MLA_UNROLL)
        return carry

    lax.fori_loop(0, qi // MLA_UNROLL, body, 0)
    rem = qi % MLA_UNROLL
    size = MLA_UNROLL // 2
    while size >= 1:
        @pl.when((rem & size) != 0)
        def _(size=size):
            blocks(qi - (rem & (2 * size - 1)), size)
        size //= 2

    outs = []
    for e in range(MLA_HEADS):
        ahead = e + MLA_AHEAD
        if ahead < MLA_HEADS:
            scores_diag(ahead)
        else:
            scores(0, ahead - MLA_HEADS, nxt)
        outs.append(consume_diag(e))
    ot = jnp.concatenate(outs, axis=0)
    o_ref[0] = (ot.T * mg_ref[0]).astype(BF16)


def _mla_call(mqt, mk, mvt, mg, tile):
    B, nt, _, _ = mqt.shape
    S = nt * tile
    return pl.pallas_call(
        functools.partial(_mla_kernel, tile=tile),
        grid=(B, nt),
        in_specs=[
            pl.BlockSpec((1, nt, MLA_HEADS * MLA_SLOT, tile), lambda b, i: (b, 0, 0, 0)),
            pl.BlockSpec((1, S, MLA_HEADS * MLA_SLOT), lambda b, i: (b, 0, 0)),
            pl.BlockSpec((1, nt, MLA_WIDTH, tile), lambda b, i: (b, 0, 0, 0)),
            pl.BlockSpec((1, tile, MLA_WIDTH), lambda b, i: (b, i, 0)),
        ],
        out_specs=pl.BlockSpec((1, tile, MLA_WIDTH), lambda b, i: (b, i, 0)),
        out_shape=jax.ShapeDtypeStruct((B, S, MLA_WIDTH), BF16),
        scratch_shapes=[pltpu.VMEM((MLA_HEADS, 1, tile), F32),
                        pltpu.VMEM((MLA_HEADS, MLA_ACC_ROWS, tile), F32),
                        pltpu.VMEM((MLA_HEADS, tile, tile), F32)],
        compiler_params=pltpu.CompilerParams(
            dimension_semantics=("arbitrary", "arbitrary"),
            vmem_limit_bytes=VMEM_LIMIT_BYTES),
        name="mla",
    )(mqt, mk, mvt, mg)


def _post_kernel(h_ref, og_ref, oc_ref, om_ref, wout_ref, gx_ref, wq_ref, mk_ref, mv_ref,
                 wo_ref, gf_ref, o_ref, wout_bf, wq_bf, wo_bf, *, layer, final_norm):
    @pl.when((pl.program_id(0) == 0) & (pl.program_id(1) == 0))
    def _():
        wout_bf[...] = wout_ref[...].astype(BF16)
        wq_bf[...] = wq_ref[...].astype(BF16)
        wo_bf[...] = wo_ref[...].astype(BF16)

    x = jnp.concatenate([og_ref[0], oc_ref[0], om_ref[0]], axis=1)
    h1 = h_ref[0] + _dot(x, wout_bf[...])
    hn = _rms(h1, gx_ref[layer:layer + 1, :]).astype(BF16)
    q = (_dot(hn, wq_bf[...]) * (LOG2E / math.sqrt(MEM_HEAD_DIM))).astype(BF16)
    ones = jnp.ones((mv_ref.shape[0], MEM_HEAD_DIM), BF16)
    head_lanes = [slice(hd * MEM_HEAD_DIM, (hd + 1) * MEM_HEAD_DIM) for hd in range(MEM_HEADS)]
    scores = [_dot_nt(q[:, sl], mk_ref[:, sl]) for sl in head_lanes]
    heads = []
    for s, sl in zip(scores, head_lanes):
        p = jnp.exp2(s - jnp.max(s, axis=1, keepdims=True)).astype(BF16)
        nd = _dot(p, jnp.concatenate([mv_ref[:, sl], ones], axis=1))
        heads.append((nd[:, :MEM_HEAD_DIM] / nd[:, MEM_HEAD_DIM:]).astype(BF16))
    o = jnp.concatenate(heads, axis=1)
    h2 = h1 + _dot(o, wo_bf[...])
    if final_norm:
        h2 = _rms(h2, gf_ref[...])
    o_ref[0] = h2


def _post_call(h, og, oc, om, memk, memv, l, p, norm_final, tile, final_norm):
    B, S, D = h.shape
    nt = S // tile
    M = memk.shape[2]

    def tok(w):
        return pl.BlockSpec((1, tile, w), lambda b, t: (b, t, 0))

    def lay(shape):
        return pl.BlockSpec((None,) + shape, lambda b, t: (l,) + (0,) * len(shape))

    mem_spec = pl.BlockSpec((None, None, M, MEM_INNER), lambda b, t: (l, b, 0, 0))
    return pl.pallas_call(
        functools.partial(_post_kernel, layer=l, final_norm=final_norm),
        grid=(B, nt),
        in_specs=[tok(D), tok(GLA_WIDTH), tok(CONV_WIDTH), tok(MLA_WIDTH),
                  lay((D_MIX, D)), pl.BlockSpec(p["norm_xattn"].shape, lambda b, t: (0, 0)),
                  lay((D, MEM_INNER)), mem_spec, mem_spec,
                  lay((MEM_INNER, D)), pl.BlockSpec((1, D), lambda b, t: (0, 0))],
        out_specs=tok(D),
        out_shape=jax.ShapeDtypeStruct((B, S, D), F32),
        scratch_shapes=[pltpu.VMEM((D_MIX, D), BF16), pltpu.VMEM((D, MEM_INNER), BF16),
                        pltpu.VMEM((MEM_INNER, D), BF16)],
        compiler_params=pltpu.CompilerParams(
            dimension_semantics=("arbitrary", "arbitrary"),
            vmem_limit_bytes=VMEM_LIMIT_BYTES),
        name="post",
    )(h, og, oc, om, p["w_out"], p["norm_xattn"], p["mem_wq"], memk, memv, p["mem_wo"],
      norm_final)


def _pad_cols(w, width):
    return jnp.pad(w, [(0, 0)] * (w.ndim - 1) + [(0, width - w.shape[-1])])


def _prep_kernel(wt_ref, mem_ref, gm_ref, wk_ref, wv_ref, o_ref, k_ref, v_ref):
    @pl.when(pl.program_id(1) == 0)
    def _():
        B, M, D = mem_ref.shape
        g = gm_ref[pl.ds(pl.program_id(0), 1), :]
        memn = _rms(mem_ref[...].reshape(B * M, D), g).astype(BF16)
        k_ref[...] = _dot(memn, wk_ref[...].astype(BF16)).astype(BF16).reshape(B, M, MEM_INNER)
        v_ref[...] = _dot(memn, wv_ref[...].astype(BF16)).astype(BF16).reshape(B, M, MEM_INNER)

    cols = wt_ref.shape[2]
    o_gv = 2 * GLA_QK
    o_glr = o_gv + GLA_WIDTH
    o_gg = o_glr + GLA_GATE_RANK
    o_cc = o_gg + GLA_WIDTH
    o_kr = o_cc + 4 * CONV_WIDTH + MLA_Q_RANK + MLA_KV_RANK
    o_mg = o_kr + MLA_ROPE
    zeros = lambda n: jnp.zeros((n, cols), F32)

    def rows(a, b):
        return wt_ref[0, a:b, :]

    pieces = [
        (SEG_Q, [rows(0, GLA_QK), zeros(GLA_QK_PAD - GLA_QK)]),
        (SEG_K, [rows(GLA_QK, o_gv), zeros(GLA_QK_PAD - GLA_QK)]),
        (SEG_V, [rows(o_gv, o_glr)]),
        (SEG_GG, [rows(o_gg, o_cc)]),
        (SEG_CC, [rows(o_cc, o_kr)]),
        (SEG_MG, [rows(o_mg, o_mg + MLA_WIDTH)]),
        (SEG_KR, [zeros(KR_ROPE_OFF), rows(o_kr, o_mg), rows(o_glr, o_gg),
                  zeros(MLA_SLOT - KR_GATE_OFF - GLA_GATE_RANK)]),
    ]
    for off, parts in pieces:
        blk = parts[0] if len(parts) == 1 else jnp.concatenate(parts, axis=0)
        o_ref[0, :, off:off + blk.shape[0]] = blk.T.astype(BF16)


def _prep_call(w_in, mem, norm_mem, wk, wv):
    wt = jnp.swapaxes(w_in, 1, 2)
    L, W, D = wt.shape
    B, M, _ = mem.shape
    cols = 512
    kv_shape = jax.ShapeDtypeStruct((L, B, M, MEM_INNER), BF16)
    kv_spec = pl.BlockSpec((None, B, M, MEM_INNER), lambda l, c: (l, 0, 0, 0))
    return pl.pallas_call(
        _prep_kernel,
        grid=(L, D // cols),
        in_specs=[pl.BlockSpec((1, W, cols), lambda l, c: (l, 0, c)),
                  pl.BlockSpec((B, M, D), lambda l, c: (0, 0, 0)),
                  pl.BlockSpec((L, D), lambda l, c: (0, 0)),
                  pl.BlockSpec((None, D, MEM_INNER), lambda l, c: (l, 0, 0)),
                  pl.BlockSpec((None, D, MEM_INNER), lambda l, c: (l, 0, 0))],
        out_specs=[pl.BlockSpec((1, cols, IN_PAD), lambda l, c: (l, c, 0)), kv_spec, kv_spec],
        out_shape=[jax.ShapeDtypeStruct((L, D, IN_PAD), BF16), kv_shape, kv_shape],
        compiler_params=pltpu.CompilerParams(
            dimension_semantics=("arbitrary", "arbitrary"),
            vmem_limit_bytes=VMEM_LIMIT_BYTES),
        name="prep",
    )(wt, mem, norm_mem, wk, wv)


def _prep_params(norm_mix, w_in_packed, gla_w_gate, gla_b_gate, gla_norm, conv_w, mla_q_norm, mla_w_uq,
                 mla_kv_norm, mla_w_ukv, w_out, norm_xattn, mem_wq, mem_wo):
    L = w_in_packed.shape[0]
    w_uq = mla_w_uq.reshape(L, MLA_Q_RANK, MLA_HEADS, MLA_NOPE + MLA_ROPE)
    half = lambda lo: _pad_cols(
        w_uq[..., lo:lo + ROPE_HALF].reshape(L, MLA_Q_RANK, MLA_HEADS * ROPE_HALF), MLA_SLOT)
    w_uq = jnp.concatenate(
        [w_uq[..., :MLA_NOPE].reshape(L, MLA_Q_RANK, MLA_Q_NOPE_W), half(MLA_NOPE),
         half(MLA_NOPE + ROPE_HALF)], axis=-1)
    w_ukv = mla_w_ukv.reshape(L, MLA_KV_RANK, MLA_HEADS, MLA_NOPE + MLA_V)
    w_uk = _pad_cols(w_ukv[..., :MLA_NOPE], MLA_SLOT).reshape(L, MLA_KV_RANK, MLA_HEADS * MLA_SLOT)
    w_uv = w_ukv[..., MLA_NOPE:].reshape(L, MLA_KV_RANK, MLA_WIDTH)
    return {
        "norm_mix": norm_mix,
        "w_in": w_in_packed,
        "wg": gla_w_gate,
        "bg": gla_b_gate,
        "gla_norm": gla_norm,
        "conv_w": conv_w,
        "q_norm": mla_q_norm,
        "w_uq": w_uq.astype(BF16),
        "kv_norm": mla_kv_norm,
        "w_ukv": jnp.concatenate([w_uk, w_uv], axis=-1).astype(BF16),
        "w_out": w_out,
        "norm_xattn": norm_xattn,
        "mem_wq": mem_wq,
        "mem_wo": mem_wo,
    }


def _rope_tables(positions):
    inv_freq = 1.0 / (ROPE_BASE ** (jnp.arange(0, MLA_ROPE, 2, dtype=F32) / MLA_ROPE))
    ang = positions.astype(F32)[:, None, :] * inv_freq[None, :, None]
    return jnp.cos(ang), jnp.sin(ang)


def _pick_tile(S, want):
    t = min(S, want)
    assert S % t == 0 and t % GLA_CHUNK == 0
    return t


def kernel(x, mem, positions, norm_mix, w_in, gla_w_gate, gla_b_gate, gla_norm, conv_w, mla_q_norm,
           mla_w_uq, mla_kv_norm, mla_w_ukv, w_out, norm_xattn, norm_mem, mem_wq, mem_wk, mem_wv,
           mem_wo, norm_final):
    depth = w_in.shape[0]
    S = x.shape[1]
    attn_tile = _pick_tile(S, 512)
    tile = _pick_tile(S, PROJ_TILE)
    w_in_packed, memk, memv = _prep_call(w_in, mem, norm_mem, mem_wk, mem_wv)
    p = _prep_params(norm_mix, w_in_packed, gla_w_gate, gla_b_gate, gla_norm, conv_w, mla_q_norm,
                     mla_w_uq, mla_kv_norm, mla_w_ukv, w_out, norm_xattn, mem_wq, mem_wo)
    tables = _rope_tables(positions)
    nf = norm_final[None, :]
    h = x
    for l in range(depth):
        og, oc, mq, mk, mv, mg = _proj_call(h, l, p, tables, tile, attn_tile)
        om = _mla_call(mq, mk, mv, mg, attn_tile)
        h = _post_call(h, og, oc, om, memk, memv, l, p, nf, tile, l == depth - 1)
    return h
```

```python
import functools
import math

import jax
import jax.numpy as jnp
from jax import lax
from jax.experimental import pallas as pl
from jax.experimental.pallas import tpu as pltpu

F32 = jnp.float32
BF16 = jnp.bfloat16

D_MODEL = 1024
EPS = 1e-6

GLA_HEADS = 4
GLA_DV = 96
GLA_DK = 48
GLA_QK = GLA_HEADS * GLA_DK
GLA_WIDTH = GLA_HEADS * GLA_DV
GLA_GATE_RANK = 16
GLA_TAU = 16.0
GLA_CHUNK = 64
GLA_QK_PAD = 256

CONV_WIDTH = 256
CONV_K = 3

MLA_HEADS = 6
MLA_NOPE = 64
MLA_ROPE = 32
MLA_V = 64
MLA_Q_RANK = 256
MLA_KV_RANK = 256
MLA_WIDTH = MLA_HEADS * MLA_V
MLA_SLOT = 128
MLA_Q_NOPE_W = MLA_HEADS * MLA_NOPE
MLA_ACC_ROWS = MLA_V + 16
MLA_AHEAD = 2
MLA_UNROLL = 4
ROPE_BASE = 10000.0
ROPE_HALF = MLA_ROPE // 2

D_MIX = GLA_WIDTH + CONV_WIDTH + MLA_WIDTH

MEM_HEADS = 4
MEM_HEAD_DIM = 128
MEM_INNER = MEM_HEADS * MEM_HEAD_DIM

LOG2E = math.log2(math.e)
MASK_VALUE = -1e30

SEG_Q = 0
SEG_K = SEG_Q + GLA_QK_PAD
SEG_V = SEG_K + GLA_QK_PAD
SEG_GG = SEG_V + GLA_WIDTH
SEG_CC = SEG_GG + GLA_WIDTH
SEG_CB = SEG_CC + CONV_WIDTH
SEG_CH = SEG_CB + CONV_WIDTH
SEG_CG = SEG_CH + CONV_WIDTH
SEG_CQ = SEG_CG + CONV_WIDTH
SEG_CKV = SEG_CQ + MLA_Q_RANK
SEG_MG = SEG_CKV + MLA_KV_RANK
SEG_KR = SEG_MG + MLA_WIDTH
IN_PAD = SEG_KR + MLA_SLOT
KR_ROPE_OFF = MLA_NOPE
KR_GATE_OFF = MLA_NOPE + MLA_ROPE

VMEM_LIMIT_BYTES = 56 * 1024 * 1024
PROJ_TILE = 1024
PIECE_N = 256
GLA_FILL_STAGES = 1


def _silu(x):
    return x * (1.0 / (1.0 + jnp.exp(-x)))


def _rms(x, g):
    return x * lax.rsqrt(jnp.mean(x * x, axis=-1, keepdims=True) + EPS) * g


def _dot(a, b):
    return jnp.dot(a, b, preferred_element_type=F32)


def _dot_nt(a, b):
    return lax.dot_general(a, b, (((1,), (1,)), ((), ())), preferred_element_type=F32)


def _dot_tn(a, b):
    return lax.dot_general(a, b, (((0,), (0,)), ((), ())), preferred_element_type=F32)


def _gla_tile(q, k, la, v, gate, gn, st_ref, og_ref, fill):
    C = GLA_CHUNK
    tile = q.shape[0]
    lane_k = lax.broadcasted_iota(jnp.int32, (1, GLA_QK_PAD), 1)
    lane_v = lax.broadcasted_iota(jnp.int32, (1, GLA_WIDTH), 1)
    kmask = [(lane_k >= h * GLA_DK) & (lane_k < (h + 1) * GLA_DK) for h in range(GLA_HEADS)]
    vmask = [(lane_v >= h * GLA_DV) & (lane_v < (h + 1) * GLA_DV) for h in range(GLA_HEADS)]
    row = lax.broadcasted_iota(jnp.int32, (C, 2 * C), 0)
    col = lax.broadcasted_iota(jnp.int32, (C, 2 * C), 1)
    cum2 = (row >= (col & (C - 1))).astype(BF16)
    hrow = lax.broadcasted_iota(jnp.int32, (C, GLA_HEADS * C), 0)
    hcol = lax.broadcasted_iota(jnp.int32, (C, GLA_HEADS * C), 1)
    tril_heads = hrow >= (hcol & (C - 1))
    srow = lax.broadcasted_iota(jnp.int32, (GLA_WIDTH, GLA_QK_PAD), 0)
    slane = lax.broadcasted_iota(jnp.int32, (GLA_WIDTH, GLA_QK_PAD), 1)
    same_head = None
    for h in range(GLA_HEADS):
        blk = ((srow >= h * GLA_DV) & (srow < (h + 1) * GLA_DV)
               & (slane >= h * GLA_DK) & (slane < (h + 1) * GLA_DK))
        same_head = blk if same_head is None else (same_head | blk)

    n_chunks = tile // C
    cums = []
    for c in range(n_chunks):
        lac = la[c * C:(c + 1) * C]
        la_hi = lac.astype(BF16)
        la_lo = (lac - la_hi.astype(F32)).astype(BF16)
        cums.append(_dot(cum2, jnp.concatenate([la_hi, la_lo], axis=0)))
    q_decs, intras, upds, decays = [], [], [], []
    for c in range(n_chunks):
        sl = slice(c * C, (c + 1) * C)
        qc, kc, vc = q[sl], k[sl], v[sl]
        b = cums[c]
        b_last = b[C - 1:C, :]
        q_dec = (qc * (GLA_DK ** -0.5) * jnp.exp(b)).astype(BF16)
        k_inv = kc * jnp.exp(-b)
        k_end = (kc * jnp.exp(b_last - b)).astype(BF16)
        k_heads = jnp.concatenate(
            [jnp.where(kmask[h], k_inv, 0.0) for h in range(GLA_HEADS)], axis=0).astype(BF16)
        a = _dot_nt(q_dec, k_heads)
        a = jnp.where(tril_heads, a, 0.0).astype(BF16)
        v_heads = jnp.concatenate(
            [jnp.where(vmask[h], vc, 0.0) for h in range(GLA_HEADS)], axis=0).astype(BF16)
        intras.append(_dot(a, v_heads))
        upds.append(_dot_tn(vc.astype(BF16), k_end))
        q_decs.append(q_dec)
        decays.append(jnp.exp(b_last))
    st = st_ref[...]
    for c in range(n_chunks):
        sl = slice(c * C, (c + 1) * C)
        o = intras[c] + _dot_nt(q_decs[c], st.astype(BF16))
        fill()
        st = st * decays[c] + jnp.where(same_head, upds[c], 0.0)
        o2 = o * o
        ms = jnp.zeros_like(o)
        for h in range(GLA_HEADS):
            mh = jnp.sum(jnp.where(vmask[h], o2, 0.0), axis=1, keepdims=True) * (1.0 / GLA_DV)
            ms = jnp.where(vmask[h], mh, ms)
        y = o * lax.rsqrt(ms + EPS) * gn
        og_ref[0, sl, :] = (y * gate[sl]).astype(BF16)
    st_ref[...] = st


def _proj_kernel(h_ref, g_ref, win_ref, wg_ref, bg_ref, gn_ref, convw_ref, qn_ref, wuq_ref,
                 kvn_ref, wukv_ref, cos_ref, sin_ref,
                 og_ref, oconv_ref, mq_ref, mk_ref, mv_ref, mg_ref,
                 ubuf_ref, st_ref, *, layer, tile, attn_tile, q_scale):
    t = pl.program_id(1)

    @pl.when(t == 0)
    def _():
        ubuf_ref[0:8, :] = jnp.zeros((8, CONV_WIDTH), F32)
        st_ref[...] = jnp.zeros(st_ref.shape, F32)

    x = h_ref[0]
    row = slice(layer, layer + 1)
    xn = _rms(x, g_ref[row, :]).astype(BF16)

    def seg(a, b):
        return _dot(xn, win_ref[:, a:b])

    val = {}
    work = []

    def queue_segment(name, a, b):
        parts = []
        for n0 in range(a, b, PIECE_N):
            work.append((D_MODEL / PIECE_N, lambda n0=n0: parts.append(seg(n0, min(n0 + PIECE_N, b)))))
        work.append((0.0, lambda: val.update({name: jnp.concatenate(parts, axis=1)})))

    def q_up():
        cqn = _rms(val["cq"], qn_ref[row, :]).astype(BF16)
        val["qn"] = _dot(cqn, wuq_ref[:, 0:MLA_Q_NOPE_W]) * q_scale
        val["qr"] = _dot(cqn, wuq_ref[:, MLA_Q_NOPE_W:])

    kv_parts = []
    kv_width = MLA_HEADS * MLA_SLOT + MLA_WIDTH

    def kv_piece(n0):
        if n0 == 0:
            val["ckvn"] = _rms(val["ckv"], kvn_ref[row, :]).astype(BF16)
        kv_parts.append(_dot(val["ckvn"], wukv_ref[:, n0:min(n0 + PIECE_N, kv_width)]))

    queue_segment("cq", SEG_CQ, SEG_CKV)
    queue_segment("ckv", SEG_CKV, SEG_MG)
    work.append(((MLA_Q_NOPE_W + 2 * MLA_SLOT) / PIECE_N, q_up))
    for n0 in range(0, kv_width, PIECE_N):
        work.append((1.0, functools.partial(kv_piece, n0)))
    queue_segment("cc", SEG_CC, SEG_CB)
    queue_segment("ch", SEG_CH, SEG_CG)
    queue_segment("cb", SEG_CB, SEG_CH)
    queue_segment("cg", SEG_CG, SEG_CQ)
    queue_segment("mg", SEG_MG, SEG_KR)
    work.reverse()
    total_cost = sum(c for c, _ in work)
    slots = GLA_FILL_STAGES * (tile // GLA_CHUNK)
    progress = {"slot": 0, "cost": 0.0}

    def fill():
        progress["slot"] += 1
        target = total_cost * progress["slot"] / slots
        while work and progress["cost"] + 0.5 * work[-1][0] <= target:
            cost, thunk = work.pop()
            progress["cost"] += cost
            thunk()

    kr = seg(SEG_KR, IN_PAD)
    gq, gk, gv, gg = seg(SEG_Q, SEG_K), seg(SEG_K, SEG_V), seg(SEG_V, SEG_GG), seg(SEG_GG, SEG_CC)
    wg = jnp.concatenate(
        [jnp.zeros((KR_GATE_OFF, GLA_QK_PAD), F32),
         jnp.concatenate([wg_ref[...], jnp.zeros((GLA_GATE_RANK, GLA_QK_PAD - GLA_QK), F32)], axis=1),
         jnp.zeros((MLA_SLOT - KR_GATE_OFF - GLA_GATE_RANK, GLA_QK_PAD), F32)], axis=0)
    bg = jnp.concatenate([bg_ref[row, :], jnp.zeros((1, GLA_QK_PAD - GLA_QK), F32)], axis=1)
    gn = jnp.concatenate([gn_ref[row, :]] * GLA_HEADS, axis=1)
    z = _dot(kr.astype(BF16), wg.astype(BF16)) + bg
    la = (jnp.minimum(z, 0.0) - jnp.log1p(jnp.exp(-jnp.abs(z)))) * (1.0 / GLA_TAU)
    _gla_tile(gq, gk, la, gv, _silu(gg), gn, st_ref, og_ref, fill)
    while work:
        work.pop()[1]()

    u = val["cc"] * val["ch"]
    ubuf_ref[8:8 + tile, :] = u
    cw = convw_ref[...]
    conv = (cw[0:1, :] * ubuf_ref[6:6 + tile, :] + cw[1:2, :] * ubuf_ref[7:7 + tile, :]
            + cw[2:3, :] * u)
    oconv_ref[0] = (val["cb"] * conv * _silu(val["cg"])).astype(BF16)
    ubuf_ref[0:8, :] = u[tile - 8:tile, :]

    reps = MLA_SLOT // ROPE_HALF
    cos = jnp.concatenate([cos_ref[0]] * reps, axis=0).T
    sin = jnp.concatenate([sin_ref[0]] * reps, axis=0).T
    qn, qr = val["qn"], val["qr"]
    x1, x2 = qr[:, 0:MLA_SLOT], qr[:, MLA_SLOT:]
    cq, sq = cos * q_scale, sin * q_scale
    r1 = x1 * cq - x2 * sq
    r2 = x2 * cq + x1 * sq
    pad_rows = jnp.zeros((MLA_SLOT - MLA_NOPE - MLA_ROPE, attn_tile), BF16)
    for i in range(tile // attn_tile):
        rows = slice(i * attn_tile, (i + 1) * attn_tile)
        qnt = qn[rows].T.astype(BF16)
        r1t = r1[rows].T.astype(BF16)
        r2t = r2[rows].T.astype(BF16)
        for h in range(MLA_HEADS):
            base = h * MLA_SLOT
            mq_ref[0, i, base:base + MLA_NOPE, :] = qnt[h * MLA_NOPE:(h + 1) * MLA_NOPE]
            mq_ref[0, i, base + MLA_NOPE:base + MLA_NOPE + ROPE_HALF, :] = (
                r1t[h * ROPE_HALF:(h + 1) * ROPE_HALF])
            mq_ref[0, i, base + MLA_NOPE + ROPE_HALF:base + MLA_NOPE + MLA_ROPE, :] = (
                r2t[h * ROPE_HALF:(h + 1) * ROPE_HALF])
            mq_ref[0, i, base + MLA_NOPE + MLA_ROPE:base + MLA_SLOT, :] = pad_rows

    kvh = jnp.concatenate(kv_parts, axis=1)
    lane = lax.broadcasted_iota(jnp.int32, (1, MLA_SLOT), 1)
    first = (lane >= KR_ROPE_OFF) & (lane < KR_ROPE_OFF + ROPE_HALF)
    second = (lane >= KR_ROPE_OFF + ROPE_HALF) & (lane < KR_ROPE_OFF + MLA_ROPE)
    krope = (kr * jnp.where(first | second, cos, 0.0)
             + pltpu.roll(kr, ROPE_HALF, 1) * jnp.where(second, sin, 0.0)
             - pltpu.roll(kr, MLA_SLOT - ROPE_HALF, 1) * jnp.where(first, sin, 0.0))
    for h in range(MLA_HEADS):
        sl = slice(h * MLA_SLOT, (h + 1) * MLA_SLOT)
        mk_ref[0, :, sl] = (kvh[:, sl] + krope).astype(BF16)
    vals = kvh[:, MLA_HEADS * MLA_SLOT:]
    for i in range(tile // attn_tile):
        mv_ref[0, i] = vals[i * attn_tile:(i + 1) * attn_tile].T.astype(BF16)
    mg_ref[0] = _silu(val["mg"])


def _proj_call(h, l, p, tables, tile, attn_tile):
    B, S, D = h.shape
    nt = S // tile
    sub = tile // attn_tile
    q_scale = LOG2E / math.sqrt(MLA_NOPE + MLA_ROPE)

    def tok(w):
        return pl.BlockSpec((1, tile, w), lambda b, t: (b, t, 0))

    def lay(shape):
        return pl.BlockSpec((None,) + shape, lambda b, t: (l,) + (0,) * len(shape))

    def stacked(w):
        return pl.BlockSpec((depth, w), lambda b, t: (0, 0))

    depth = p["w_in"].shape[0]
    in_specs = [
        tok(D),
        stacked(D),
        lay((D, IN_PAD)),
        lay((GLA_GATE_RANK, GLA_QK)),
        stacked(GLA_QK),
        stacked(GLA_DV),
        lay((CONV_K, CONV_WIDTH)),
        stacked(MLA_Q_RANK),
        lay((MLA_Q_RANK, MLA_Q_NOPE_W + 2 * MLA_SLOT)),
        stacked(MLA_KV_RANK),
        lay((MLA_KV_RANK, MLA_HEADS * MLA_SLOT + MLA_WIDTH)),
        pl.BlockSpec((1, ROPE_HALF, tile), lambda b, t: (b, 0, t)),
        pl.BlockSpec((1, ROPE_HALF, tile), lambda b, t: (b, 0, t)),
    ]

    def tok_out(w, dt):
        return tok(w), jax.ShapeDtypeStruct((B, S, w), dt)

    def tile_t_out(w, dt):
        return (pl.BlockSpec((1, sub, w, attn_tile), lambda b, t: (b, t, 0, 0)),
                jax.ShapeDtypeStruct((B, S // attn_tile, w, attn_tile), dt))

    outs = [
        tok_out(GLA_WIDTH, BF16), tok_out(CONV_WIDTH, BF16),
        tile_t_out(MLA_HEADS * MLA_SLOT, BF16), tok_out(MLA_HEADS * MLA_SLOT, BF16),
        tile_t_out(MLA_WIDTH, BF16), tok_out(MLA_WIDTH, F32),
    ]
    return pl.pallas_call(
        functools.partial(_proj_kernel, layer=l, tile=tile, attn_tile=attn_tile, q_scale=q_scale),
        grid=(B, nt),
        in_specs=in_specs,
        out_specs=[spec for spec, _ in outs],
        out_shape=[shape for _, shape in outs],
        scratch_shapes=[pltpu.VMEM((tile + 8, CONV_WIDTH), F32),
                        pltpu.VMEM((GLA_WIDTH, GLA_QK_PAD), F32)],
        compiler_params=pltpu.CompilerParams(
            dimension_semantics=("arbitrary", "arbitrary"),
            vmem_limit_bytes=VMEM_LIMIT_BYTES),
        name="proj",
    )(h, p["norm_mix"], p["w_in"], p["wg"], p["bg"], p["gla_norm"], p["conv_w"], p["q_norm"],
      p["w_uq"], p["kv_norm"], p["w_ukv"], *tables)


def _mla_kernel(qt_ref, k_ref, vt_ref, mg_ref, o_ref, m_ref, acc_ref, s_ref, *, tile):
    qi = pl.program_id(1)
    nt = pl.num_programs(1)
    half = tile // 2
    krow = lax.broadcasted_iota(jnp.int32, (half, tile), 0)
    qcol = lax.broadcasted_iota(jnp.int32, (half, tile), 1)
    causal_top = krow <= qcol
    causal_sq = causal_top[:, 0:half]
    ones_rows = jnp.ones((MLA_ACC_ROWS - MLA_V, tile), BF16)

    m_ref[...] = jnp.full(m_ref.shape, MASK_VALUE, F32)
    acc_ref[...] = jnp.zeros(acc_ref.shape, F32)

    def head(e):
        return slice(e * MLA_SLOT, (e + 1) * MLA_SLOT)

    def scores(j, e, q_tile):
        rows = pl.ds(pl.multiple_of(j * tile, tile), tile)
        s_ref[e] = _dot(k_ref[0, rows, head(e)], qt_ref[0, q_tile, head(e), :])

    def scores_diag(e):
        top = pl.ds(pl.multiple_of(qi * tile, tile), half)
        bot = pl.ds(pl.multiple_of(qi * tile + half, half), half)
        s_ref[e, 0:half, :] = _dot(k_ref[0, top, head(e)], qt_ref[0, qi, head(e), :])
        s_ref[e, half:tile, half:tile] = _dot(k_ref[0, bot, head(e)],
                                              qt_ref[0, qi, head(e), half:tile])

    def values(j, e, lo, hi):
        vt = vt_ref[0, j, e * MLA_V:(e + 1) * MLA_V, lo:hi]
        return jnp.concatenate([vt, ones_rows[:, lo:hi]], axis=0)

    def consume(j, e):
        s = s_ref[e]
        m_prev = m_ref[e]
        m_new = jnp.maximum(m_prev, jnp.max(s, axis=0, keepdims=True))
        alpha = jnp.exp2(m_prev - m_new)
        p = jnp.exp2(s - m_new).astype(BF16)
        acc_ref[e] = alpha * acc_ref[e] + _dot(values(j, e, 0, tile), p)
        m_ref[e] = m_new

    def consume_diag(e):
        s = jnp.where(causal_top, s_ref[e, 0:half, :], MASK_VALUE)
        m_prev = m_ref[e]
        m_new = jnp.maximum(m_prev, jnp.max(s, axis=0, keepdims=True))
        p = jnp.exp2(s - m_new).astype(BF16)
        acc = jnp.exp2(m_prev - m_new) * acc_ref[e] + _dot(values(qi, e, 0, half), p)
        s2 = jnp.where(causal_sq, s_ref[e, half:tile, half:tile], MASK_VALUE)
        m2_prev = m_new[:, half:tile]
        m2_new = jnp.maximum(m2_prev, jnp.max(s2, axis=0, keepdims=True))
        p2 = jnp.exp2(s2 - m2_new).astype(BF16)
        acc2 = (jnp.exp2(m2_prev - m2_new) * acc[:, half:tile]
                + _dot(values(qi, e, half, tile), p2))
        acc = jnp.concatenate([acc[:, 0:half], acc2], axis=1)
        return acc[0:MLA_V, :] / acc[MLA_V:MLA_V + 1, :]

    nxt = jnp.minimum(qi + 1, nt - 1)

    @pl.when(qi == 0)
    def _():
        for e in range(MLA_AHEAD):
            scores(0, e, 0)

    def blocks(j0, count):
        for t in range(count * MLA_HEADS):
            ahead = t + MLA_AHEAD
            scores(j0 + ahead // MLA_HEADS, ahead % MLA_HEADS, qi)
            consume(j0 + t // MLA_HEADS, t % MLA_HEADS)

    def body(i, carry):
        blocks(MLA_UNROLL * i, MLA_UNROLL)
        return carry

    lax.fori_loop(0, qi // MLA_UNROLL, body, 0)
    rem = qi % MLA_UNROLL
    size = MLA_UNROLL // 2
    while size >= 1:
        @pl.when((rem & size) != 0)
        def _(size=size):
            blocks(qi - (rem & (2 * size - 1)), size)
        size //= 2

    outs = []
    for e in range(MLA_HEADS):
        ahead = e + MLA_AHEAD
        if ahead < MLA_HEADS:
            scores_diag(ahead)
        else:
            scores(0, ahead - MLA_HEADS, nxt)
        outs.append(consume_diag(e))
    ot = jnp.concatenate(outs, axis=0)
    o_ref[0] = (ot.T * mg_ref[0]).astype(BF16)


def _mla_call(mqt, mk, mvt, mg, tile):
    B, nt, _, _ = mqt.shape
    S = nt * tile
    return pl.pallas_call(
        functools.partial(_mla_kernel, tile=tile),
        grid=(B, nt),
        in_specs=[
            pl.BlockSpec((1, nt, MLA_HEADS * MLA_SLOT, tile), lambda b, i: (b, 0, 0, 0)),
            pl.BlockSpec((1, S, MLA_HEADS * MLA_SLOT), lambda b, i: (b, 0, 0)),
            pl.BlockSpec((1, nt, MLA_WIDTH, tile), lambda b, i: (b, 0, 0, 0)),
            pl.BlockSpec((1, tile, MLA_WIDTH), lambda b, i: (b, i, 0)),
        ],
        out_specs=pl.BlockSpec((1, tile, MLA_WIDTH), lambda b, i: (b, i, 0)),
        out_shape=jax.ShapeDtypeStruct((B, S, MLA_WIDTH), BF16),
        scratch_shapes=[pltpu.VMEM((MLA_HEADS, 1, tile), F32),
                        pltpu.VMEM((MLA_HEADS, MLA_ACC_ROWS, tile), F32),
                        pltpu.VMEM((MLA_HEADS, tile, tile), F32)],
        compiler_params=pltpu.CompilerParams(
            dimension_semantics=("arbitrary", "arbitrary"),
            vmem_limit_bytes=VMEM_LIMIT_BYTES),
        name="mla",
    )(mqt, mk, mvt, mg)


def _post_kernel(h_ref, og_ref, oc_ref, om_ref, wout_ref, gx_ref, wq_ref, mk_ref, mv_ref,
                 wo_ref, gf_ref, o_ref, wout_bf, wq_bf, wo_bf, *, layer, final_norm):
    @pl.when((pl.program_id(0) == 0) & (pl.program_id(1) == 0))
    def _():
        wout_bf[...] = wout_ref[...].astype(BF16)
        wq_bf[...] = wq_ref[...].astype(BF16)
        wo_bf[...] = wo_ref[...].astype(BF16)

    x = jnp.concatenate([og_ref[0], oc_ref[0], om_ref[0]], axis=1)
    h1 = h_ref[0] + _dot(x, wout_bf[...])
    hn = _rms(h1, gx_ref[layer:layer + 1, :]).astype(BF16)
    q = (_dot(hn, wq_bf[...]) * (LOG2E / math.sqrt(MEM_HEAD_DIM))).astype(BF16)
    ones = jnp.ones((mv_ref.shape[0], MEM_HEAD_DIM), BF16)
    head_lanes = [slice(hd * MEM_HEAD_DIM, (hd + 1) * MEM_HEAD_DIM) for hd in range(MEM_HEADS)]
    scores = [_dot_nt(q[:, sl], mk_ref[:, sl]) for sl in head_lanes]
    heads = []
    for s, sl in zip(scores, head_lanes):
        p = jnp.exp2(s - jnp.max(s, axis=1, keepdims=True)).astype(BF16)
        nd = _dot(p, jnp.concatenate([mv_ref[:, sl], ones], axis=1))
        heads.append((nd[:, :MEM_HEAD_DIM] / nd[:, MEM_HEAD_DIM:]).astype(BF16))
    o = jnp.concatenate(heads, axis=1)
    h2 = h1 + _dot(o, wo_bf[...])
    if final_norm:
        h2 = _rms(h2, gf_ref[...])
    o_ref[0] = h2


def _post_call(h, og, oc, om, memk, memv, l, p, norm_final, tile, final_norm):
    B, S, D = h.shape
    nt = S // tile
    M = memk.shape[2]

    def tok(w):
        return pl.BlockSpec((1, tile, w), lambda b, t: (b, t, 0))

    def lay(shape):
        return pl.BlockSpec((None,) + shape, lambda b, t: (l,) + (0,) * len(shape))

    mem_spec = pl.BlockSpec((None, None, M, MEM_INNER), lambda b, t: (l, b, 0, 0))
    return pl.pallas_call(
        functools.partial(_post_kernel, layer=l, final_norm=final_norm),
        grid=(B, nt),
        in_specs=[tok(D), tok(GLA_WIDTH), tok(CONV_WIDTH), tok(MLA_WIDTH),
                  lay((D_MIX, D)), pl.BlockSpec(p["norm_xattn"].shape, lambda b, t: (0, 0)),
                  lay((D, MEM_INNER)), mem_spec, mem_spec,
                  lay((MEM_INNER, D)), pl.BlockSpec((1, D), lambda b, t: (0, 0))],
        out_specs=tok(D),
        out_shape=jax.ShapeDtypeStruct((B, S, D), F32),
        scratch_shapes=[pltpu.VMEM((D_MIX, D), BF16), pltpu.VMEM((D, MEM_INNER), BF16),
                        pltpu.VMEM((MEM_INNER, D), BF16)],
        compiler_params=pltpu.CompilerParams(
            dimension_semantics=("arbitrary", "arbitrary"),
            vmem_limit_bytes=VMEM_LIMIT_BYTES),
        name="post",
    )(h, og, oc, om, p["w_out"], p["norm_xattn"], p["mem_wq"], memk, memv, p["mem_wo"],
      norm_final)


def _pad_cols(w, width):
    return jnp.pad(w, [(0, 0)] * (w.ndim - 1) + [(0, width - w.shape[-1])])


def _prep_kernel(wt_ref, mem_ref, gm_ref, wk_ref, wv_ref, o_ref, k_ref, v_ref):
    @pl.when(pl.program_id(1) == 0)
    def _():
        B, M, D = mem_ref.shape
        g = gm_ref[pl.ds(pl.program_id(0), 1), :]
        memn = _rms(mem_ref[...].reshape(B * M, D), g).astype(BF16)
        k_ref[...] = _dot(memn, wk_ref[...].astype(BF16)).astype(BF16).reshape(B, M, MEM_INNER)
        v_ref[...] = _dot(memn, wv_ref[...].astype(BF16)).astype(BF16).reshape(B, M, MEM_INNER)

    cols = wt_ref.shape[2]
    o_gv = 2 * GLA_QK
    o_glr = o_gv + GLA_WIDTH
    o_gg = o_glr + GLA_GATE_RANK
    o_cc = o_gg + GLA_WIDTH
    o_kr = o_cc + 4 * CONV_WIDTH + MLA_Q_RANK + MLA_KV_RANK
    o_mg = o_kr + MLA_ROPE
    zeros = lambda n: jnp.zeros((n, cols), F32)

    def rows(a, b):
        return wt_ref[0, a:b, :]

    pieces = [
        (SEG_Q, [rows(0, GLA_QK), zeros(GLA_QK_PAD - GLA_QK)]),
        (SEG_K, [rows(GLA_QK, o_gv), zeros(GLA_QK_PAD - GLA_QK)]),
        (SEG_V, [rows(o_gv, o_glr)]),
        (SEG_GG, [rows(o_gg, o_cc)]),
        (SEG_CC, [rows(o_cc, o_kr)]),
        (SEG_MG, [rows(o_mg, o_mg + MLA_WIDTH)]),
        (SEG_KR, [zeros(KR_ROPE_OFF), rows(o_kr, o_mg), rows(o_glr, o_gg),
                  zeros(MLA_SLOT - KR_GATE_OFF - GLA_GATE_RANK)]),
    ]
    for off, parts in pieces:
        blk = parts[0] if len(parts) == 1 else jnp.concatenate(parts, axis=0)
        o_ref[0, :, off:off + blk.shape[0]] = blk.T.astype(BF16)


def _prep_call(w_in, mem, norm_mem, wk, wv):
    wt = jnp.swapaxes(w_in, 1, 2)
    L, W, D = wt.shape
    B, M, _ = mem.shape
    cols = 512
    kv_shape = jax.ShapeDtypeStruct((L, B, M, MEM_INNER), BF16)
    kv_spec = pl.BlockSpec((None, B, M, MEM_INNER), lambda l, c: (l, 0, 0, 0))
    return pl.pallas_call(
        _prep_kernel,
        grid=(L, D // cols),
        in_specs=[pl.BlockSpec((1, W, cols), lambda l, c: (l, 0, c)),
                  pl.BlockSpec((B, M, D), lambda l, c: (0, 0, 0)),
                  pl.BlockSpec((L, D), lambda l, c: (0, 0)),
                  pl.BlockSpec((None, D, MEM_INNER), lambda l, c: (l, 0, 0)),
                  pl.BlockSpec((None, D, MEM_INNER), lambda l, c: (l, 0, 0))],
        out_specs=[pl.BlockSpec((1, cols, IN_PAD), lambda l, c: (l, c, 0)), kv_spec, kv_spec],
        out_shape=[jax.ShapeDtypeStruct((L, D, IN_PAD), BF16), kv_shape, kv_shape],
        compiler_params=pltpu.CompilerParams(
            dimension_semantics=("arbitrary", "arbitrary"),
            vmem_limit_bytes=VMEM_LIMIT_BYTES),
        name="prep",
    )(wt, mem, norm_mem, wk, wv)


def _prep_params(norm_mix, w_in_packed, gla_w_gate, gla_b_gate, gla_norm, conv_w, mla_q_norm, mla_w_uq,
                 mla_kv_norm, mla_w_ukv, w_out, norm_xattn, mem_wq, mem_wo):
    L = w_in_packed.shape[0]
    w_uq = mla_w_uq.reshape(L, MLA_Q_RANK, MLA_HEADS, MLA_NOPE + MLA_ROPE)
    half = lambda lo: _pad_cols(
        w_uq[..., lo:lo + ROPE_HALF].reshape(L, MLA_Q_RANK, MLA_HEADS * ROPE_HALF), MLA_SLOT)
    w_uq = jnp.concatenate(
        [w_uq[..., :MLA_NOPE].reshape(L, MLA_Q_RANK, MLA_Q_NOPE_W), half(MLA_NOPE),
         half(MLA_NOPE + ROPE_HALF)], axis=-1)
    w_ukv = mla_w_ukv.reshape(L, MLA_KV_RANK, MLA_HEADS, MLA_NOPE + MLA_V)
    w_uk = _pad_cols(w_ukv[..., :MLA_NOPE], MLA_SLOT).reshape(L, MLA_KV_RANK, MLA_HEADS * MLA_SLOT)
    w_uv = w_ukv[..., MLA_NOPE:].reshape(L, MLA_KV_RANK, MLA_WIDTH)
    return {
        "norm_mix": norm_mix,
        "w_in": w_in_packed,
        "wg": gla_w_gate,
        "bg": gla_b_gate,
        "gla_norm": gla_norm,
        "conv_w": conv_w,
        "q_norm": mla_q_norm,
        "w_uq": w_uq.astype(BF16),
        "kv_norm": mla_kv_norm,
        "w_ukv": jnp.concatenate([w_uk, w_uv], axis=-1).astype(BF16),
        "w_out": w_out,
        "norm_xattn": norm_xattn,
        "mem_wq": mem_wq,
        "mem_wo": mem_wo,
    }


def _rope_tables(positions):
    inv_freq = 1.0 / (ROPE_BASE ** (jnp.arange(0, MLA_ROPE, 2, dtype=F32) / MLA_ROPE))
    ang = positions.astype(F32)[:, None, :] * inv_freq[None, :, None]
    return jnp.cos(ang), jnp.sin(ang)


def _pick_tile(S, want):
    t = min(S, want)
    assert S % t == 0 and t % GLA_CHUNK == 0
    return t


def kernel(x, mem, positions, norm_mix, w_in, gla_w_gate, gla_b_gate, gla_norm, conv_w, mla_q_norm,
           mla_w_uq, mla_kv_norm, mla_w_ukv, w_out, norm_xattn, norm_mem, mem_wq, mem_wk, mem_wv,
           mem_wo, norm_final):
    depth = w_in.shape[0]
    S = x.shape[1]
    attn_tile = _pick_tile(S, 512)
    tile = _pick_tile(S, PROJ_TILE)
    w_in_packed, memk, memv = _prep_call(w_in, mem, norm_mem, mem_wk, mem_wv)
    p = _prep_params(norm_mix, w_in_packed, gla_w_gate, gla_b_gate, gla_norm, conv_w, mla_q_norm,
                     mla_w_uq, mla_kv_norm, mla_w_ukv, w_out, norm_xattn, mem_wq, mem_wo)
    tables = _rope_tables(positions)
    nf = norm_final[None, :]
    h = x
    for l in range(depth):
        og, oc, mq, mk, mv, mg = _proj_call(h, l, p, tables, tile, attn_tile)
        om = _mla_call(mq, mk, mv, mg, attn_tile)
        h = _post_call(h, og, oc, om, memk, memv, l, p, nf, tile, l == depth - 1)
    return h
```

```python
import functools
import math

import jax
import jax.numpy as jnp
from jax import lax
from jax.experimental import pallas as pl
from jax.experimental.pallas import tpu as pltpu

F32 = jnp.float32
BF16 = jnp.bfloat16

D_MODEL = 1024
EPS = 1e-6

GLA_HEADS = 4
GLA_DV = 96
GLA_DK = 48
GLA_QK = GLA_HEADS * GLA_DK
GLA_WIDTH = GLA_HEADS * GLA_DV
GLA_GATE_RANK = 16
GLA_TAU = 16.0
GLA_CHUNK = 64
GLA_QK_PAD = 256

CONV_WIDTH = 256
CONV_K = 3

MLA_HEADS = 6
MLA_NOPE = 64
MLA_ROPE = 32
MLA_V = 64
MLA_Q_RANK = 256
MLA_KV_RANK = 256
MLA_WIDTH = MLA_HEADS * MLA_V
MLA_SLOT = 128
MLA_Q_NOPE_W = MLA_HEADS * MLA_NOPE
MLA_ACC_ROWS = MLA_V + 16
MLA_AHEAD = 2
MLA_UNROLL = 4
ROPE_BASE = 10000.0
ROPE_HALF = MLA_ROPE // 2

D_MIX = GLA_WIDTH + CONV_WIDTH + MLA_WIDTH

MEM_HEADS = 4
MEM_HEAD_DIM = 128
MEM_INNER = MEM_HEADS * MEM_HEAD_DIM

LOG2E = math.log2(math.e)
MASK_VALUE = -1e30

SEG_Q = 0
SEG_K = SEG_Q + GLA_QK_PAD
SEG_V = SEG_K + GLA_QK_PAD
SEG_GG = SEG_V + GLA_WIDTH
SEG_CC = SEG_GG + GLA_WIDTH
SEG_CB = SEG_CC + CONV_WIDTH
SEG_CH = SEG_CB + CONV_WIDTH
SEG_CG = SEG_CH + CONV_WIDTH
SEG_CQ = SEG_CG + CONV_WIDTH
SEG_CKV = SEG_CQ + MLA_Q_RANK
SEG_MG = SEG_CKV + MLA_KV_RANK
SEG_KR = SEG_MG + MLA_WIDTH
IN_PAD = SEG_KR + MLA_SLOT
KR_ROPE_OFF = MLA_NOPE
KR_GATE_OFF = MLA_NOPE + MLA_ROPE

VMEM_LIMIT_BYTES = 56 * 1024 * 1024
PROJ_TILE = 1024
PIECE_N = 256
GLA_FILL_STAGES = 1


def _silu(x):
    return x * (1.0 / (1.0 + jnp.exp(-x)))


def _rms(x, g):
    return x * lax.rsqrt(jnp.mean(x * x, axis=-1, keepdims=True) + EPS) * g


def _dot(a, b):
    return jnp.dot(a, b, preferred_element_type=F32)


def _dot_nt(a, b):
    return lax.dot_general(a, b, (((1,), (1,)), ((), ())), preferred_element_type=F32)


def _dot_tn(a, b):
    return lax.dot_general(a, b, (((0,), (0,)), ((), ())), preferred_element_type=F32)


def _gla_tile(q, k, la, v, gate, gn, st_ref, og_ref, fill):
    C = GLA_CHUNK
    tile = q.shape[0]
    lane_k = lax.broadcasted_iota(jnp.int32, (1, GLA_QK_PAD), 1)
    lane_v = lax.broadcasted_iota(jnp.int32, (1, GLA_WIDTH), 1)
    kmask = [(lane_k >= h * GLA_DK) & (lane_k < (h + 1) * GLA_DK) for h in range(GLA_HEADS)]
    vmask = [(lane_v >= h * GLA_DV) & (lane_v < (h + 1) * GLA_DV) for h in range(GLA_HEADS)]
    row = lax.broadcasted_iota(jnp.int32, (C, 2 * C), 0)
    col = lax.broadcasted_iota(jnp.int32, (C, 2 * C), 1)
    cum2 = (row >= (col & (C - 1))).astype(BF16)
    hrow = lax.broadcasted_iota(jnp.int32, (C, GLA_HEADS * C), 0)
    hcol = lax.broadcasted_iota(jnp.int32, (C, GLA_HEADS * C), 1)
    tril_heads = hrow >= (hcol & (C - 1))
    srow = lax.broadcasted_iota(jnp.int32, (GLA_WIDTH, GLA_QK_PAD), 0)
    slane = lax.broadcasted_iota(jnp.int32, (GLA_WIDTH, GLA_QK_PAD), 1)
    same_head = None
    for h in range(GLA_HEADS):
        blk = ((srow >= h * GLA_DV) & (srow < (h + 1) * GLA_DV)
               & (slane >= h * GLA_DK) & (slane < (h + 1) * GLA_DK))
        same_head = blk if same_head is None else (same_head | blk)

    n_chunks = tile // C
    cums = []
    for c in range(n_chunks):
        lac = la[c * C:(c + 1) * C]
        la_hi = lac.astype(BF16)
        la_lo = (lac - la_hi.astype(F32)).astype(BF16)
        cums.append(_dot(cum2, jnp.concatenate([la_hi, la_lo], axis=0)))
    q_decs, intras, upds, decays = [], [], [], []
    for c in range(n_chunks):
        sl = slice(c * C, (c + 1) * C)
        qc, kc, vc = q[sl], k[sl], v[sl]
        b = cums[c]
        b_last = b[C - 1:C, :]
        q_dec = (qc * (GLA_DK ** -0.5) * jnp.exp(b)).astype(BF16)
        k_inv = kc * jnp.exp(-b)
        k_end = (kc * jnp.exp(b_last - b)).astype(BF16)
        k_heads = jnp.concatenate(
            [jnp.where(kmask[h], k_inv, 0.0) for h in range(GLA_HEADS)], axis=0).astype(BF16)
        a = _dot_nt(q_dec, k_heads)
        a = jnp.where(tril_heads, a, 0.0).astype(BF16)
        v_heads = jnp.concatenate(
            [jnp.where(vmask[h], vc, 0.0) for h in range(GLA_HEADS)], axis=0).astype(BF16)
        intras.append(_dot(a, v_heads))
        upds.append(_dot_tn(vc.astype(BF16), k_end))
        q_decs.append(q_dec)
        decays.append(jnp.exp(b_last))
    st = st_ref[...]
    for c in range(n_chunks):
        sl = slice(c * C, (c + 1) * C)
        o = intras[c] + _dot_nt(q_decs[c], st.astype(BF16))
        fill()
        st = st * decays[c] + jnp.where(same_head, upds[c], 0.0)
        o2 = o * o
        ms = jnp.zeros_like(o)
        for h in range(GLA_HEADS):
            mh = jnp.sum(jnp.where(vmask[h], o2, 0.0), axis=1, keepdims=True) * (1.0 / GLA_DV)
            ms = jnp.where(vmask[h], mh, ms)
        y = o * lax.rsqrt(ms + EPS) * gn
        og_ref[0, sl, :] = (y * gate[sl]).astype(BF16)
    st_ref[...] = st


def _proj_kernel(h_ref, g_ref, win_ref, wg_ref, bg_ref, gn_ref, convw_ref, qn_ref, wuq_ref,
                 kvn_ref, wukv_ref, cos_ref, sin_ref,
                 og_ref, oconv_ref, mq_ref, mk_ref, mv_ref, mg_ref,
                 ubuf_ref, st_ref, *, layer, tile, attn_tile, q_scale):
    t = pl.program_id(1)

    @pl.when(t == 0)
    def _():
        ubuf_ref[0:8, :] = jnp.zeros((8, CONV_WIDTH), F32)
        st_ref[...] = jnp.zeros(st_ref.shape, F32)

    x = h_ref[0]
    row = slice(layer, layer + 1)
    xn = _rms(x, g_ref[row, :]).astype(BF16)

    def seg(a, b):
        return _dot(xn, win_ref[:, a:b])

    val = {}
    work = []

    def queue_segment(name, a, b):
        parts = []
        for n0 in range(a, b, PIECE_N):
            work.append((D_MODEL / PIECE_N, lambda n0=n0: parts.append(seg(n0, min(n0 + PIECE_N, b)))))
        work.append((0.0, lambda: val.update({name: jnp.concatenate(parts, axis=1)})))

    def q_up():
        cqn = _rms(val["cq"], qn_ref[row, :]).astype(BF16)
        val["qn"] = _dot(cqn, wuq_ref[:, 0:MLA_Q_NOPE_W]) * q_scale
        val["qr"] = _dot(cqn, wuq_ref[:, MLA_Q_NOPE_W:])

    kv_parts = []
    kv_width = MLA_HEADS * MLA_SLOT + MLA_WIDTH

    def kv_piece(n0):
        if n0 == 0:
            val["ckvn"] = _rms(val["ckv"], kvn_ref[row, :]).astype(BF16)
        kv_parts.append(_dot(val["ckvn"], wukv_ref[:, n0:min(n0 + PIECE_N, kv_width)]))

    queue_segment("cq", SEG_CQ, SEG_CKV)
    queue_segment("ckv", SEG_CKV, SEG_MG)
    work.append(((MLA_Q_NOPE_W + 2 * MLA_SLOT) / PIECE_N, q_up))
    for n0 in range(0, kv_width, PIECE_N):
        work.append((1.0, functools.partial(kv_piece, n0)))
    queue_segment("cc", SEG_CC, SEG_CB)
    queue_segment("ch", SEG_CH, SEG_CG)
    queue_segment("cb", SEG_CB, SEG_CH)
    queue_segment("cg", SEG_CG, SEG_CQ)
    queue_segment("mg", SEG_MG, SEG_KR)
    work.reverse()
    total_cost = sum(c for c, _ in work)
    slots = GLA_FILL_STAGES * (tile // GLA_CHUNK)
    progress = {"slot": 0, "cost": 0.0}

    def fill():
        progress["slot"] += 1
        target = total_cost * progress["slot"] / slots
        while work and progress["cost"] + 0.5 * work[-1][0] <= target:
            cost, thunk = work.pop()
            progress["cost"] += cost
            thunk()

    kr = seg(SEG_KR, IN_PAD)
    gq, gk = seg(SEG_Q, SEG_K), seg(SEG_K, SEG_V)
    wg = jnp.concatenate(
        [jnp.zeros((KR_GATE_OFF, GLA_QK_PAD), F32),
         jnp.concatenate([wg_ref[...], jnp.zeros((GLA_GATE_RANK, GLA_QK_PAD - GLA_QK), F32)], axis=1),
         jnp.zeros((MLA_SLOT - KR_GATE_OFF - GLA_GATE_RANK, GLA_QK_PAD), F32)], axis=0)
    bg = jnp.concatenate([bg_ref[row, :], jnp.zeros((1, GLA_QK_PAD - GLA_QK), F32)], axis=1)
    gn = jnp.concatenate([gn_ref[row, :]] * GLA_HEADS, axis=1)
    z = _dot(kr.astype(BF16), wg.astype(BF16)) + bg
    gv, gg = seg(SEG_V, SEG_GG), seg(SEG_GG, SEG_CC)
    la = (jnp.minimum(z, 0.0) - jnp.log1p(jnp.exp(-jnp.abs(z)))) * (1.0 / GLA_TAU)
    _gla_tile(gq, gk, la, gv, _silu(gg), gn, st_ref, og_ref, fill)
    while work:
        work.pop()[1]()

    u = val["cc"] * val["ch"]
    ubuf_ref[8:8 + tile, :] = u
    cw = convw_ref[...]
    conv = (cw[0:1, :] * ubuf_ref[6:6 + tile, :] + cw[1:2, :] * ubuf_ref[7:7 + tile, :]
            + cw[2:3, :] * u)
    oconv_ref[0] = (val["cb"] * conv * _silu(val["cg"])).astype(BF16)
    ubuf_ref[0:8, :] = u[tile - 8:tile, :]

    reps = MLA_SLOT // ROPE_HALF
    cos = jnp.concatenate([cos_ref[0]] * reps, axis=0).T
    sin = jnp.concatenate([sin_ref[0]] * reps, axis=0).T
    qn, qr = val["qn"], val["qr"]
    x1, x2 = qr[:, 0:MLA_SLOT], qr[:, MLA_SLOT:]
    cq, sq = cos * q_scale, sin * q_scale
    r1 = x1 * cq - x2 * sq
    r2 = x2 * cq + x1 * sq
    pad_rows = jnp.zeros((MLA_SLOT - MLA_NOPE - MLA_ROPE, attn_tile), BF16)
    for i in range(tile // attn_tile):
        rows = slice(i * attn_tile, (i + 1) * attn_tile)
        qnt = qn[rows].T.astype(BF16)
        r1t = r1[rows].T.astype(BF16)
        r2t = r2[rows].T.astype(BF16)
        for h in range(MLA_HEADS):
            base = h * MLA_SLOT
            mq_ref[0, i, base:base + MLA_NOPE, :] = qnt[h * MLA_NOPE:(h + 1) * MLA_NOPE]
            mq_ref[0, i, base + MLA_NOPE:base + MLA_NOPE + ROPE_HALF, :] = (
                r1t[h * ROPE_HALF:(h + 1) * ROPE_HALF])
            mq_ref[0, i, base + MLA_NOPE + ROPE_HALF:base + MLA_NOPE + MLA_ROPE, :] = (
                r2t[h * ROPE_HALF:(h + 1) * ROPE_HALF])
            mq_ref[0, i, base + MLA_NOPE + MLA_ROPE:base + MLA_SLOT, :] = pad_rows

    kvh = jnp.concatenate(kv_parts, axis=1)
    lane = lax.broadcasted_iota(jnp.int32, (1, MLA_SLOT), 1)
    first = (lane >= KR_ROPE_OFF) & (lane < KR_ROPE_OFF + ROPE_HALF)
    second = (lane >= KR_ROPE_OFF + ROPE_HALF) & (lane < KR_ROPE_OFF + MLA_ROPE)
    krope = (kr * jnp.where(first | second, cos, 0.0)
             + pltpu.roll(kr, ROPE_HALF, 1) * jnp.where(second, sin, 0.0)
             - pltpu.roll(kr, MLA_SLOT - ROPE_HALF, 1) * jnp.where(first, sin, 0.0))
    for h in range(MLA_HEADS):
        sl = slice(h * MLA_SLOT, (h + 1) * MLA_SLOT)
        mk_ref[0, :, sl] = (kvh[:, sl] + krope).astype(BF16)
    vals = kvh[:, MLA_HEADS * MLA_SLOT:]
    for i in range(tile // attn_tile):
        mv_ref[0, i] = vals[i * attn_tile:(i + 1) * attn_tile].T.astype(BF16)
    mg_ref[0] = _silu(val["mg"])


def _proj_call(h, l, p, tables, tile, attn_tile):
    B, S, D = h.shape
    nt = S // tile
    sub = tile // attn_tile
    q_scale = LOG2E / math.sqrt(MLA_NOPE + MLA_ROPE)

    def tok(w):
        return pl.BlockSpec((1, tile, w), lambda b, t: (b, t, 0))

    def lay(shape):
        return pl.BlockSpec((None,) + shape, lambda b, t: (l,) + (0,) * len(shape))

    def stacked(w):
        return pl.BlockSpec((depth, w), lambda b, t: (0, 0))

    depth = p["w_in"].shape[0]
    in_specs = [
        tok(D),
        stacked(D),
        lay((D, IN_PAD)),
        lay((GLA_GATE_RANK, GLA_QK)),
        stacked(GLA_QK),
        stacked(GLA_DV),
        lay((CONV_K, CONV_WIDTH)),
        stacked(MLA_Q_RANK),
        lay((MLA_Q_RANK, MLA_Q_NOPE_W + 2 * MLA_SLOT)),
        stacked(MLA_KV_RANK),
        lay((MLA_KV_RANK, MLA_HEADS * MLA_SLOT + MLA_WIDTH)),
        pl.BlockSpec((1, ROPE_HALF, tile), lambda b, t: (b, 0, t)),
        pl.BlockSpec((1, ROPE_HALF, tile), lambda b, t: (b, 0, t)),
    ]

    def tok_out(w, dt):
        return tok(w), jax.ShapeDtypeStruct((B, S, w), dt)

    def tile_t_out(w, dt):
        return (pl.BlockSpec((1, sub, w, attn_tile), lambda b, t: (b, t, 0, 0)),
                jax.ShapeDtypeStruct((B, S // attn_tile, w, attn_tile), dt))

    outs = [
        tok_out(GLA_WIDTH, BF16), tok_out(CONV_WIDTH, BF16),
        tile_t_out(MLA_HEADS * MLA_SLOT, BF16), tok_out(MLA_HEADS * MLA_SLOT, BF16),
        tile_t_out(MLA_WIDTH, BF16), tok_out(MLA_WIDTH, F32),
    ]
    return pl.pallas_call(
        functools.partial(_proj_kernel, layer=l, tile=tile, attn_tile=attn_tile, q_scale=q_scale),
        grid=(B, nt),
        in_specs=in_specs,
        out_specs=[spec for spec, _ in outs],
        out_shape=[shape for _, shape in outs],
        scratch_shapes=[pltpu.VMEM((tile + 8, CONV_WIDTH), F32),
                        pltpu.VMEM((GLA_WIDTH, GLA_QK_PAD), F32)],
        compiler_params=pltpu.CompilerParams(
            dimension_semantics=("arbitrary", "arbitrary"),
            vmem_limit_bytes=VMEM_LIMIT_BYTES),
        name="proj",
    )(h, p["norm_mix"], p["w_in"], p["wg"], p["bg"], p["gla_norm"], p["conv_w"], p["q_norm"],
      p["w_uq"], p["kv_norm"], p["w_ukv"], *tables)


def _mla_kernel(qt_ref, k_ref, vt_ref, mg_ref, o_ref, m_ref, acc_ref, s_ref, *, tile):
    qi = pl.program_id(1)
    nt = pl.num_programs(1)
    half = tile // 2
    krow = lax.broadcasted_iota(jnp.int32, (half, tile), 0)
    qcol = lax.broadcasted_iota(jnp.int32, (half, tile), 1)
    causal_top = krow <= qcol
    causal_sq = causal_top[:, 0:half]
    ones_rows = jnp.ones((MLA_ACC_ROWS - MLA_V, tile), BF16)

    m_ref[...] = jnp.full(m_ref.shape, MASK_VALUE, F32)
    acc_ref[...] = jnp.zeros(acc_ref.shape, F32)

    def head(e):
        return slice(e * MLA_SLOT, (e + 1) * MLA_SLOT)

    def scores(j, e, q_tile):
        rows = pl.ds(pl.multiple_of(j * tile, tile), tile)
        s_ref[e] = _dot(k_ref[0, rows, head(e)], qt_ref[0, q_tile, head(e), :])

    def scores_diag(e):
        top = pl.ds(pl.multiple_of(qi * tile, tile), half)
        bot = pl.ds(pl.multiple_of(qi * tile + half, half), half)
        s_ref[e, 0:half, :] = _dot(k_ref[0, top, head(e)], qt_ref[0, qi, head(e), :])
        s_ref[e, half:tile, half:tile] = _dot(k_ref[0, bot, head(e)],
                                              qt_ref[0, qi, head(e), half:tile])

    def values(j, e, lo, hi):
        vt = vt_ref[0, j, e * MLA_V:(e + 1) * MLA_V, lo:hi]
        return jnp.concatenate([vt, ones_rows[:, lo:hi]], axis=0)

    def consume(j, e):
        s = s_ref[e]
        m_prev = m_ref[e]
        m_new = jnp.maximum(m_prev, jnp.max(s, axis=0, keepdims=True))
        alpha = jnp.exp2(m_prev - m_new)
        p = jnp.exp2(s - m_new).astype(BF16)
        acc_ref[e] = alpha * acc_ref[e] + _dot(values(j, e, 0, tile), p)
        m_ref[e] = m_new

    def consume_diag(e):
        s = jnp.where(causal_top, s_ref[e, 0:half, :], MASK_VALUE)
        m_prev = m_ref[e]
        m_new = jnp.maximum(m_prev, jnp.max(s, axis=0, keepdims=True))
        p = jnp.exp2(s - m_new).astype(BF16)
        acc = jnp.exp2(m_prev - m_new) * acc_ref[e] + _dot(values(qi, e, 0, half), p)
        s2 = jnp.where(causal_sq, s_ref[e, half:tile, half:tile], MASK_VALUE)
        m2_prev = m_new[:, half:tile]
        m2_new = jnp.maximum(m2_prev, jnp.max(s2, axis=0, keepdims=True))
        p2 = jnp.exp2(s2 - m2_new).astype(BF16)
        acc2 = (jnp.exp2(m2_prev - m2_new) * acc[:, half:tile]
                + _dot(values(qi, e, half, tile), p2))
        acc = jnp.concatenate([acc[:, 0:half], acc2], axis=1)
        return acc[0:MLA_V, :] / acc[MLA_V:MLA_V + 1, :]

    nxt = jnp.minimum(qi + 1, nt - 1)

    @pl.when(qi == 0)
    def _():
        for e in range(MLA_AHEAD):
            scores(0, e, 0)

    def blocks(j0, count):
        for t in range(count * MLA_HEADS):
            ahead = t + MLA_AHEAD
            scores(j0 + ahead // MLA_HEADS, ahead % MLA_HEADS, qi)
            consume(j0 + t // MLA_HEADS, t % MLA_HEADS)

    def body(i, carry):
        blocks(MLA_UNROLL * i, MLA_UNROLL)
        return carry

    lax.fori_loop(0, qi // MLA_UNROLL, body, 0)
    rem = qi % MLA_UNROLL
    size = MLA_UNROLL // 2
    while size >= 1:
        @pl.when((rem & size) != 0)
        def _(size=size):
            blocks(qi - (rem & (2 * size - 1)), size)
        size //= 2

    outs = []
    for e in range(MLA_HEADS):
        ahead = e + MLA_AHEAD
        if ahead < MLA_HEADS:
            scores_diag(ahead)
        else:
            scores(0, ahead - MLA_HEADS, nxt)
        outs.append(consume_diag(e))
    ot = jnp.concatenate(outs, axis=0)
    o_ref[0] = (ot.T * mg_ref[0]).astype(BF16)


def _mla_call(mqt, mk, mvt, mg, tile):
    B, nt, _, _ = mqt.shape
    S = nt * tile
    return pl.pallas_call(
        functools.partial(_mla_kernel, tile=tile),
        grid=(B, nt),
        in_specs=[
            pl.BlockSpec((1, nt, MLA_HEADS * MLA_SLOT, tile), lambda b, i: (b, 0, 0, 0)),
            pl.BlockSpec((1, S, MLA_HEADS * MLA_SLOT), lambda b, i: (b, 0, 0)),
            pl.BlockSpec((1, nt, MLA_WIDTH, tile), lambda b, i: (b, 0, 0, 0)),
            pl.BlockSpec((1, tile, MLA_WIDTH), lambda b, i: (b, i, 0)),
        ],
        out_specs=pl.BlockSpec((1, tile, MLA_WIDTH), lambda b, i: (b, i, 0)),
        out_shape=jax.ShapeDtypeStruct((B, S, MLA_WIDTH), BF16),
        scratch_shapes=[pltpu.VMEM((MLA_HEADS, 1, tile), F32),
                        pltpu.VMEM((MLA_HEADS, MLA_ACC_ROWS, tile), F32),
                        pltpu.VMEM((MLA_HEADS, tile, tile), F32)],
        compiler_params=pltpu.CompilerParams(
            dimension_semantics=("arbitrary", "arbitrary"),
            vmem_limit_bytes=VMEM_LIMIT_BYTES),
        name="mla",
    )(mqt, mk, mvt, mg)


def _post_kernel(h_ref, og_ref, oc_ref, om_ref, wout_ref, gx_ref, wq_ref, mk_ref, mv_ref,
                 wo_ref, gf_ref, o_ref, wout_bf, wq_bf, wo_bf, *, layer, final_norm):
    @pl.when((pl.program_id(0) == 0) & (pl.program_id(1) == 0))
    def _():
        wout_bf[...] = wout_ref[...].astype(BF16)
        wq_bf[...] = wq_ref[...].astype(BF16)
        wo_bf[...] = wo_ref[...].astype(BF16)

    x = jnp.concatenate([og_ref[0], oc_ref[0], om_ref[0]], axis=1)
    h1 = h_ref[0] + _dot(x, wout_bf[...])
    hn = _rms(h1, gx_ref[layer:layer + 1, :]).astype(BF16)
    q = (_dot(hn, wq_bf[...]) * (LOG2E / math.sqrt(MEM_HEAD_DIM))).astype(BF16)
    ones = jnp.ones((mv_ref.shape[0], MEM_HEAD_DIM), BF16)
    head_lanes = [slice(hd * MEM_HEAD_DIM, (hd + 1) * MEM_HEAD_DIM) for hd in range(MEM_HEADS)]
    scores = [_dot_nt(q[:, sl], mk_ref[:, sl]) for sl in head_lanes]
    heads = []
    for s, sl in zip(scores, head_lanes):
        p = jnp.exp2(s - jnp.max(s, axis=1, keepdims=True)).astype(BF16)
        nd = _dot(p, jnp.concatenate([mv_ref[:, sl], ones], axis=1))
        heads.append((nd[:, :MEM_HEAD_DIM] / nd[:, MEM_HEAD_DIM:]).astype(BF16))
    o = jnp.concatenate(heads, axis=1)
    h2 = h1 + _dot(o, wo_bf[...])
    if final_norm:
        h2 = _rms(h2, gf_ref[...])
    o_ref[0] = h2


def _post_call(h, og, oc, om, memk, memv, l, p, norm_final, tile, final_norm):
    B, S, D = h.shape
    nt = S // tile
    M = memk.shape[2]

    def tok(w):
        return pl.BlockSpec((1, tile, w), lambda b, t: (b, t, 0))

    def lay(shape):
        return pl.BlockSpec((None,) + shape, lambda b, t: (l,) + (0,) * len(shape))

    mem_spec = pl.BlockSpec((None, None, M, MEM_INNER), lambda b, t: (l, b, 0, 0))
    return pl.pallas_call(
        functools.partial(_post_kernel, layer=l, final_norm=final_norm),
        grid=(B, nt),
        in_specs=[tok(D), tok(GLA_WIDTH), tok(CONV_WIDTH), tok(MLA_WIDTH),
                  lay((D_MIX, D)), pl.BlockSpec(p["norm_xattn"].shape, lambda b, t: (0, 0)),
                  lay((D, MEM_INNER)), mem_spec, mem_spec,
                  lay((MEM_INNER, D)), pl.BlockSpec((1, D), lambda b, t: (0, 0))],
        out_specs=tok(D),
        out_shape=jax.ShapeDtypeStruct((B, S, D), F32),
        scratch_shapes=[pltpu.VMEM((D_MIX, D), BF16), pltpu.VMEM((D, MEM_INNER), BF16),
                        pltpu.VMEM((MEM_INNER, D), BF16)],
        compiler_params=pltpu.CompilerParams(
            dimension_semantics=("arbitrary", "arbitrary"),
            vmem_limit_bytes=VMEM_LIMIT_BYTES),
        name="post",
    )(h, og, oc, om, p["w_out"], p["norm_xattn"], p["mem_wq"], memk, memv, p["mem_wo"],
      norm_final)


def _pad_cols(w, width):
    return jnp.pad(w, [(0, 0)] * (w.ndim - 1) + [(0, width - w.shape[-1])])


def _prep_kernel(wt_ref, mem_ref, gm_ref, wk_ref, wv_ref, o_ref, k_ref, v_ref):
    @pl.when(pl.program_id(1) == 0)
    def _():
        B, M, D = mem_ref.shape
        g = gm_ref[pl.ds(pl.program_id(0), 1), :]
        memn = _rms(mem_ref[...].reshape(B * M, D), g).astype(BF16)
        k_ref[...] = _dot(memn, wk_ref[...].astype(BF16)).astype(BF16).reshape(B, M, MEM_INNER)
        v_ref[...] = _dot(memn, wv_ref[...].astype(BF16)).astype(BF16).reshape(B, M, MEM_INNER)

    cols = wt_ref.shape[2]
    o_gv = 2 * GLA_QK
    o_glr = o_gv + GLA_WIDTH
    o_gg = o_glr + GLA_GATE_RANK
    o_cc = o_gg + GLA_WIDTH
    o_kr = o_cc + 4 * CONV_WIDTH + MLA_Q_RANK + MLA_KV_RANK
    o_mg = o_kr + MLA_ROPE
    zeros = lambda n: jnp.zeros((n, cols), F32)

    def rows(a, b):
        return wt_ref[0, a:b, :]

    pieces = [
        (SEG_Q, [rows(0, GLA_QK), zeros(GLA_QK_PAD - GLA_QK)]),
        (SEG_K, [rows(GLA_QK, o_gv), zeros(GLA_QK_PAD - GLA_QK)]),
        (SEG_V, [rows(o_gv, o_glr)]),
        (SEG_GG, [rows(o_gg, o_cc)]),
        (SEG_CC, [rows(o_cc, o_kr)]),
        (SEG_MG, [rows(o_mg, o_mg + MLA_WIDTH)]),
        (SEG_KR, [zeros(KR_ROPE_OFF), rows(o_kr, o_mg), rows(o_glr, o_gg),
                  zeros(MLA_SLOT - KR_GATE_OFF - GLA_GATE_RANK)]),
    ]
    for off, parts in pieces:
        blk = parts[0] if len(parts) == 1 else jnp.concatenate(parts, axis=0)
        o_ref[0, :, off:off + blk.shape[0]] = blk.T.astype(BF16)


def _prep_call(w_in, mem, norm_mem, wk, wv):
    wt = jnp.swapaxes(w_in, 1, 2)
    L, W, D = wt.shape
    B, M, _ = mem.shape
    cols = 512
    kv_shape = jax.ShapeDtypeStruct((L, B, M, MEM_INNER), BF16)
    kv_spec = pl.BlockSpec((None, B, M, MEM_INNER), lambda l, c: (l, 0, 0, 0))
    return pl.pallas_call(
        _prep_kernel,
        grid=(L, D // cols),
        in_specs=[pl.BlockSpec((1, W, cols), lambda l, c: (l, 0, c)),
                  pl.BlockSpec((B, M, D), lambda l, c: (0, 0, 0)),
                  pl.BlockSpec((L, D), lambda l, c: (0, 0)),
                  pl.BlockSpec((None, D, MEM_INNER), lambda l, c: (l, 0, 0)),
                  pl.BlockSpec((None, D, MEM_INNER), lambda l, c: (l, 0, 0))],
        out_specs=[pl.BlockSpec((1, cols, IN_PAD), lambda l, c: (l, c, 0)), kv_spec, kv_spec],
        out_shape=[jax.ShapeDtypeStruct((L, D, IN_PAD), BF16), kv_shape, kv_shape],
        compiler_params=pltpu.CompilerParams(
            dimension_semantics=("arbitrary", "arbitrary"),
            vmem_limit_bytes=VMEM_LIMIT_BYTES),
        name="prep",
    )(wt, mem, norm_mem, wk, wv)


def _prep_params(norm_mix, w_in_packed, gla_w_gate, gla_b_gate, gla_norm, conv_w, mla_q_norm, mla_w_uq,
                 mla_kv_norm, mla_w_ukv, w_out, norm_xattn, mem_wq, mem_wo):
    L = w_in_packed.shape[0]
    w_uq = mla_w_uq.reshape(L, MLA_Q_RANK, MLA_HEADS, MLA_NOPE + MLA_ROPE)
    half = lambda lo: _pad_cols(
        w_uq[..., lo:lo + ROPE_HALF].reshape(L, MLA_Q_RANK, MLA_HEADS * ROPE_HALF), MLA_SLOT)
    w_uq = jnp.concatenate(
        [w_uq[..., :MLA_NOPE].reshape(L, MLA_Q_RANK, MLA_Q_NOPE_W), half(MLA_NOPE),
         half(MLA_NOPE + ROPE_HALF)], axis=-1)
    w_ukv = mla_w_ukv.reshape(L, MLA_KV_RANK, MLA_HEADS, MLA_NOPE + MLA_V)
    w_uk = _pad_cols(w_ukv[..., :MLA_NOPE], MLA_SLOT).reshape(L, MLA_KV_RANK, MLA_HEADS * MLA_SLOT)
    w_uv = w_ukv[..., MLA_NOPE:].reshape(L, MLA_KV_RANK, MLA_WIDTH)
    return {
        "norm_mix": norm_mix,
        "w_in": w_in_packed,
        "wg": gla_w_gate,
        "bg": gla_b_gate,
        "gla_norm": gla_norm,
        "conv_w": conv_w,
        "q_norm": mla_q_norm,
        "w_uq": w_uq.astype(BF16),
        "kv_norm": mla_kv_norm,
        "w_ukv": jnp.concatenate([w_uk, w_uv], axis=-1).astype(BF16),
        "w_out": w_out,
        "norm_xattn": norm_xattn,
        "mem_wq": mem_wq,
        "mem_wo": mem_wo,
    }


def _rope_tables(positions):
    inv_freq = 1.0 / (ROPE_BASE ** (jnp.arange(0, MLA_ROPE, 2, dtype=F32) / MLA_ROPE))
    ang = positions.astype(F32)[:, None, :] * inv_freq[None, :, None]
    return jnp.cos(ang), jnp.sin(ang)


def _pick_tile(S, want):
    t = min(S, want)
    assert S % t == 0 and t % GLA_CHUNK == 0
    return t


def kernel(x, mem, positions, norm_mix, w_in, gla_w_gate, gla_b_gate, gla_norm, conv_w, mla_q_norm,
           mla_w_uq, mla_kv_norm, mla_w_ukv, w_out, norm_xattn, norm_mem, mem_wq, mem_wk, mem_wv,
           mem_wo, norm_final):
    depth = w_in.shape[0]
    S = x.shape[1]
    attn_tile = _pick_tile(S, 512)
    tile = _pick_tile(S, PROJ_TILE)
    w_in_packed, memk, memv = _prep_call(w_in, mem, norm_mem, mem_wk, mem_wv)
    p = _prep_params(norm_mix, w_in_packed, gla_w_gate, gla_b_gate, gla_norm, conv_w, mla_q_norm,
                     mla_w_uq, mla_kv_norm, mla_w_ukv, w_out, norm_xattn, mem_wq, mem_wo)
    tables = _rope_tables(positions)
    nf = norm_final[None, :]
    h = x
    for l in range(depth):
        og, oc, mq, mk, mv, mg = _proj_call(h, l, p, tables, tile, attn_tile)
        om = _mla_call(mq, mk, mv, mg, attn_tile)
        h = _post_call(h, og, oc, om, memk, memv, l, p, nf, tile, l == depth - 1)
    return h
```
